```python
import jax, jax.numpy as jnp
from jax import lax
import numpy as np

D_MODEL = 1024
BATCH = 2
SEQ = 8192
DEPTH = 1
DEC_BATCH = 32
DEC_SEQ = 32
PAST_LEN = 2048

CHUNK = 64
N_HEADS_A = 8
HEAD_DIM = 64
WIDTH_A = N_HEADS_A * HEAD_DIM
IDX_HEADS = 4
IDX_DIM = 64
TOPK_MAX = 256
N_GROUPS_B = 4
GROUP_DIM_B = 128
WIDTH_B = N_GROUPS_B * GROUP_DIM_B
GMLP_CHUNK = 128
MIX_WIDTH = WIDTH_A + WIDTH_B
D_FF = 4 * D_MODEL
ROPE_THETA = 10000.0
EPS = 1e-6
Q_BLOCK = 128
SPLIT_SIZES = (WIDTH_A, WIDTH_A, WIDTH_A, IDX_HEADS * IDX_DIM, IDX_DIM, IDX_HEADS, WIDTH_B, WIDTH_B)
IN_WIDTH = 3 * WIDTH_A + IDX_HEADS * IDX_DIM + IDX_DIM + IDX_HEADS + 2 * WIDTH_B
SPLIT_POINTS = (WIDTH_A, 2 * WIDTH_A, 3 * WIDTH_A, 3 * WIDTH_A + IDX_HEADS * IDX_DIM, 3 * WIDTH_A + IDX_HEADS * IDX_DIM + IDX_DIM, 3 * WIDTH_A + IDX_HEADS * IDX_DIM + IDX_DIM + IDX_HEADS, 3 * WIDTH_A + IDX_HEADS * IDX_DIM + IDX_DIM + IDX_HEADS + WIDTH_B)

kernel_name = "hybrid_dsa_gmlp_streaming_step"


def rms_norm(x, g):
    xf = x.astype(jnp.float32)
    y = xf * lax.rsqrt(jnp.mean(xf * xf, axis=-1, keepdims=True) + EPS)
    return (y * g.astype(jnp.float32)).astype(x.dtype)


def layer_norm(x, g, b):
    xf = x.astype(jnp.float32)
    xc = xf - jnp.mean(xf, axis=-1, keepdims=True)
    y = xc * lax.rsqrt(jnp.mean(xc * xc, axis=-1, keepdims=True) + EPS)
    return (y * g.astype(jnp.float32) + b.astype(jnp.float32)).astype(x.dtype)


def rope(x, pos):
    half = x.shape[-1] // 2
    inv_freq = jnp.power(jnp.float32(ROPE_THETA), -jnp.arange(half, dtype=jnp.float32) / half)
    ang = pos.astype(jnp.float32)[:, None] * inv_freq[None, :]
    cos = jnp.cos(ang)[None, :, None, :]
    sin = jnp.sin(ang)[None, :, None, :]
    xf = x.astype(jnp.float32)
    x1, x2 = xf[..., :half], xf[..., half:]
    return jnp.concatenate([x1 * cos - x2 * sin, x2 * cos + x1 * sin], axis=-1).astype(x.dtype)


def chunk_mask(q_pos, k_pos):
    return k_pos[None, :] < ((q_pos // CHUNK + 1) * CHUNK)[:, None]


def dsa_attend(q, q_idx, w_idx, k_all, v_all, kidx_all, mask, topk):
    s = jnp.einsum('bqhd,bsd->bqhs', q_idx.astype(jnp.float32), kidx_all.astype(jnp.float32))
    score = jnp.einsum('bqhs,bqh->bqs', jax.nn.relu(s), w_idx.astype(jnp.float32))
    score = jnp.where(mask[None], score, -jnp.inf)
    top_val, top_idx = lax.top_k(score, topk)
    valid = jnp.isfinite(top_val)
    gather = jax.vmap(lambda a, i: a[i])
    k_sel = gather(k_all, top_idx)
    v_sel = gather(v_all, top_idx)
    logits = jnp.einsum('bqhd,bqkhd->bqhk', q.astype(jnp.float32), k_sel.astype(jnp.float32)) * (HEAD_DIM ** -0.5)
    logits = jnp.where(valid[:, :, None, :], logits, -jnp.inf)
    p = jax.nn.softmax(logits, axis=-1)
    out = jnp.einsum('bqhk,bqkhd->bqhd', p, v_sel.astype(jnp.float32))
    return out.astype(q.dtype)


def dsa_prompt(q, q_idx, w_idx, k, v, kidx, topk):
    B, T = q.shape[0], q.shape[1]
    nblk = T // Q_BLOCK
    k_pos = jnp.arange(T)

    def to_blocks(a):
        return jnp.swapaxes(a.reshape((B, nblk, Q_BLOCK) + a.shape[2:]), 0, 1)

    def one_block(args):
        qb, qib, wib, start = args
        mask = chunk_mask(start + jnp.arange(Q_BLOCK), k_pos)
        return dsa_attend(qb, qib, wib, k, v, kidx, mask, topk)

    out = lax.map(one_block, (to_blocks(q), to_blocks(q_idx), to_blocks(w_idx), jnp.arange(nblk) * Q_BLOCK))
    return jnp.swapaxes(out, 0, 1).reshape(q.shape)


def gmlp_spatial(u, vn, ws, bs):
    B, T = u.shape[0], u.shape[1]
    L = min(T, GMLP_CHUNK)
    nc = T // L
    ws_m = ws * jnp.tril(jnp.ones((GMLP_CHUNK, GMLP_CHUNK), ws.dtype))
    v5 = vn.reshape(B, nc, L, N_GROUPS_B, GROUP_DIM_B)
    mixed = jnp.einsum('gts,bcsgd->bctgd', ws_m[:, :L, :L], v5) + jnp.transpose(bs[:, :L])[None, None, :, :, None]
    return (u.reshape(B, nc, L, N_GROUPS_B, GROUP_DIM_B) * mixed).reshape(B, T, WIDTH_B)


def hybrid_layer(x, c, past_k, past_v, past_kidx, w_ada, b_ada, norm1_g, norm2_g, w_in, q_norm_g, k_norm_g, gmlp_ln_g, gmlp_ln_b, gmlp_ws, gmlp_bs, w_out, w_ff1, w_ff2):
    B, T, _ = x.shape
    past = 0 if past_k is None else past_k.shape[1]
    pos = past + jnp.arange(T)
    mod = (jax.nn.silu(c) @ w_ada + b_ada)[:, None, :]
    sh1, sc1, g1, sh2, sc2, g2 = jnp.split(mod, 6, axis=-1)

    h = rms_norm(x, norm1_g) * (1 + sc1) + sh1
    z = h @ w_in
    q, k, v, qi, ki, wi, u, vg = jnp.split(z, SPLIT_POINTS, axis=-1)
    q = rope(rms_norm(q.reshape(B, T, N_HEADS_A, HEAD_DIM), q_norm_g), pos)
    k = rope(rms_norm(k.reshape(B, T, N_HEADS_A, HEAD_DIM), k_norm_g), pos)
    v = v.reshape(B, T, N_HEADS_A, HEAD_DIM)
    qi = rope(qi.reshape(B, T, IDX_HEADS, IDX_DIM), pos)
    ki = rope(ki[:, :, None, :], pos)[:, :, 0, :]
    wi = wi * ((IDX_DIM * IDX_HEADS) ** -0.5)
    if past_k is None:
        attn = dsa_prompt(q, qi, wi, k, v, ki, min(TOPK_MAX, T // 4))
    else:
        k_all = jnp.concatenate([past_k, k], axis=1)
        v_all = jnp.concatenate([past_v, v], axis=1)
        kidx_all = jnp.concatenate([past_kidx, ki], axis=1)
        S = past + T
        mask = chunk_mask(pos, jnp.arange(S))
        attn = dsa_attend(q, qi, wi, k_all, v_all, kidx_all, mask, min(TOPK_MAX, S // 4))
    u = jax.nn.gelu(u)
    vn = layer_norm(jax.nn.gelu(vg), gmlp_ln_g, gmlp_ln_b)
    gm = gmlp_spatial(u, vn, gmlp_ws, gmlp_bs)

    y = jnp.concatenate([attn.reshape(B, T, WIDTH_A), gm], axis=-1) @ w_out
    x = x + g1 * y
    h2 = rms_norm(x, norm2_g) * (1 + sc2) + sh2
    x = x + g2 * (jnp.square(jax.nn.relu(h2 @ w_ff1)) @ w_ff2)
    return x, k, v, ki, vn


def setup_inputs(seed: int = 0) -> dict:
    key = jax.random.key(seed)
    ks = jax.random.split(key, 24)
    n = jax.random.normal
    f = jnp.float32
    return {
        "x_prompt": n(ks[0], (BATCH, SEQ, D_MODEL), f),
        "x_sample": n(ks[1], (DEC_BATCH, DEC_SEQ, D_MODEL), f),
        "c_prompt": n(ks[2], (BATCH, D_MODEL), f),
        "c_sample": n(ks[3], (DEC_BATCH, D_MODEL), f),
        "cache_k": n(ks[4], (DEPTH, DEC_BATCH, PAST_LEN, N_HEADS_A, HEAD_DIM), f),
        "cache_v": n(ks[5], (DEPTH, DEC_BATCH, PAST_LEN, N_HEADS_A, HEAD_DIM), f),
        "cache_kidx": n(ks[6], (DEPTH, DEC_BATCH, PAST_LEN, IDX_DIM), f),
        "w_ada": n(ks[7], (DEPTH, D_MODEL, 6 * D_MODEL), f) * (0.5 * D_MODEL ** -0.5),
        "b_ada": n(ks[8], (DEPTH, 6 * D_MODEL), f) * 0.01,
        "norm1_g": 1.0 + 0.05 * n(ks[9], (DEPTH, D_MODEL), f),
        "norm2_g": 1.0 + 0.05 * n(ks[10], (DEPTH, D_MODEL), f),
        "w_in": n(ks[11], (DEPTH, D_MODEL, IN_WIDTH), f) * (D_MODEL ** -0.5),
        "q_norm_g": 1.0 + 0.05 * n(ks[12], (DEPTH, HEAD_DIM), f),
        "k_norm_g": 1.0 + 0.05 * n(ks[13], (DEPTH, HEAD_DIM), f),
        "gmlp_ln_g": 1.0 + 0.05 * n(ks[14], (DEPTH, WIDTH_B), f),
        "gmlp_ln_b": 0.01 * n(ks[15], (DEPTH, WIDTH_B), f),
        "gmlp_ws": n(ks[16], (DEPTH, N_GROUPS_B, GMLP_CHUNK, GMLP_CHUNK), f) * (GMLP_CHUNK ** -0.5),
        "gmlp_bs": 1.0 + 0.1 * n(ks[17], (DEPTH, N_GROUPS_B, GMLP_CHUNK), f),
        "w_out": n(ks[18], (DEPTH, MIX_WIDTH, D_MODEL), f) * (MIX_WIDTH ** -0.5),
        "w_ff1": n(ks[19], (DEPTH, D_MODEL, D_FF), f) * (D_MODEL ** -0.5),
        "w_ff2": n(ks[20], (DEPTH, D_FF, D_MODEL), f) * (D_FF ** -0.5),
    }


def reference(x_prompt, x_sample, c_prompt, c_sample, cache_k, cache_v, cache_kidx, w_ada, b_ada, norm1_g, norm2_g, w_in, q_norm_g, k_norm_g, gmlp_ln_g, gmlp_ln_b, gmlp_ws, gmlp_bs, w_out, w_ff1, w_ff2):
    yp, ys = x_prompt, x_sample
    kp_l, vp_l, kip_l, ks_l, vs_l, kis_l, gvs_l = [], [], [], [], [], [], []
    for l in range(DEPTH):
        params = (w_ada[l], b_ada[l], norm1_g[l], norm2_g[l], w_in[l], q_norm_g[l], k_norm_g[l], gmlp_ln_g[l], gmlp_ln_b[l], gmlp_ws[l], gmlp_bs[l], w_out[l], w_ff1[l], w_ff2[l])
        yp, kp, vp, kip, _ = hybrid_layer(yp, c_prompt, None, None, None, *params)
        ys, ksm, vsm, kism, gvs = hybrid_layer(ys, c_sample, cache_k[l], cache_v[l], cache_kidx[l], *params)
        kp_l.append(kp); vp_l.append(vp); kip_l.append(kip)
        ks_l.append(ksm); vs_l.append(vsm); kis_l.append(kism); gvs_l.append(gvs)
    return (yp, ys, jnp.stack(kp_l), jnp.stack(vp_l), jnp.stack(kip_l), jnp.stack(ks_l), jnp.stack(vs_l), jnp.stack(kis_l), jnp.stack(gvs_l))
```

```python
import functools

import jax
import jax.numpy as jnp
from jax import lax
from jax.experimental import pallas as pl
from jax.experimental.pallas import tpu as pltpu

N_HEADS_A = 8
HEAD_DIM = 64
WIDTH_A = N_HEADS_A * HEAD_DIM
IDX_HEADS = 4
IDX_DIM = 64
N_GROUPS_B = 4
GROUP_DIM_B = 128
WIDTH_B = N_GROUPS_B * GROUP_DIM_B
GMLP_CHUNK = 128
CHUNK = 64
TOPK_MAX = 256
ROPE_THETA = 10000.0
EPS = 1e-6

LANES = 128
KI_PAD = LANES
WI_PAD = LANES
IN_WIDTH_PADDED = 3 * WIDTH_A + IDX_HEADS * IDX_DIM + KI_PAD + WI_PAD + 2 * WIDTH_B

PROJ_ROWS = 512
DSA_TQ = 256
DSA_TK = PROJ_ROWS
SAMPLE_GROUP = 8
VMEM_LIMIT = 56 * 1024 * 1024

MASK_BIAS = -1e30
KEY_NEG_INF = -2139095041
KEY_POS_INF = 2139095040
F32_MIN_NORMAL = 1.1754944e-38
F32_MAX = 3.4028235e38

_F32 = jnp.float32
_BF16 = jnp.bfloat16


def _dot(a, b):
    return jnp.dot(a, b, preferred_element_type=_F32)


def _dot_nt(a, b):
    return lax.dot_general(a, b, (((1,), (1,)), ((), ())), preferred_element_type=_F32)


def _split_bf16(a):
    hi = a.astype(_BF16)
    lo = (a - hi.astype(_F32)).astype(_BF16)
    return hi, lo


def _ada_kernel(c_ref, w_ref, b_ref, o_ref):
    c = c_ref[...]
    s_hi, s_lo = _split_bf16(c * jax.nn.sigmoid(c))
    w_hi, w_lo = _split_bf16(w_ref[...])
    o_ref[...] = _dot(s_hi, w_hi) + _dot(s_lo, w_hi) + _dot(s_hi, w_lo) + b_ref[...]


def _ada_call(c, w_ada, b_ada):
    rows, d = c.shape
    n = w_ada.shape[1]
    tn = 1024
    return pl.pallas_call(
        _ada_kernel,
        grid=(n // tn,),
        in_specs=[
            pl.BlockSpec((rows, d), lambda j: (0, 0)),
            pl.BlockSpec((d, tn), lambda j: (0, j)),
            pl.BlockSpec((1, tn), lambda j: (0, j)),
        ],
        out_specs=pl.BlockSpec((rows, tn), lambda j: (0, j)),
        out_shape=jax.ShapeDtypeStruct((rows, n), _F32),
        compiler_params=pltpu.CompilerParams(dimension_semantics=("arbitrary",), vmem_limit_bytes=VMEM_LIMIT),
        name="ada",
    )(c, w_ada, b_ada)


def _proj_kernel(x_ref, sh_ref, sc_ref, ng_ref, w_ref, cos_ref, sin_ref, bd_ref, qg_ref, kg_ref,
                 lng_ref, lnb_ref, ws_ref, bs_ref, *out_refs, prompt, cb):
    nb, r, d = x_ref.shape
    m = nb * r

    x = x_ref[...]
    ms = jnp.mean(x * x, axis=-1, keepdims=True)
    h = (x * lax.rsqrt(ms + EPS) * ng_ref[...]) * (1.0 + sc_ref[...]) + sh_ref[...]
    h = h.reshape(m, d).astype(_BF16)

    def seg(a, b):
        return _dot(h, w_ref[:, a:b])

    o = 0
    q = seg(o, o + WIDTH_A); o += WIDTH_A
    k = seg(o, o + WIDTH_A); o += WIDTH_A
    v = seg(o, o + WIDTH_A); o += WIDTH_A
    qi = seg(o, o + IDX_HEADS * IDX_DIM); o += IDX_HEADS * IDX_DIM
    ki = seg(o, o + KI_PAD); o += KI_PAD
    wi = seg(o, o + WI_PAD); o += WI_PAD
    u = seg(o, o + WIDTH_B); o += WIDTH_B
    vg = seg(o, o + WIDTH_B)

    bd = bd_ref[...]

    def head_norm(t, g):
        hi, lo = _split_bf16(t * t)
        ss = _dot(hi, bd) + _dot(lo, bd)
        return t * lax.rsqrt(ss * (1.0 / HEAD_DIM) + EPS) * g

    cos1 = jnp.broadcast_to(cos_ref[...], (nb, r, LANES)).reshape(m, LANES)
    sin1 = jnp.broadcast_to(sin_ref[...], (nb, r, LANES)).reshape(m, LANES)

    def rope(t):
        w = t.shape[1]
        reps = w // LANES
        cosw = cos1 if reps == 1 else jnp.concatenate([cos1] * reps, axis=1)
        sinw = sin1 if reps == 1 else jnp.concatenate([sin1] * reps, axis=1)
        lane = lax.broadcasted_iota(jnp.int32, (1, w), 1)
        first_half = (lane % HEAD_DIM) < (HEAD_DIM // 2)
        rot = jnp.where(first_half, pltpu.roll(t, w - HEAD_DIM // 2, 1), pltpu.roll(t, HEAD_DIM // 2, 1))
        return t * cosw + rot * sinw

    qr = rope(head_norm(q, qg_ref[...]))
    kr = rope(head_norm(k, kg_ref[...]))
    qir = rope(qi)
    kir = rope(ki)
    wis = wi * ((IDX_DIM * IDX_HEADS) ** -0.5)

    ug = jax.nn.gelu(u)
    vgg = jax.nn.gelu(vg)
    mu = jnp.mean(vgg, axis=-1, keepdims=True)
    xc = vgg - mu
    var = jnp.mean(xc * xc, axis=-1, keepdims=True)
    vn = xc * lax.rsqrt(var + EPS) * lng_ref[...] + lnb_ref[...]
    vnb = vn.astype(_BF16)
    rowi = lax.broadcasted_iota(jnp.int32, (cb, cb), 0)
    coli = lax.broadcasted_iota(jnp.int32, (cb, cb), 1)
    ws_m = [jnp.where(rowi >= coli, ws_ref[g], 0.0).astype(_BF16) for g in range(N_GROUPS_B)]
    gm_rows = []
    for c in range(m // cb):
        pieces = []
        for g in range(N_GROUPS_B):
            lanes = slice(g * GROUP_DIM_B, (g + 1) * GROUP_DIM_B)
            mixed = _dot(ws_m[g], vnb[c * cb:(c + 1) * cb, lanes]) + bs_ref[:, lanes]
            pieces.append(ug[c * cb:(c + 1) * cb, lanes] * mixed)
        gm_rows.append(jnp.concatenate(pieces, axis=1))
    gm = gm_rows[0] if len(gm_rows) == 1 else jnp.concatenate(gm_rows, axis=0)

    def put(ref, val):
        ref[...] = val.astype(ref.dtype).reshape(ref.shape)

    if prompt:
        k_out, v_out, ki_out, q_bf, qi_bf, wi_out, gm_out, kt_bf, v_bf, kit_bf = out_refs
        put(kt_bf, kr.T)
        put(v_bf, v)
        put(kit_bf, kir.T[:IDX_DIM, :])
    else:
        k_out, v_out, ki_out, q_bf, qi_bf, wi_out, gm_out, vn_out = out_refs
        put(vn_out, vn)
    put(k_out, kr)
    put(v_out, v)
    put(ki_out, kir[:, :IDX_DIM])
    put(q_bf, qr * (HEAD_DIM ** -0.5))
    put(qi_bf, qir)
    put(wi_out, wis)
    put(gm_out, gm)


def _proj_call(x3, mod3, norm_g, w_in_p, cos_t, sin_t, bd, qg, kg, lng, lnb, ws, bs_b, *, nb, r, prompt):
    nseq, t, d = x3.shape
    m = nb * r
    cb = ws.shape[1]
    grid = (nseq // nb, t // r)
    const2 = lambda i, j: (0, 0)
    const3 = lambda i, j: (0, 0, 0)
    row_spec = lambda c: pl.BlockSpec((nb, r, c), lambda i, j: (i, j, 0))
    in_specs = [
        row_spec(d),
        pl.BlockSpec((nb, 1, d), lambda i, j: (i, 0, 0)),
        pl.BlockSpec((nb, 1, d), lambda i, j: (i, 0, 1)),
        pl.BlockSpec((1, d), const2),
        pl.BlockSpec(w_in_p.shape, const2, pipeline_mode=pl.Buffered(1)),
        pl.BlockSpec((1, r, LANES), lambda i, j: (0, j, 0)),
        pl.BlockSpec((1, r, LANES), lambda i, j: (0, j, 0)),
        pl.BlockSpec(bd.shape, const2),
        pl.BlockSpec((1, WIDTH_A), const2),
        pl.BlockSpec((1, WIDTH_A), const2),
        pl.BlockSpec((1, WIDTH_B), const2),
        pl.BlockSpec((1, WIDTH_B), const2),
        pl.BlockSpec(ws.shape, const3),
        pl.BlockSpec(bs_b.shape, const2),
    ]
    out_shape = [
        jax.ShapeDtypeStruct((nseq, t, WIDTH_A), _F32),
        jax.ShapeDtypeStruct((nseq, t, WIDTH_A), _F32),
        jax.ShapeDtypeStruct((nseq, t, IDX_DIM), _F32),
        jax.ShapeDtypeStruct((nseq, t, WIDTH_A), _BF16),
        jax.ShapeDtypeStruct((nseq, t, IDX_HEADS * IDX_DIM), _BF16),
        jax.ShapeDtypeStruct((nseq, t, LANES), _F32),
        jax.ShapeDtypeStruct((nseq, t, WIDTH_B), _BF16),
    ]
    out_specs = [row_spec(WIDTH_A), row_spec(WIDTH_A), row_spec(IDX_DIM), row_spec(WIDTH_A),
                 row_spec(IDX_HEADS * IDX_DIM), row_spec(LANES), row_spec(WIDTH_B)]
    if prompt:
        assert nb == 1
        nt = t // r
        out_shape += [
            jax.ShapeDtypeStruct((nseq, nt, WIDTH_A, r), _BF16),
            jax.ShapeDtypeStruct((nseq, t, WIDTH_A), _BF16),
            jax.ShapeDtypeStruct((nseq, nt, IDX_DIM, r), _BF16),
        ]
        out_specs += [
            pl.BlockSpec((1, 1, WIDTH_A, r), lambda i, j: (i, j, 0, 0)),
            row_spec(WIDTH_A),
            pl.BlockSpec((1, 1, IDX_DIM, r), lambda i, j: (i, j, 0, 0)),
        ]
    else:
        out_shape += [jax.ShapeDtypeStruct((nseq, t, WIDTH_B), _F32)]
        out_specs += [row_spec(WIDTH_B)]
    return pl.pallas_call(
        functools.partial(_proj_kernel, prompt=prompt, cb=cb),
        grid=grid,
        in_specs=in_specs,
        out_specs=out_specs,
        out_shape=out_shape,
        compiler_params=pltpu.CompilerParams(
            dimension_semantics=("arbitrary", "arbitrary"), vmem_limit_bytes=VMEM_LIMIT),
        name="proj_prompt" if prompt else "proj_sample",
    )(x3, mod3, mod3, norm_g, w_in_p, cos_t, sin_t, bd, qg, kg, lng, lnb, ws, bs_b)


def _key_to_f32(key):
    bits = jnp.where(key >= 0, key, key ^ jnp.int32(0x7FFFFFFF))
    return lax.bitcast_convert_type(bits, _F32)


def _count_rows(sc_ref, nkb, pred):
    _, tq, tk = sc_ref.shape

    def body(j, acc):
        ones = jnp.where(pred(sc_ref[j], j), 1.0, 0.0)
        part = ones[:, :LANES]
        for c in range(1, tk // LANES):
            part = part + ones[:, c * LANES:(c + 1) * LANES]
        return acc + part

    acc = lax.fori_loop(0, nkb, body, jnp.zeros((tq, LANES), _F32))
    return jnp.sum(acc, axis=1, keepdims=True)


def _scores_to_bias(sc_ref, nkb, topk):
    nblk, tq, tk = sc_ref.shape
    kf = jnp.float32(topk)
    total = nkb * tk

    def mid_of(lo, hi):
        return (lo >> 1) + (hi >> 1) + (lo & hi & 1)

    def is_active(lo, hi, cnt_lo):
        return (cnt_lo != kf) & (mid_of(lo, hi) != lo)

    def cond(st):
        return (st[4] > 0) & (st[5] < 34)

    def body(st):
        lo, hi, cnt_lo, cnt_hi, _, it = st
        mid = mid_of(lo, hi)
        active = is_active(lo, hi, cnt_lo)
        thr = _key_to_f32(mid)
        cnt = _count_rows(sc_ref, nkb, lambda t, j: t >= thr)
        ge = cnt >= kf
        up = active & ge
        dn = active & jnp.logical_not(ge)
        lo = jnp.where(up, mid, lo)
        cnt_lo = jnp.where(up, cnt, cnt_lo)
        hi = jnp.where(dn, mid, hi)
        cnt_hi = jnp.where(dn, cnt, cnt_hi)
        n_active = jnp.max(is_active(lo, hi, cnt_lo).astype(jnp.int32))
        return lo, hi, cnt_lo, cnt_hi, n_active, it + 1

    init = (
        jnp.full((tq, 1), KEY_NEG_INF, jnp.int32),
        jnp.full((tq, 1), KEY_POS_INF, jnp.int32),
        jnp.full((tq, 1), total, jnp.int32).astype(_F32),
        jnp.zeros((tq, 1), _F32),
        jnp.int32(1),
        jnp.int32(0),
    )
    lo, hi, cnt_lo, cnt_hi, _, _ = lax.while_loop(cond, body, init)

    thr = _key_to_f32(lo)
    thr = jnp.where(jnp.abs(thr) < F32_MIN_NORMAL, 0.0, thr)
    thr = jnp.maximum(thr, -F32_MAX)
    tie = (cnt_lo > kf) & (lo > KEY_NEG_INF)
    any_tie = jnp.max(tie.astype(jnp.int32)) > 0
    col = lax.broadcasted_iota(jnp.int32, (1, tk), 1)

    @pl.when(jnp.logical_not(any_tie))
    def _():
        def fill(j, carry):
            sc_ref[j] = jnp.where(sc_ref[j] >= thr, 0.0, MASK_BIAS)
            return carry
        lax.fori_loop(0, nkb, fill, 0)

    @pl.when(any_tie)
    def _():
        need = kf - cnt_hi
        lo_m = jnp.zeros((tq, 1), jnp.int32)
        hi_m = jnp.full((tq, 1), total, jnp.int32)
        for _ in range((nblk * tk).bit_length()):
            mid = (lo_m + hi_m) >> 1
            cnt = _count_rows(sc_ref, nkb, lambda t, j: (t == thr) & ((j * tk + col) < mid))
            ge = cnt >= need
            hi_m = jnp.where(ge, mid, hi_m)
            lo_m = jnp.where(ge, lo_m, mid)
        m_sel = jnp.where(tie, hi_m, total)

        def fill(j, carry):
            t = sc_ref[j]
            sel = (t > thr) | ((t == thr) & ((j * tk + col) < m_sel))
            sc_ref[j] = jnp.where(sel, 0.0, MASK_BIAS)
            return carry
        lax.fori_loop(0, nkb, fill, 0)


def _dsa_prompt_kernel(q_ref, qi_ref, wi_ref, kt_ref, v_ref, kit_ref, o_ref, sc_ref, m_ref, l_ref, acc_ref, *, topk):
    i = pl.program_id(1)
    tq = q_ref.shape[1]
    tk = kit_ref.shape[3]
    nkb = ((i + 1) * tq + tk - 1) // tk

    qi = qi_ref[0]
    wi = wi_ref[0]
    row_pos = i * tq + lax.broadcasted_iota(jnp.int32, (tq, 1), 0)
    vis_lim = (row_pos // CHUNK + 1) * CHUNK
    col = lax.broadcasted_iota(jnp.int32, (1, tk), 1)

    def score_body(j, carry):
        kit = kit_ref[0, j]
        acc = jnp.zeros((tq, tk), _F32)
        for h in range(IDX_HEADS):
            s = _dot(qi[:, h * IDX_DIM:(h + 1) * IDX_DIM], kit)
            acc = acc + wi[:, h:h + 1] * jnp.maximum(s, 0.0)
        sc_ref[j] = jnp.where((j * tk + col) < vis_lim, acc, -jnp.inf)
        return carry

    lax.fori_loop(0, nkb, score_body, 0)
    _scores_to_bias(sc_ref, nkb, topk)

    for h in range(N_HEADS_A):
        hs = slice(h * HEAD_DIM, (h + 1) * HEAD_DIM)
        qh = q_ref[0, :, hs]
        m_ref[...] = jnp.full(m_ref.shape, -jnp.inf, _F32)
        l_ref[...] = jnp.zeros(l_ref.shape, _F32)
        acc_ref[...] = jnp.zeros(acc_ref.shape, _F32)

        def att_body(j, carry):
            s = _dot(qh, kt_ref[0, j, hs, :]) + sc_ref[j]
            m_old = m_ref[...]
            m_new = jnp.maximum(m_old, jnp.max(s, axis=1, keepdims=True))
            p = jnp.exp(s - m_new)
            alpha = jnp.exp(m_old - m_new)
            l_ref[...] = alpha * l_ref[...] + jnp.sum(p, axis=1, keepdims=True)
            acc_ref[...] = alpha * acc_ref[...] + _dot(p.astype(_BF16), v_ref[0, j, :, hs])
            m_ref[...] = m_new
            return carry

        lax.fori_loop(0, nkb, att_body, 0)
        o_ref[0, :, hs] = (acc_ref[...] / l_ref[...]).astype(o_ref.dtype)


def _dsa_prompt_call(q_bf, qi_bf, wi, kt_bf, v_bf, kit_bf, *, topk):
    b, t, _ = q_bf.shape
    nt, tk = kt_bf.shape[1], kt_bf.shape[3]
    tq = min(DSA_TQ, t)
    v4 = v_bf.reshape(b, nt, tk, WIDTH_A)
    resident = lambda shape: pl.BlockSpec((1,) + shape, lambda bi, i: (bi, 0, 0, 0), pipeline_mode=pl.Buffered(1))
    return pl.pallas_call(
        functools.partial(_dsa_prompt_kernel, topk=topk),
        grid=(b, t // tq),
        in_specs=[
            pl.BlockSpec((1, tq, WIDTH_A), lambda bi, i: (bi, i, 0)),
            pl.BlockSpec((1, tq, IDX_HEADS * IDX_DIM), lambda bi, i: (bi, i, 0)),
            pl.BlockSpec((1, tq, LANES), lambda bi, i: (bi, i, 0)),
            resident((nt, WIDTH_A, tk)),
            resident((nt, tk, WIDTH_A)),
            resident((nt, IDX_DIM, tk)),
        ],
        out_specs=pl.BlockSpec((1, tq, WIDTH_A), lambda bi, i: (bi, i, 0)),
        out_shape=jax.ShapeDtypeStruct((b, t, WIDTH_A), _BF16),
        scratch_shapes=[
            pltpu.VMEM((nt, tq, tk), _F32),
            pltpu.VMEM((tq, 1), _F32),
            pltpu.VMEM((tq, 1), _F32),
            pltpu.VMEM((tq, HEAD_DIM), _F32),
        ],
        compiler_params=pltpu.CompilerParams(
            dimension_semantics=("arbitrary", "arbitrary"), vmem_limit_bytes=VMEM_LIMIT),
        name="dsa_prompt",
    )(q_bf, qi_bf, wi, kt_bf, v4, kit_bf)


def _dsa_sample_kernel(q_ref, qi_ref, wi_ref, kn_ref, vn_ref, kin_ref, ck_ref, cv_ref, cki_ref, o_ref,
                       sc_ref, kall_ref, vall_ref, *, topk):
    t = q_ref.shape[1]
    past = ck_ref.shape[2]
    s_pad = sc_ref.shape[2]
    tail = s_pad - past

    kall_ref[:past, :] = ck_ref[0, 0].astype(_BF16)
    vall_ref[:past, :] = cv_ref[0, 0].astype(_BF16)
    zpad = jnp.zeros((tail - t, WIDTH_A), _BF16)
    kall_ref[past:, :] = jnp.concatenate([kn_ref[0].astype(_BF16), zpad], axis=0)
    vall_ref[past:, :] = jnp.concatenate([vn_ref[0].astype(_BF16), zpad], axis=0)

    qi = qi_ref[0]
    wi = wi_ref[0]
    qi_stack = jnp.concatenate([qi[:, h * IDX_DIM:(h + 1) * IDX_DIM] for h in range(IDX_HEADS)], axis=0)
    ki_tail = jnp.concatenate([kin_ref[0].astype(_BF16), jnp.zeros((tail - t, IDX_DIM), _BF16)], axis=0)
    s_cache = _dot_nt(qi_stack, cki_ref[0, 0].astype(_BF16))
    s_tail = _dot_nt(qi_stack, ki_tail)

    def combine(s):
        acc = jnp.zeros((t, s.shape[1]), _F32)
        for h in range(IDX_HEADS):
            acc = acc + wi[:, h:h + 1] * jnp.maximum(s[h * t:(h + 1) * t, :], 0.0)
        return acc

    tail_col = lax.broadcasted_iota(jnp.int32, (1, tail), 1)
    sc_ref[0, :, :past] = combine(s_cache)
    sc_ref[0, :, past:] = jnp.where(tail_col < t, combine(s_tail), -jnp.inf)
    _scores_to_bias(sc_ref, 1, topk)

    nrow = N_HEADS_A * t
    row_head = lax.broadcasted_iota(jnp.int32, (nrow, WIDTH_A), 0) // t
    col_head = lax.broadcasted_iota(jnp.int32, (nrow, WIDTH_A), 1) // HEAD_DIM
    own = row_head == col_head
    q_rep = jnp.concatenate([q_ref[0]] * N_HEADS_A, axis=0)
    q_bd = jnp.where(own, q_rep, jnp.zeros_like(q_rep))
    bias = sc_ref[0]
    s = _dot_nt(q_bd, kall_ref[...]) + jnp.concatenate([bias] * N_HEADS_A, axis=0)
    m = jnp.max(s, axis=1, keepdims=True)
    p = jnp.exp(s - m)
    l = jnp.sum(p, axis=1, keepdims=True)
    pv = _dot(p.astype(_BF16), vall_ref[...]) / l
    pv = jnp.where(own, pv, 0.0)
    out = pv[:t]
    for h in range(1, N_HEADS_A):
        out = out + pv[h * t:(h + 1) * t]
    o_ref[0] = out.astype(o_ref.dtype)


def _dsa_sample_call(q_bf, qi_bf, wi, k_new, v_new, ki_new, cache_k, cache_v, cache_kidx, *, topk):
    b, t, _ = q_bf.shape
    past = cache_k.shape[2]
    s_pad = past + LANES
    assert t <= LANES
    row = lambda c: pl.BlockSpec((1, t, c), lambda bi: (bi, 0, 0))
    cache = lambda c: pl.BlockSpec((1, 1, past, c), lambda bi: (0, bi, 0, 0))
    return pl.pallas_call(
        functools.partial(_dsa_sample_kernel, topk=topk),
        grid=(b,),
        in_specs=[row(WIDTH_A), row(IDX_HEADS * IDX_DIM), row(LANES), row(WIDTH_A), row(WIDTH_A), row(IDX_DIM),
                  cache(WIDTH_A), cache(WIDTH_A), cache(IDX_DIM)],
        out_specs=row(WIDTH_A),
        out_shape=jax.ShapeDtypeStruct((b, t, WIDTH_A), _BF16),
        scratch_shapes=[
            pltpu.VMEM((1, t, s_pad), _F32),
            pltpu.VMEM((s_pad, WIDTH_A), _BF16),
            pltpu.VMEM((s_pad, WIDTH_A), _BF16),
        ],
        compiler_params=pltpu.CompilerParams(dimension_semantics=("arbitrary",), vmem_limit_bytes=VMEM_LIMIT),
        name="dsa_sample",
    )(q_bf, qi_bf, wi, k_new, v_new, ki_new, cache_k, cache_v, cache_kidx)


def _out_kernel(x_ref, at_ref, gm_ref, g1_ref, sh2_ref, sc2_ref, g2_ref, ng_ref, wo_ref, w1_ref, w2_ref, y_ref):
    nb, r, d = x_ref.shape
    m = nb * r
    at = at_ref[...].reshape(m, WIDTH_A)
    gm = gm_ref[...].reshape(m, WIDTH_B)
    y = _dot(at, wo_ref[:WIDTH_A, :]) + _dot(gm, wo_ref[WIDTH_A:, :])
    x1 = x_ref[...] + g1_ref[...] * y.reshape(nb, r, d)
    ms = jnp.mean(x1 * x1, axis=-1, keepdims=True)
    h2 = (x1 * lax.rsqrt(ms + EPS) * ng_ref[...]) * (1.0 + sc2_ref[...]) + sh2_ref[...]
    h2 = h2.reshape(m, d).astype(_BF16)
    dff = w1_ref.shape[1]
    fc = 1024
    ff = jnp.zeros((m, d), _F32)
    for c in range(dff // fc):
        a = jnp.maximum(_dot(h2, w1_ref[:, c * fc:(c + 1) * fc]), 0.0)
        ff = ff + _dot((a * a).astype(_BF16), w2_ref[c * fc:(c + 1) * fc, :])
    y_ref[...] = x1 + g2_ref[...] * ff.reshape(nb, r, d)


def _out_call(x3, attn, gm, mod3, norm_g, w_out, w_ff1, w_ff2, *, nb, r, name):
    nseq, t, d = x3.shape
    const2 = lambda i, j: (0, 0)
    row_spec = lambda c: pl.BlockSpec((nb, r, c), lambda i, j: (i, j, 0))
    mod_spec = lambda col: pl.BlockSpec((nb, 1, d), lambda i, j: (i, 0, col))
    return pl.pallas_call(
        _out_kernel,
        grid=(nseq // nb, t // r),
        in_specs=[row_spec(d), row_spec(WIDTH_A), row_spec(WIDTH_B),
                  mod_spec(2), mod_spec(3), mod_spec(4), mod_spec(5),
                  pl.BlockSpec((1, d), const2),
                  pl.BlockSpec(w_out.shape, const2, pipeline_mode=pl.Buffered(1)),
                  pl.BlockSpec(w_ff1.shape, const2, pipeline_mode=pl.Buffered(1)),
                  pl.BlockSpec(w_ff2.shape, const2, pipeline_mode=pl.Buffered(1))],
        out_specs=row_spec(d),
        out_shape=jax.ShapeDtypeStruct((nseq, t, d), _F32),
        compiler_params=pltpu.CompilerParams(
            dimension_semantics=("arbitrary", "arbitrary"), vmem_limit_bytes=VMEM_LIMIT),
        name=name,
    )(x3, attn, gm, mod3, mod3, mod3, mod3, norm_g, w_out, w_ff1, w_ff2)


def _rope_tables(pos):
    half = HEAD_DIM // 2
    inv_freq = jnp.power(jnp.float32(ROPE_THETA), -jnp.arange(half, dtype=_F32) / half)
    ang = pos.astype(_F32)[:, None] * inv_freq[None, :]
    cos, sin = jnp.cos(ang), jnp.sin(ang)
    reps = LANES // HEAD_DIM
    cos_t = jnp.tile(jnp.concatenate([cos, cos], axis=1), (1, reps))
    sin_t = jnp.tile(jnp.concatenate([-sin, sin], axis=1), (1, reps))
    return cos_t[None], sin_t[None]


def _pad_w_in(w_in):
    d = w_in.shape[0]
    a, i = WIDTH_A, IDX_HEADS * IDX_DIM
    o_ki = 3 * a + i
    o_wi = o_ki + IDX_DIM
    o_u = o_wi + IDX_HEADS
    z = lambda n: jnp.zeros((d, n), w_in.dtype)
    cols = [w_in[:, :o_ki], w_in[:, o_ki:o_wi], z(KI_PAD - IDX_DIM), w_in[:, o_wi:o_u], z(WI_PAD - IDX_HEADS),
            w_in[:, o_u:]]
    return jnp.concatenate(cols, axis=1).astype(_BF16)


def _layer(x_prompt, x_sample, c_prompt, c_sample, cache_k, cache_v, cache_kidx, w_ada, b_ada, norm1_g, norm2_g,
           w_in, q_norm_g, k_norm_g, gmlp_ln_g, gmlp_ln_b, gmlp_ws, gmlp_bs, w_out, w_ff1, w_ff2):
    bp, tp, d = x_prompt.shape
    bs_, ts, _ = x_sample.shape
    past = cache_k.shape[1]

    c_all = jnp.concatenate([c_prompt, c_sample], axis=0)
    rows = -(-c_all.shape[0] // 16) * 16
    c_all = jnp.pad(c_all, ((0, rows - c_all.shape[0]), (0, 0)))
    mod = _ada_call(c_all, w_ada, b_ada[None, :])
    mod_p = mod[:bp, None, :]
    mod_s = mod[bp:bp + bs_, None, :]

    w_in_p = _pad_w_in(w_in)
    w_out_b, w_ff1_b, w_ff2_b = w_out.astype(_BF16), w_ff1.astype(_BF16), w_ff2.astype(_BF16)
    head_of = jnp.arange(WIDTH_A) // HEAD_DIM
    bd = (head_of[:, None] == head_of[None, :]).astype(_BF16)
    qg = jnp.tile(q_norm_g, N_HEADS_A)[None, :]
    kg = jnp.tile(k_norm_g, N_HEADS_A)[None, :]
    lng, lnb = gmlp_ln_g[None, :], gmlp_ln_b[None, :]
    n1, n2 = norm1_g[None, :], norm2_g[None, :]

    rp = min(PROJ_ROWS, tp)
    cos_p, sin_p = _rope_tables(jnp.arange(tp))
    lp = min(tp, GMLP_CHUNK)
    bs_p = jnp.repeat(jnp.transpose(gmlp_bs[:, :lp]), GROUP_DIM_B, axis=1)
    (kp, vp, kip, q_bf, qi_bf, wi_p, gm_p, kt_bf, v_bf, kit_bf) = _proj_call(
        x_prompt, mod_p, n1, w_in_p, cos_p, sin_p, bd, qg, kg, lng, lnb, gmlp_ws[:, :lp, :lp], bs_p,
        nb=1, r=rp, prompt=True)
    attn_p = _dsa_prompt_call(q_bf, qi_bf, wi_p, kt_bf, v_bf, kit_bf, topk=min(TOPK_MAX, tp // 4))
    yp = _out_call(x_prompt, attn_p, gm_p, mod_p, n2, w_out_b, w_ff1_b, w_ff2_b, nb=1, r=rp, name="out_prompt")

    nb = min(SAMPLE_GROUP, bs_)
    cos_s, sin_s = _rope_tables(past + jnp.arange(ts))
    ls = min(ts, GMLP_CHUNK)
    assert ls == ts
    eye = jnp.eye(nb, dtype=gmlp_ws.dtype)
    ws_s = jax.vmap(lambda w: jnp.kron(eye, w))(gmlp_ws[:, :ls, :ls])
    bs_s = jnp.tile(jnp.repeat(jnp.transpose(gmlp_bs[:, :ls]), GROUP_DIM_B, axis=1), (nb, 1))
    (ks, vs, kis, qs_bf, qis_bf, wi_s, gm_s, gvs) = _proj_call(
        x_sample, mod_s, n1, w_in_p, cos_s, sin_s, bd, qg, kg, lng, lnb, ws_s, bs_s,
        nb=nb, r=ts, prompt=False)
    attn_s = _dsa_sample_call(
        qs_bf, qis_bf, wi_s, ks, vs, kis,
        cache_k.reshape(1, bs_, past, WIDTH_A), cache_v.reshape(1, bs_, past, WIDTH_A),
        cache_kidx.reshape(1, bs_, past, IDX_DIM), topk=min(TOPK_MAX, (past + ts) // 4))
    ys = _out_call(x_sample, attn_s, gm_s, mod_s, n2, w_out_b, w_ff1_b, w_ff2_b, nb=nb, r=ts, name="out_sample")

    heads = lambda a: a.reshape(a.shape[0], a.shape[1], N_HEADS_A, HEAD_DIM)
    return yp, ys, heads(kp), heads(vp), kip, heads(ks), heads(vs), kis, gvs


def kernel(x_prompt, x_sample, c_prompt, c_sample, cache_k, cache_v, cache_kidx, w_ada, b_ada, norm1_g, norm2_g,
           w_in, q_norm_g, k_norm_g, gmlp_ln_g, gmlp_ln_b, gmlp_ws, gmlp_bs, w_out, w_ff1, w_ff2):
    depth = w_ada.shape[0]
    yp, ys = x_prompt, x_sample
    outs = [[] for _ in range(7)]
    for l in range(depth):
        res = _layer(yp, ys, c_prompt, c_sample, cache_k[l], cache_v[l], cache_kidx[l], w_ada[l], b_ada[l],
                     norm1_g[l], norm2_g[l], w_in[l], q_norm_g[l], k_norm_g[l], gmlp_ln_g[l], gmlp_ln_b[l],
                     gmlp_ws[l], gmlp_bs[l], w_out[l], w_ff1[l], w_ff2[l])
        yp, ys = res[0], res[1]
        for acc, leaf in zip(outs, res[2:]):
            acc.append(leaf)
    return (yp, ys) + tuple(jnp.stack(o) for o in outs)
```

```python
import functools

import jax
import jax.numpy as jnp
from jax import lax
from jax.experimental import pallas as pl
from jax.experimental.pallas import tpu as pltpu

N_HEADS_A = 8
HEAD_DIM = 64
WIDTH_A = N_HEADS_A * HEAD_DIM
IDX_HEADS = 4
IDX_DIM = 64
N_GROUPS_B = 4
GROUP_DIM_B = 128
WIDTH_B = N_GROUPS_B * GROUP_DIM_B
GMLP_CHUNK = 128
CHUNK = 64
TOPK_MAX = 256
ROPE_THETA = 10000.0
EPS = 1e-6

LANES = 128
KI_PAD = LANES
WI_PAD = LANES
IN_WIDTH_PADDED = 3 * WIDTH_A + IDX_HEADS * IDX_DIM + KI_PAD + WI_PAD + 2 * WIDTH_B

PROJ_ROWS = 512
DSA_TQ = 256
DSA_TK = PROJ_ROWS
SAMPLE_GROUP = 8
COUNT_ROW_STRIP = 128
VMEM_LIMIT = 56 * 1024 * 1024

MASK_BIAS = -1e30
Q_SCALE = HEAD_DIM ** -0.5 * 1.4426950408889634
KEY_NEG_INF = -2139095041
KEY_POS_INF = 2139095040
F32_MIN_NORMAL = 1.1754944e-38
F32_MAX = 3.4028235e38

_F32 = jnp.float32
_BF16 = jnp.bfloat16


def _dot(a, b):
    return jnp.dot(a, b, preferred_element_type=_F32)


def _dot_nt(a, b):
    return lax.dot_general(a, b, (((1,), (1,)), ((), ())), preferred_element_type=_F32)


def _split_bf16(a):
    hi = a.astype(_BF16)
    lo = (a - hi.astype(_F32)).astype(_BF16)
    return hi, lo


def _ada_kernel(c_ref, w_ref, b_ref, o_ref):
    c = c_ref[...]
    s_hi, s_lo = _split_bf16(c * jax.nn.sigmoid(c))
    w_hi, w_lo = _split_bf16(w_ref[...])
    o_ref[...] = _dot(s_hi, w_hi) + _dot(s_lo, w_hi) + _dot(s_hi, w_lo) + b_ref[...]


def _ada_call(c, w_ada, b_ada):
    rows, d = c.shape
    n = w_ada.shape[1]
    tn = 1024
    return pl.pallas_call(
        _ada_kernel,
        grid=(n // tn,),
        in_specs=[
            pl.BlockSpec((rows, d), lambda j: (0, 0)),
            pl.BlockSpec((d, tn), lambda j: (0, j)),
            pl.BlockSpec((1, tn), lambda j: (0, j)),
        ],
        out_specs=pl.BlockSpec((rows, tn), lambda j: (0, j)),
        out_shape=jax.ShapeDtypeStruct((rows, n), _F32),
        compiler_params=pltpu.CompilerParams(dimension_semantics=("arbitrary",), vmem_limit_bytes=VMEM_LIMIT),
        name="ada",
    )(c, w_ada, b_ada)


def _proj_kernel(x_ref, sh_ref, sc_ref, ng_ref, w_ref, cos_ref, sin_ref, bd_ref, qg_ref, kg_ref,
                 lng_ref, lnb_ref, ws_ref, bs_ref, *out_refs, prompt, cb):
    nb, r, d = x_ref.shape
    m = nb * r

    x = x_ref[...]
    ms = jnp.mean(x * x, axis=-1, keepdims=True)
    h = (x * lax.rsqrt(ms + EPS) * ng_ref[...]) * (1.0 + sc_ref[...]) + sh_ref[...]
    h = h.reshape(m, d).astype(_BF16)

    def seg(a, b):
        return _dot(h, w_ref[:, a:b])

    o = 0
    q = seg(o, o + WIDTH_A); o += WIDTH_A
    k = seg(o, o + WIDTH_A); o += WIDTH_A
    v = seg(o, o + WIDTH_A); o += WIDTH_A
    qi = seg(o, o + IDX_HEADS * IDX_DIM); o += IDX_HEADS * IDX_DIM
    ki = seg(o, o + KI_PAD); o += KI_PAD
    wi = seg(o, o + WI_PAD); o += WI_PAD
    u = seg(o, o + WIDTH_B); o += WIDTH_B
    vg = seg(o, o + WIDTH_B)

    bd = bd_ref[...]

    def head_norm(t, g):
        hi, lo = _split_bf16(t * t)
        ss = _dot(hi, bd) + _dot(lo, bd)
        return t * lax.rsqrt(ss * (1.0 / HEAD_DIM) + EPS) * g

    cos1 = jnp.broadcast_to(cos_ref[...], (nb, r, LANES)).reshape(m, LANES)
    sin1 = jnp.broadcast_to(sin_ref[...], (nb, r, LANES)).reshape(m, LANES)

    def rope(t):
        w = t.shape[1]
        reps = w // LANES
        cosw = cos1 if reps == 1 else jnp.concatenate([cos1] * reps, axis=1)
        sinw = sin1 if reps == 1 else jnp.concatenate([sin1] * reps, axis=1)
        lane = lax.broadcasted_iota(jnp.int32, (1, w), 1)
        first_half = (lane % HEAD_DIM) < (HEAD_DIM // 2)
        rot = jnp.where(first_half, pltpu.roll(t, w - HEAD_DIM // 2, 1), pltpu.roll(t, HEAD_DIM // 2, 1))
        return t * cosw + rot * sinw

    qr = rope(head_norm(q, qg_ref[...]))
    kr = rope(head_norm(k, kg_ref[...]))
    qir = rope(qi)
    kir = rope(ki)
    wis = wi * ((IDX_DIM * IDX_HEADS) ** -0.5)

    ug = jax.nn.gelu(u)
    vgg = jax.nn.gelu(vg)
    mu = jnp.mean(vgg, axis=-1, keepdims=True)
    xc = vgg - mu
    var = jnp.mean(xc * xc, axis=-1, keepdims=True)
    vn = xc * lax.rsqrt(var + EPS) * lng_ref[...] + lnb_ref[...]
    vnb = vn.astype(_BF16)
    rowi = lax.broadcasted_iota(jnp.int32, (cb, cb), 0)
    coli = lax.broadcasted_iota(jnp.int32, (cb, cb), 1)
    ws_m = [jnp.where(rowi >= coli, ws_ref[g], 0.0).astype(_BF16) for g in range(N_GROUPS_B)]
    gm_rows = []
    for c in range(m // cb):
        pieces = []
        for g in range(N_GROUPS_B):
            lanes = slice(g * GROUP_DIM_B, (g + 1) * GROUP_DIM_B)
            mixed = _dot(ws_m[g], vnb[c * cb:(c + 1) * cb, lanes]) + bs_ref[:, lanes]
            pieces.append(ug[c * cb:(c + 1) * cb, lanes] * mixed)
        gm_rows.append(jnp.concatenate(pieces, axis=1))
    gm = gm_rows[0] if len(gm_rows) == 1 else jnp.concatenate(gm_rows, axis=0)

    def put(ref, val):
        ref[...] = val.astype(ref.dtype).reshape(ref.shape)

    if prompt:
        k_out, v_out, ki_out, q_bf, qi_bf, wi_out, gm_out, kt_bf, v_bf, kit_bf = out_refs
        put(kt_bf, kr.T)
        lane = lax.broadcasted_iota(jnp.int32, (1, LANES), 1)
        pieces = []
        for hd in range(N_HEADS_A):
            src = v[:, (hd // 2) * LANES:(hd // 2 + 1) * LANES]
            if hd % 2:
                src = pltpu.roll(src, HEAD_DIM, 1)
            pieces.append(jnp.where(lane < HEAD_DIM, src, 1.0))
        put(v_bf, jnp.concatenate(pieces, axis=1))
        put(kit_bf, kir.T[:IDX_DIM, :])
    else:
        k_out, v_out, ki_out, q_bf, qi_bf, wi_out, gm_out, vn_out = out_refs
        put(vn_out, vn)
    put(k_out, kr)
    put(v_out, v)
    put(ki_out, kir[:, :IDX_DIM])
    put(q_bf, qr * Q_SCALE)
    put(qi_bf, qir)
    put(wi_out, wis)
    put(gm_out, gm)


def _proj_call(x3, mod3, norm_g, w_in_p, cos_t, sin_t, bd, qg, kg, lng, lnb, ws, bs_b, *, nb, r, prompt):
    nseq, t, d = x3.shape
    m = nb * r
    cb = ws.shape[1]
    grid = (nseq // nb, t // r)
    const2 = lambda i, j: (0, 0)
    const3 = lambda i, j: (0, 0, 0)
    row_spec = lambda c: pl.BlockSpec((nb, r, c), lambda i, j: (i, j, 0))
    in_specs = [
        row_spec(d),
        pl.BlockSpec((nb, 1, d), lambda i, j: (i, 0, 0)),
        pl.BlockSpec((nb, 1, d), lambda i, j: (i, 0, 1)),
        pl.BlockSpec((1, d), const2),
        pl.BlockSpec(w_in_p.shape, const2, pipeline_mode=pl.Buffered(1)),
        pl.BlockSpec((1, r, LANES), lambda i, j: (0, j, 0)),
        pl.BlockSpec((1, r, LANES), lambda i, j: (0, j, 0)),
        pl.BlockSpec(bd.shape, const2),
        pl.BlockSpec((1, WIDTH_A), const2),
        pl.BlockSpec((1, WIDTH_A), const2),
        pl.BlockSpec((1, WIDTH_B), const2),
        pl.BlockSpec((1, WIDTH_B), const2),
        pl.BlockSpec(ws.shape, const3),
        pl.BlockSpec(bs_b.shape, const2),
    ]
    out_shape = [
        jax.ShapeDtypeStruct((nseq, t, WIDTH_A), _F32),
        jax.ShapeDtypeStruct((nseq, t, WIDTH_A), _F32),
        jax.ShapeDtypeStruct((nseq, t, IDX_DIM), _F32),
        jax.ShapeDtypeStruct((nseq, t, WIDTH_A), _BF16),
        jax.ShapeDtypeStruct((nseq, t, IDX_HEADS * IDX_DIM), _BF16),
        jax.ShapeDtypeStruct((nseq, t, LANES), _F32),
        jax.ShapeDtypeStruct((nseq, t, WIDTH_B), _BF16),
    ]
    out_specs = [row_spec(WIDTH_A), row_spec(WIDTH_A), row_spec(IDX_DIM), row_spec(WIDTH_A),
                 row_spec(IDX_HEADS * IDX_DIM), row_spec(LANES), row_spec(WIDTH_B)]
    if prompt:
        assert nb == 1
        nt = t // r
        out_shape += [
            jax.ShapeDtypeStruct((nseq, nt, WIDTH_A, r), _BF16),
            jax.ShapeDtypeStruct((nseq, t, N_HEADS_A * LANES), _BF16),
            jax.ShapeDtypeStruct((nseq, nt, IDX_DIM, r), _BF16),
        ]
        out_specs += [
            pl.BlockSpec((1, 1, WIDTH_A, r), lambda i, j: (i, j, 0, 0)),
            row_spec(N_HEADS_A * LANES),
            pl.BlockSpec((1, 1, IDX_DIM, r), lambda i, j: (i, j, 0, 0)),
        ]
    else:
        out_shape += [jax.ShapeDtypeStruct((nseq, t, WIDTH_B), _F32)]
        out_specs += [row_spec(WIDTH_B)]
    return pl.pallas_call(
        functools.partial(_proj_kernel, prompt=prompt, cb=cb),
        grid=grid,
        in_specs=in_specs,
        out_specs=out_specs,
        out_shape=out_shape,
        compiler_params=pltpu.CompilerParams(
            dimension_semantics=("arbitrary", "arbitrary"), vmem_limit_bytes=VMEM_LIMIT),
        name="proj_prompt" if prompt else "proj_sample",
    )(x3, mod3, mod3, norm_g, w_in_p, cos_t, sin_t, bd, qg, kg, lng, lnb, ws, bs_b)


def _key_to_f32(key):
    bits = jnp.where(key >= 0, key, key ^ jnp.int32(0x7FFFFFFF))
    return lax.bitcast_convert_type(bits, _F32)


def _count_rows(sc_ref, nkb, prep, pred):
    nblk, tq, tk = sc_ref.shape
    assert nblk * (tk // LANES) <= 256
    rs = min(tq, COUNT_ROW_STRIP)
    ones_mat = jnp.ones((LANES, LANES), _BF16)
    counts = []
    for r0 in range(0, tq, rs):
        rows = slice(r0, r0 + rs)
        operands = prep(rows)

        def body(j, acc, rows=rows, operands=operands):
            for c in range(tk // LANES):
                x = sc_ref[j, rows, c * LANES:(c + 1) * LANES]
                acc = acc + jnp.where(pred(x, j * tk + c * LANES, operands), 1.0, 0.0)
            return acc

        counts.append(lax.fori_loop(0, nkb, body, jnp.zeros((rs, LANES), _F32)))
    per_lane = counts[0] if len(counts) == 1 else jnp.concatenate(counts, axis=0)
    return _dot(per_lane.astype(_BF16), ones_mat)


def _scores_to_bias(sc_ref, rowf_ref, rowi_ref, nkb, topk):
    nblk, tq, tk = sc_ref.shape
    reps = tk // LANES
    kf = jnp.float32(topk)
    total = nkb * tk
    lane = lax.broadcasted_iota(jnp.int32, (1, LANES), 1)

    def mid_of(lo, hi):
        return lo + lax.shift_right_logical(hi - lo, 1)

    def cond(st):
        return (st[5] > 0) & (st[6] < 34)

    def body(st):
        return step(step(st))

    def step(st):
        lo, hi, cnt_lo, cnt_hi, mid, _, it = st
        rowf_ref[...] = _key_to_f32(mid)
        cnt = _count_rows(sc_ref, nkb, lambda rows: rowf_ref[rows, :], lambda x, col0, thr: x >= thr)
        ge = cnt >= kf
        lo = jnp.where(ge, mid, lo)
        cnt_lo = jnp.where(ge, cnt, cnt_lo)
        hi = jnp.where(ge, hi, mid)
        cnt_hi = jnp.where(ge, cnt_hi, cnt)
        mid = mid_of(lo, hi)
        unsettled = jnp.where((cnt_lo != kf) & (mid != lo), 1, 0)
        return lo, hi, cnt_lo, cnt_hi, mid, jnp.max(unsettled), it + 1

    lo0 = jnp.full((tq, LANES), KEY_NEG_INF, jnp.int32)
    hi0 = jnp.full((tq, LANES), KEY_POS_INF, jnp.int32)
    init = (lo0, hi0, jnp.full((tq, LANES), total, jnp.int32).astype(_F32), jnp.zeros((tq, LANES), _F32),
            mid_of(lo0, hi0), jnp.int32(1), jnp.int32(0))
    lo, hi, cnt_lo, cnt_hi = lax.while_loop(cond, body, init)[:4]

    thr = _key_to_f32(lo)
    thr = jnp.where(jnp.abs(thr) < F32_MIN_NORMAL, 0.0, thr)
    rowf_ref[...] = jnp.maximum(thr, -F32_MAX)
    tie = (cnt_lo > kf) & (lo > KEY_NEG_INF)
    any_tie = jnp.max(jnp.where(tie, 1, 0)) > 0

    @pl.when(jnp.logical_not(any_tie))
    def _():
        def fill(j, carry):
            thr_w = jnp.concatenate([rowf_ref[...]] * reps, axis=1)
            sc_ref[j] = jnp.where(sc_ref[j] >= thr_w, 0.0, MASK_BIAS)
            return carry
        lax.fori_loop(0, nkb, fill, 0)

    @pl.when(any_tie)
    def _():
        need = kf - cnt_hi
        lo_m = jnp.zeros((tq, LANES), jnp.int32)
        hi_m = jnp.full((tq, LANES), total, jnp.int32)
        for _ in range((nblk * tk).bit_length()):
            mid = (lo_m + hi_m) >> 1
            rowi_ref[...] = mid
            cnt = _count_rows(
                sc_ref, nkb, lambda rows: (rowf_ref[rows, :], rowi_ref[rows, :]),
                lambda x, col0, ops: (x == ops[0]) & ((col0 + lane) < ops[1]))
            ge = cnt >= need
            hi_m = jnp.where(ge, mid, hi_m)
            lo_m = jnp.where(ge, lo_m, mid)
        rowi_ref[...] = jnp.where(tie, hi_m, total)

        def fill(j, carry):
            for c in range(reps):
                cols = slice(c * LANES, (c + 1) * LANES)
                x = sc_ref[j, :, cols]
                thr_b = rowf_ref[...]
                sel = (x > thr_b) | ((x == thr_b) & ((j * tk + c * LANES + lane) < rowi_ref[...]))
                sc_ref[j, :, cols] = jnp.where(sel, 0.0, MASK_BIAS)
            return carry
        lax.fori_loop(0, nkb, fill, 0)


def _dsa_prompt_kernel(q_ref, qi_ref, wi_ref, kt_ref, v_ref, kit_ref, o_ref, sc_ref, rowf_ref, rowi_ref, m_ref, acc_ref, *, topk):
    i = pl.program_id(1)
    tq = q_ref.shape[1]
    tk = kit_ref.shape[3]
    nkb = ((i + 1) * tq + tk - 1) // tk

    qi = qi_ref[0]
    wi = wi_ref[0]
    row_pos = i * tq + lax.broadcasted_iota(jnp.int32, (tq, 1), 0)
    vis_lim = (row_pos // CHUNK + 1) * CHUNK
    col = lax.broadcasted_iota(jnp.int32, (1, tk), 1)

    def score_body(j, carry):
        kit = kit_ref[0, j]
        acc = jnp.zeros((tq, tk), _F32)
        for h in range(IDX_HEADS):
            s = _dot(qi[:, h * IDX_DIM:(h + 1) * IDX_DIM], kit)
            acc = acc + wi[:, h:h + 1] * jnp.maximum(s, 0.0)
        sc_ref[j] = jnp.where((j * tk + col) < vis_lim, acc, -jnp.inf)
        return carry

    lax.fori_loop(0, nkb, score_body, 0)
    _scores_to_bias(sc_ref, rowf_ref, rowi_ref, nkb, topk)

    reps = tk // LANES
    q_heads = [q_ref[0, :, h * HEAD_DIM:(h + 1) * HEAD_DIM] for h in range(N_HEADS_A)]

    def logits(j, h):
        return _dot(q_heads[h], kt_ref[0, j, h * HEAD_DIM:(h + 1) * HEAD_DIM, :]) + sc_ref[j]

    m_ref[...] = jnp.full(m_ref.shape, -jnp.inf, _F32)

    def max_body(j, carry):
        for h in range(N_HEADS_A):
            s = logits(j, h)
            part = s[:, :LANES]
            for c in range(1, reps):
                part = jnp.maximum(part, s[:, c * LANES:(c + 1) * LANES])
            m_ref[h] = jnp.maximum(m_ref[h], part)
        return carry

    lax.fori_loop(0, nkb, max_body, 0)
    for h in range(N_HEADS_A):
        m_ref[h] = jnp.broadcast_to(jnp.max(m_ref[h], axis=1, keepdims=True), (tq, LANES))

    acc_ref[...] = jnp.zeros(acc_ref.shape, _F32)

    def att_body(j, carry):
        for h in range(N_HEADS_A):
            p = jnp.exp2(logits(j, h) - jnp.concatenate([m_ref[h]] * reps, axis=1))
            acc_ref[h] += _dot(p.astype(_BF16), v_ref[0, j, :, h * LANES:(h + 1) * LANES])
        return carry

    lax.fori_loop(0, nkb, att_body, 0)
    lane = lax.broadcasted_iota(jnp.int32, (1, LANES), 1)
    for hp in range(N_HEADS_A // 2):
        a0, a1 = acc_ref[2 * hp], acc_ref[2 * hp + 1]
        out0 = a0 / pltpu.roll(a0, HEAD_DIM, 1)
        out1 = pltpu.roll(a1, HEAD_DIM, 1) / a1
        o_ref[0, :, hp * LANES:(hp + 1) * LANES] = jnp.where(lane < HEAD_DIM, out0, out1).astype(o_ref.dtype)


def _dsa_prompt_call(q_bf, qi_bf, wi, kt_bf, v_bf, kit_bf, *, topk):
    b, t, _ = q_bf.shape
    nt, tk = kt_bf.shape[1], kt_bf.shape[3]
    tq = min(DSA_TQ, t)
    v4 = v_bf.reshape(b, nt, tk, N_HEADS_A * LANES)
    resident = lambda shape: pl.BlockSpec((1,) + shape, lambda bi, i: (bi, 0, 0, 0), pipeline_mode=pl.Buffered(1))
    return pl.pallas_call(
        functools.partial(_dsa_prompt_kernel, topk=topk),
        grid=(b, t // tq),
        in_specs=[
            pl.BlockSpec((1, tq, WIDTH_A), lambda bi, i: (bi, i, 0)),
            pl.BlockSpec((1, tq, IDX_HEADS * IDX_DIM), lambda bi, i: (bi, i, 0)),
            pl.BlockSpec((1, tq, LANES), lambda bi, i: (bi, i, 0)),
            resident((nt, WIDTH_A, tk)),
            resident((nt, tk, N_HEADS_A * LANES)),
            resident((nt, IDX_DIM, tk)),
        ],
        out_specs=pl.BlockSpec((1, tq, WIDTH_A), lambda bi, i: (bi, i, 0)),
        out_shape=jax.ShapeDtypeStruct((b, t, WIDTH_A), _BF16),
        scratch_shapes=[
            pltpu.VMEM((nt, tq, tk), _F32),
            pltpu.VMEM((tq, LANES), _F32),
            pltpu.VMEM((tq, LANES), jnp.int32),
            pltpu.VMEM((N_HEADS_A, tq, LANES), _F32),
            pltpu.VMEM((N_HEADS_A, tq, LANES), _F32),
        ],
        compiler_params=pltpu.CompilerParams(
            dimension_semantics=("arbitrary", "arbitrary"), vmem_limit_bytes=VMEM_LIMIT),
        name="dsa_prompt",
    )(q_bf, qi_bf, wi, kt_bf, v4, kit_bf)


def _dsa_sample_kernel(q_ref, qi_ref, wi_ref, kn_ref, vn_ref, kin_ref, ck_ref, cv_ref, cki_ref, o_ref,
                       sc_ref, rowf_ref, rowi_ref, kall_ref, vall_ref, *, topk):
    t = q_ref.shape[1]
    past = ck_ref.shape[2]
    s_pad = sc_ref.shape[2]
    tail = s_pad - past

    kall_ref[:past, :] = ck_ref[0, 0].astype(_BF16)
    vall_ref[:past, :] = cv_ref[0, 0].astype(_BF16)
    zpad = jnp.zeros((tail - t, WIDTH_A), _BF16)
    kall_ref[past:, :] = jnp.concatenate([kn_ref[0].astype(_BF16), zpad], axis=0)
    vall_ref[past:, :] = jnp.concatenate([vn_ref[0].astype(_BF16), zpad], axis=0)

    qi = qi_ref[0]
    wi = wi_ref[0]
    qi_stack = jnp.concatenate([qi[:, h * IDX_DIM:(h + 1) * IDX_DIM] for h in range(IDX_HEADS)], axis=0)
    ki_tail = jnp.concatenate([kin_ref[0].astype(_BF16), jnp.zeros((tail - t, IDX_DIM), _BF16)], axis=0)
    s_cache = _dot_nt(qi_stack, cki_ref[0, 0].astype(_BF16))
    s_tail = _dot_nt(qi_stack, ki_tail)

    def combine(s):
        acc = jnp.zeros((t, s.shape[1]), _F32)
        for h in range(IDX_HEADS):
            acc = acc + wi[:, h:h + 1] * jnp.maximum(s[h * t:(h + 1) * t, :], 0.0)
        return acc

    tail_col = lax.broadcasted_iota(jnp.int32, (1, tail), 1)
    sc_ref[0, :, :past] = combine(s_cache)
    sc_ref[0, :, past:] = jnp.where(tail_col < t, combine(s_tail), -jnp.inf)
    _scores_to_bias(sc_ref, rowf_ref, rowi_ref, 1, topk)

    nrow = N_HEADS_A * t
    row_head = lax.broadcasted_iota(jnp.int32, (nrow, WIDTH_A), 0) // t
    col_head = lax.broadcasted_iota(jnp.int32, (nrow, WIDTH_A), 1) // HEAD_DIM
    own = row_head == col_head
    q_rep = jnp.concatenate([q_ref[0]] * N_HEADS_A, axis=0)
    q_bd = jnp.where(own, q_rep, jnp.zeros_like(q_rep))
    bias = sc_ref[0]
    s = _dot_nt(q_bd, kall_ref[...]) + jnp.concatenate([bias] * N_HEADS_A, axis=0)
    m = jnp.max(s, axis=1, keepdims=True)
    p = jnp.exp2(s - m)
    l = jnp.sum(p, axis=1, keepdims=True)
    pv = _dot(p.astype(_BF16), vall_ref[...]) / l
    pv = jnp.where(own, pv, 0.0)
    out = pv[:t]
    for h in range(1, N_HEADS_A):
        out = out + pv[h * t:(h + 1) * t]
    o_ref[0] = out.astype(o_ref.dtype)


def _dsa_sample_call(q_bf, qi_bf, wi, k_new, v_new, ki_new, cache_k, cache_v, cache_kidx, *, topk):
    b, t, _ = q_bf.shape
    past = cache_k.shape[2]
    s_pad = past + LANES
    assert t <= LANES
    row = lambda c: pl.BlockSpec((1, t, c), lambda bi: (bi, 0, 0))
    cache = lambda c: pl.BlockSpec((1, 1, past, c), lambda bi: (0, bi, 0, 0))
    return pl.pallas_call(
        functools.partial(_dsa_sample_kernel, topk=topk),
        grid=(b,),
        in_specs=[row(WIDTH_A), row(IDX_HEADS * IDX_DIM), row(LANES), row(WIDTH_A), row(WIDTH_A), row(IDX_DIM),
                  cache(WIDTH_A), cache(WIDTH_A), cache(IDX_DIM)],
        out_specs=row(WIDTH_A),
        out_shape=jax.ShapeDtypeStruct((b, t, WIDTH_A), _BF16),
        scratch_shapes=[
            pltpu.VMEM((1, t, s_pad), _F32),
            pltpu.VMEM((t, LANES), _F32),
            pltpu.VMEM((t, LANES), jnp.int32),
            pltpu.VMEM((s_pad, WIDTH_A), _BF16),
            pltpu.VMEM((s_pad, WIDTH_A), _BF16),
        ],
        compiler_params=pltpu.CompilerParams(dimension_semantics=("arbitrary",), vmem_limit_bytes=VMEM_LIMIT),
        name="dsa_sample",
    )(q_bf, qi_bf, wi, k_new, v_new, ki_new, cache_k, cache_v, cache_kidx)


def _out_kernel(x_ref, at_ref, gm_ref, g1_ref, sh2_ref, sc2_ref, g2_ref, ng_ref, wo_ref, w1_ref, w2_ref, y_ref):
    nb, r, d = x_ref.shape
    m = nb * r
    at = at_ref[...].reshape(m, WIDTH_A)
    gm = gm_ref[...].reshape(m, WIDTH_B)
    y = _dot(at, wo_ref[:WIDTH_A, :]) + _dot(gm, wo_ref[WIDTH_A:, :])
    x1 = x_ref[...] + g1_ref[...] * y.reshape(nb, r, d)
    ms = jnp.mean(x1 * x1, axis=-1, keepdims=True)
    h2 = (x1 * lax.rsqrt(ms + EPS) * ng_ref[...]) * (1.0 + sc2_ref[...]) + sh2_ref[...]
    h2 = h2.reshape(m, d).astype(_BF16)
    dff = w1_ref.shape[1]
    fc = 1024
    ff = jnp.zeros((m, d), _F32)
    for c in range(dff // fc):
        a = jnp.maximum(_dot(h2, w1_ref[:, c * fc:(c + 1) * fc]), 0.0)
        ff = ff + _dot((a * a).astype(_BF16), w2_ref[c * fc:(c + 1) * fc, :])
    y_ref[...] = x1 + g2_ref[...] * ff.reshape(nb, r, d)


def _out_call(x3, attn, gm, mod3, norm_g, w_out, w_ff1, w_ff2, *, nb, r, name):
    nseq, t, d = x3.shape
    const2 = lambda i, j: (0, 0)
    row_spec = lambda c: pl.BlockSpec((nb, r, c), lambda i, j: (i, j, 0))
    mod_spec = lambda col: pl.BlockSpec((nb, 1, d), lambda i, j: (i, 0, col))
    return pl.pallas_call(
        _out_kernel,
        grid=(nseq // nb, t // r),
        in_specs=[row_spec(d), row_spec(WIDTH_A), row_spec(WIDTH_B),
                  mod_spec(2), mod_spec(3), mod_spec(4), mod_spec(5),
                  pl.BlockSpec((1, d), const2),
                  pl.BlockSpec(w_out.shape, const2, pipeline_mode=pl.Buffered(1)),
                  pl.BlockSpec(w_ff1.shape, const2, pipeline_mode=pl.Buffered(1)),
                  pl.BlockSpec(w_ff2.shape, const2, pipeline_mode=pl.Buffered(1))],
        out_specs=row_spec(d),
        out_shape=jax.ShapeDtypeStruct((nseq, t, d), _F32),
        compiler_params=pltpu.CompilerParams(
            dimension_semantics=("arbitrary", "arbitrary"), vmem_limit_bytes=VMEM_LIMIT),
        name=name,
    )(x3, attn, gm, mod3, mod3, mod3, mod3, norm_g, w_out, w_ff1, w_ff2)


def _rope_tables(pos):
    half = HEAD_DIM // 2
    inv_freq = jnp.power(jnp.float32(ROPE_THETA), -jnp.arange(half, dtype=_F32) / half)
    ang = pos.astype(_F32)[:, None] * inv_freq[None, :]
    cos, sin = jnp.cos(ang), jnp.sin(ang)
    reps = LANES // HEAD_DIM
    cos_t = jnp.tile(jnp.concatenate([cos, cos], axis=1), (1, reps))
    sin_t = jnp.tile(jnp.concatenate([-sin, sin], axis=1), (1, reps))
    return cos_t[None], sin_t[None]


def _pad_w_in(w_in):
    d = w_in.shape[0]
    a, i = WIDTH_A, IDX_HEADS * IDX_DIM
    o_ki = 3 * a + i
    o_wi = o_ki + IDX_DIM
    o_u = o_wi + IDX_HEADS
    z = lambda n: jnp.zeros((d, n), w_in.dtype)
    cols = [w_in[:, :o_ki], w_in[:, o_ki:o_wi], z(KI_PAD - IDX_DIM), w_in[:, o_wi:o_u], z(WI_PAD - IDX_HEADS),
            w_in[:, o_u:]]
    return jnp.concatenate(cols, axis=1).astype(_BF16)


def _layer(x_prompt, x_sample, c_prompt, c_sample, cache_k, cache_v, cache_kidx, w_ada, b_ada, norm1_g, norm2_g,
           w_in, q_norm_g, k_norm_g, gmlp_ln_g, gmlp_ln_b, gmlp_ws, gmlp_bs, w_out, w_ff1, w_ff2):
    bp, tp, d = x_prompt.shape
    bs_, ts, _ = x_sample.shape
    past = cache_k.shape[1]

    c_all = jnp.concatenate([c_prompt, c_sample], axis=0)
    rows = -(-c_all.shape[0] // 16) * 16
    c_all = jnp.pad(c_all, ((0, rows - c_all.shape[0]), (0, 0)))
    mod = _ada_call(c_all, w_ada, b_ada[None, :])
    mod_p = mod[:bp, None, :]
    mod_s = mod[bp:bp + bs_, None, :]

    w_in_p = _pad_w_in(w_in)
    w_out_b, w_ff1_b, w_ff2_b = w_out.astype(_BF16), w_ff1.astype(_BF16), w_ff2.astype(_BF16)
    head_of = jnp.arange(WIDTH_A) // HEAD_DIM
    bd = (head_of[:, None] == head_of[None, :]).astype(_BF16)
    qg = jnp.tile(q_norm_g, N_HEADS_A)[None, :]
    kg = jnp.tile(k_norm_g, N_HEADS_A)[None, :]
    lng, lnb = gmlp_ln_g[None, :], gmlp_ln_b[None, :]
    n1, n2 = norm1_g[None, :], norm2_g[None, :]

    rp = min(PROJ_ROWS, tp)
    cos_p, sin_p = _rope_tables(jnp.arange(tp))
    lp = min(tp, GMLP_CHUNK)
    bs_p = jnp.repeat(jnp.transpose(gmlp_bs[:, :lp]), GROUP_DIM_B, axis=1)
    (kp, vp, kip, q_bf, qi_bf, wi_p, gm_p, kt_bf, v_bf, kit_bf) = _proj_call(
        x_prompt, mod_p, n1, w_in_p, cos_p, sin_p, bd, qg, kg, lng, lnb, gmlp_ws[:, :lp, :lp], bs_p,
        nb=1, r=rp, prompt=True)
    attn_p = _dsa_prompt_call(q_bf, qi_bf, wi_p, kt_bf, v_bf, kit_bf, topk=min(TOPK_MAX, tp // 4))
    yp = _out_call(x_prompt, attn_p, gm_p, mod_p, n2, w_out_b, w_ff1_b, w_ff2_b, nb=1, r=rp, name="out_prompt")

    nb = min(SAMPLE_GROUP, bs_)
    cos_s, sin_s = _rope_tables(past + jnp.arange(ts))
    ls = min(ts, GMLP_CHUNK)
    assert ls == ts
    eye = jnp.eye(nb, dtype=gmlp_ws.dtype)
    ws_s = jax.vmap(lambda w: jnp.kron(eye, w))(gmlp_ws[:, :ls, :ls])
    bs_s = jnp.tile(jnp.repeat(jnp.transpose(gmlp_bs[:, :ls]), GROUP_DIM_B, axis=1), (nb, 1))
    (ks, vs, kis, qs_bf, qis_bf, wi_s, gm_s, gvs) = _proj_call(
        x_sample, mod_s, n1, w_in_p, cos_s, sin_s, bd, qg, kg, lng, lnb, ws_s, bs_s,
        nb=nb, r=ts, prompt=False)
    attn_s = _dsa_sample_call(
        qs_bf, qis_bf, wi_s, ks, vs, kis,
        cache_k.reshape(1, bs_, past, WIDTH_A), cache_v.reshape(1, bs_, past, WIDTH_A),
        cache_kidx.reshape(1, bs_, past, IDX_DIM), topk=min(TOPK_MAX, (past + ts) // 4))
    ys = _out_call(x_sample, attn_s, gm_s, mod_s, n2, w_out_b, w_ff1_b, w_ff2_b, nb=nb, r=ts, name="out_sample")

    heads = lambda a: a.reshape(a.shape[0], a.shape[1], N_HEADS_A, HEAD_DIM)
    return yp, ys, heads(kp), heads(vp), kip, heads(ks), heads(vs), kis, gvs


def kernel(x_prompt, x_sample, c_prompt, c_sample, cache_k, cache_v, cache_kidx, w_ada, b_ada, norm1_g, norm2_g,
           w_in, q_norm_g, k_norm_g, gmlp_ln_g, gmlp_ln_b, gmlp_ws, gmlp_bs, w_out, w_ff1, w_ff2):
    depth = w_ada.shape[0]
    yp, ys = x_prompt, x_sample
    outs = [[] for _ in range(7)]
    for l in range(depth):
        res = _layer(yp, ys, c_prompt, c_sample, cache_k[l], cache_v[l], cache_kidx[l], w_ada[l], b_ada[l],
                     norm1_g[l], norm2_g[l], w_in[l], q_norm_g[l], k_norm_g[l], gmlp_ln_g[l], gmlp_ln_b[l],
                     gmlp_ws[l], gmlp_bs[l], w_out[l], w_ff1[l], w_ff2[l])
        yp, ys = res[0], res[1]
        for acc, leaf in zip(outs, res[2:]):
            acc.append(leaf)
    return (yp, ys) + tuple(jnp.stack(o) for o in outs)
```

```python
import functools

import jax
import jax.numpy as jnp
from jax import lax
from jax.experimental import pallas as pl
from jax.experimental.pallas import tpu as pltpu

N_HEADS_A = 8
HEAD_DIM = 64
WIDTH_A = N_HEADS_A * HEAD_DIM
IDX_HEADS = 4
IDX_DIM = 64
N_GROUPS_B = 4
GROUP_DIM_B = 128
WIDTH_B = N_GROUPS_B * GROUP_DIM_B
GMLP_CHUNK = 128
CHUNK = 64
TOPK_MAX = 256
ROPE_THETA = 10000.0
EPS = 1e-6

LANES = 128
KI_PAD = LANES
WI_PAD = LANES
IN_WIDTH_PADDED = 3 * WIDTH_A + IDX_HEADS * IDX_DIM + KI_PAD + WI_PAD + 2 * WIDTH_B

PROJ_ROWS = 512
DSA_TQ = 256
DSA_TK = PROJ_ROWS
SAMPLE_GROUP = 8
COUNT_ACC_ROWS = 32
VMEM_LIMIT = 56 * 1024 * 1024

MASK_BIAS = -1e30
Q_SCALE = HEAD_DIM ** -0.5 * 1.4426950408889634
KEY_NEG_INF = -2139095041
KEY_POS_INF = 2139095040
F32_MIN_NORMAL = 1.1754944e-38
F32_MAX = 3.4028235e38

_F32 = jnp.float32
_BF16 = jnp.bfloat16


def _dot(a, b):
    return jnp.dot(a, b, preferred_element_type=_F32)


def _dot_nt(a, b):
    return lax.dot_general(a, b, (((1,), (1,)), ((), ())), preferred_element_type=_F32)


def _split_bf16(a):
    hi = a.astype(_BF16)
    lo = (a - hi.astype(_F32)).astype(_BF16)
    return hi, lo


def _ada_kernel(c_ref, w_ref, b_ref, o_ref):
    c = c_ref[...]
    s_hi, s_lo = _split_bf16(c * jax.nn.sigmoid(c))
    w_hi, w_lo = _split_bf16(w_ref[...])
    o_ref[...] = _dot(s_hi, w_hi) + _dot(s_lo, w_hi) + _dot(s_hi, w_lo) + b_ref[...]


def _ada_call(c, w_ada, b_ada):
    rows, d = c.shape
    n = w_ada.shape[1]
    tn = 1024
    return pl.pallas_call(
        _ada_kernel,
        grid=(n // tn,),
        in_specs=[
            pl.BlockSpec((rows, d), lambda j: (0, 0)),
            pl.BlockSpec((d, tn), lambda j: (0, j)),
            pl.BlockSpec((1, tn), lambda j: (0, j)),
        ],
        out_specs=pl.BlockSpec((rows, tn), lambda j: (0, j)),
        out_shape=jax.ShapeDtypeStruct((rows, n), _F32),
        compiler_params=pltpu.CompilerParams(dimension_semantics=("arbitrary",), vmem_limit_bytes=VMEM_LIMIT),
        name="ada",
    )(c, w_ada, b_ada)


def _proj_kernel(x_ref, sh_ref, sc_ref, ng_ref, w_ref, cos_ref, sin_ref, bd_ref, qg_ref, kg_ref,
                 lng_ref, lnb_ref, ws_ref, bs_ref, *out_refs, prompt, cb):
    nb, r, d = x_ref.shape
    m = nb * r

    x = x_ref[...]
    ms = jnp.mean(x * x, axis=-1, keepdims=True)
    h = (x * lax.rsqrt(ms + EPS) * ng_ref[...]) * (1.0 + sc_ref[...]) + sh_ref[...]
    h = h.reshape(m, d).astype(_BF16)

    def seg(a, b):
        return _dot(h, w_ref[:, a:b])

    o = 0
    q = seg(o, o + WIDTH_A); o += WIDTH_A
    k = seg(o, o + WIDTH_A); o += WIDTH_A
    v = seg(o, o + WIDTH_A); o += WIDTH_A
    qi = seg(o, o + IDX_HEADS * IDX_DIM); o += IDX_HEADS * IDX_DIM
    ki = seg(o, o + KI_PAD); o += KI_PAD
    wi = seg(o, o + WI_PAD); o += WI_PAD
    u = seg(o, o + WIDTH_B); o += WIDTH_B
    vg = seg(o, o + WIDTH_B)

    bd = bd_ref[...]

    def head_norm(t, g):
        hi, lo = _split_bf16(t * t)
        ss = _dot(hi, bd) + _dot(lo, bd)
        return t * lax.rsqrt(ss * (1.0 / HEAD_DIM) + EPS) * g

    cos1 = jnp.broadcast_to(cos_ref[...], (nb, r, LANES)).reshape(m, LANES)
    sin1 = jnp.broadcast_to(sin_ref[...], (nb, r, LANES)).reshape(m, LANES)

    def rope(t):
        w = t.shape[1]
        reps = w // LANES
        cosw = cos1 if reps == 1 else jnp.concatenate([cos1] * reps, axis=1)
        sinw = sin1 if reps == 1 else jnp.concatenate([sin1] * reps, axis=1)
        lane = lax.broadcasted_iota(jnp.int32, (1, w), 1)
        first_half = (lane % HEAD_DIM) < (HEAD_DIM // 2)
        rot = jnp.where(first_half, pltpu.roll(t, w - HEAD_DIM // 2, 1), pltpu.roll(t, HEAD_DIM // 2, 1))
        return t * cosw + rot * sinw

    qr = rope(head_norm(q, qg_ref[...]))
    kr = rope(head_norm(k, kg_ref[...]))
    qir = rope(qi)
    kir = rope(ki)
    wis = wi * ((IDX_DIM * IDX_HEADS) ** -0.5)

    ug = jax.nn.gelu(u)
    vgg = jax.nn.gelu(vg)
    mu = jnp.mean(vgg, axis=-1, keepdims=True)
    xc = vgg - mu
    var = jnp.mean(xc * xc, axis=-1, keepdims=True)
    vn = xc * lax.rsqrt(var + EPS) * lng_ref[...] + lnb_ref[...]
    vnb = vn.astype(_BF16)
    rowi = lax.broadcasted_iota(jnp.int32, (cb, cb), 0)
    coli = lax.broadcasted_iota(jnp.int32, (cb, cb), 1)
    ws_m = [jnp.where(rowi >= coli, ws_ref[g], 0.0).astype(_BF16) for g in range(N_GROUPS_B)]
    gm_rows = []
    for c in range(m // cb):
        pieces = []
        for g in range(N_GROUPS_B):
            lanes = slice(g * GROUP_DIM_B, (g + 1) * GROUP_DIM_B)
            mixed = _dot(ws_m[g], vnb[c * cb:(c + 1) * cb, lanes]) + bs_ref[:, lanes]
            pieces.append(ug[c * cb:(c + 1) * cb, lanes] * mixed)
        gm_rows.append(jnp.concatenate(pieces, axis=1))
    gm = gm_rows[0] if len(gm_rows) == 1 else jnp.concatenate(gm_rows, axis=0)

    def put(ref, val):
        ref[...] = val.astype(ref.dtype).reshape(ref.shape)

    if prompt:
        k_out, v_out, ki_out, q_bf, gm_out, kt_bf, v_bf, ki_bf, qit_bf, wit_out = out_refs
        put(kt_bf, kr.T)
        put(ki_bf, kir[:, :IDX_DIM])
        put(qit_bf, qir.T)
        put(wit_out, wis.T[:8, :])
        lane = lax.broadcasted_iota(jnp.int32, (1, LANES), 1)
        pieces = []
        for hd in range(N_HEADS_A):
            src = v[:, (hd // 2) * LANES:(hd // 2 + 1) * LANES]
            if hd % 2:
                src = pltpu.roll(src, HEAD_DIM, 1)
            pieces.append(jnp.where(lane < HEAD_DIM, src, 1.0))
        put(v_bf, jnp.concatenate(pieces, axis=1))
    else:
        k_out, v_out, ki_out, q_bf, gm_out, qi_bf, wi_out, vn_out = out_refs
        put(qi_bf, qir)
        put(wi_out, wis)
        put(vn_out, vn)
    put(k_out, kr)
    put(v_out, v)
    put(ki_out, kir[:, :IDX_DIM])
    put(q_bf, qr * Q_SCALE)
    put(gm_out, gm)


def _proj_call(x3, mod3, norm_g, w_in_p, cos_t, sin_t, bd, qg, kg, lng, lnb, ws, bs_b, *, nb, r, prompt):
    nseq, t, d = x3.shape
    m = nb * r
    cb = ws.shape[1]
    grid = (nseq // nb, t // r)
    const2 = lambda i, j: (0, 0)
    const3 = lambda i, j: (0, 0, 0)
    row_spec = lambda c: pl.BlockSpec((nb, r, c), lambda i, j: (i, j, 0))
    in_specs = [
        row_spec(d),
        pl.BlockSpec((nb, 1, d), lambda i, j: (i, 0, 0)),
        pl.BlockSpec((nb, 1, d), lambda i, j: (i, 0, 1)),
        pl.BlockSpec((1, d), const2),
        pl.BlockSpec(w_in_p.shape, const2, pipeline_mode=pl.Buffered(1)),
        pl.BlockSpec((1, r, LANES), lambda i, j: (0, j, 0)),
        pl.BlockSpec((1, r, LANES), lambda i, j: (0, j, 0)),
        pl.BlockSpec(bd.shape, const2),
        pl.BlockSpec((1, WIDTH_A), const2),
        pl.BlockSpec((1, WIDTH_A), const2),
        pl.BlockSpec((1, WIDTH_B), const2),
        pl.BlockSpec((1, WIDTH_B), const2),
        pl.BlockSpec(ws.shape, const3),
        pl.BlockSpec(bs_b.shape, const2),
    ]
    out_shape = [
        jax.ShapeDtypeStruct((nseq, t, WIDTH_A), _F32),
        jax.ShapeDtypeStruct((nseq, t, WIDTH_A), _F32),
        jax.ShapeDtypeStruct((nseq, t, IDX_DIM), _F32),
        jax.ShapeDtypeStruct((nseq, t, WIDTH_A), _BF16),
        jax.ShapeDtypeStruct((nseq, t, WIDTH_B), _BF16),
    ]
    out_specs = [row_spec(WIDTH_A), row_spec(WIDTH_A), row_spec(IDX_DIM), row_spec(WIDTH_A), row_spec(WIDTH_B)]
    if prompt:
        assert nb == 1
        nt = t // r
        out_shape += [
            jax.ShapeDtypeStruct((nseq, nt, WIDTH_A, r), _BF16),
            jax.ShapeDtypeStruct((nseq, t, N_HEADS_A * LANES), _BF16),
            jax.ShapeDtypeStruct((nseq, t, IDX_DIM), _BF16),
            jax.ShapeDtypeStruct((nseq, IDX_HEADS * IDX_DIM, t), _BF16),
            jax.ShapeDtypeStruct((nseq, 8, t), _F32),
        ]
        out_specs += [
            pl.BlockSpec((1, 1, WIDTH_A, r), lambda i, j: (i, j, 0, 0)),
            row_spec(N_HEADS_A * LANES),
            row_spec(IDX_DIM),
            pl.BlockSpec((1, IDX_HEADS * IDX_DIM, r), lambda i, j: (i, 0, j)),
            pl.BlockSpec((1, 8, r), lambda i, j: (i, 0, j)),
        ]
    else:
        out_shape += [
            jax.ShapeDtypeStruct((nseq, t, IDX_HEADS * IDX_DIM), _BF16),
            jax.ShapeDtypeStruct((nseq, t, LANES), _F32),
            jax.ShapeDtypeStruct((nseq, t, WIDTH_B), _F32),
        ]
        out_specs += [row_spec(IDX_HEADS * IDX_DIM), row_spec(LANES), row_spec(WIDTH_B)]
    return pl.pallas_call(
        functools.partial(_proj_kernel, prompt=prompt, cb=cb),
        grid=grid,
        in_specs=in_specs,
        out_specs=out_specs,
        out_shape=out_shape,
        compiler_params=pltpu.CompilerParams(
            dimension_semantics=("arbitrary", "arbitrary"), vmem_limit_bytes=VMEM_LIMIT),
        name="proj_prompt" if prompt else "proj_sample",
    )(x3, mod3, mod3, norm_g, w_in_p, cos_t, sin_t, bd, qg, kg, lng, lnb, ws, bs_b)


def _key_to_f32(key):
    bits = jnp.where(key >= 0, key, key ^ jnp.int32(0x7FFFFFFF))
    return lax.bitcast_convert_type(bits, _F32)


def _count_keys(sct_ref, nkb, preds):
    _, tk, nq = sct_ref.shape
    g = COUNT_ACC_ROWS

    def body(j, accs):
        x = sct_ref[j]
        return tuple(acc + jnp.where(pred(x), 1.0, 0.0).reshape(tk // g, g, nq).sum(axis=0)
                     for acc, pred in zip(accs, preds))

    accs = lax.fori_loop(0, nkb, body, tuple(jnp.zeros((g, nq), _F32) for _ in preds))
    return [jnp.sum(acc, axis=0, keepdims=True) for acc in accs]


def _scores_to_bias(sct_ref, bias_ref, nkb, topk):
    nblk, tk, nq = sct_ref.shape
    kf = jnp.float32(topk)
    total = jnp.asarray(nkb * tk).astype(_F32)

    def mid_of(lo, hi):
        return lo + lax.shift_right_logical(hi - lo, 1)

    def unsettled_rows(lo, cnt_lo, mid):
        return jnp.max(jnp.where((cnt_lo != kf) & (mid != lo), 1, 0))

    def cond(st):
        return (st[5] > 0) & (st[6] < 34)

    def body(st):
        return step(step(st))

    def step(st):
        lo, hi, cnt_lo, cnt_hi, mid, _, it = st
        thr_mid = _key_to_f32(mid)
        cnt = _count_keys(sct_ref, nkb, [lambda x: x >= thr_mid])[0]
        ge = cnt >= kf
        lo = jnp.where(ge, mid, lo)
        cnt_lo = jnp.where(ge, cnt, cnt_lo)
        hi = jnp.where(ge, hi, mid)
        cnt_hi = jnp.where(ge, cnt_hi, cnt)
        mid = mid_of(lo, hi)
        return lo, hi, cnt_lo, cnt_hi, mid, unsettled_rows(lo, cnt_lo, mid), it + 1

    c_ge0, c_gt0 = _count_keys(sct_ref, nkb, [lambda x: x >= 0.0, lambda x: x > 0.0])
    is_pos = c_gt0 >= kf
    is_zero = c_ge0 >= kf
    lo = jnp.where(is_pos, 1, jnp.where(is_zero, 0, KEY_NEG_INF))
    hi = jnp.where(is_pos, KEY_POS_INF, jnp.where(is_zero, 1, -1))
    cnt_lo = jnp.where(is_pos, c_gt0, jnp.where(is_zero, c_ge0, total))
    cnt_hi = jnp.where(is_pos, 0.0, jnp.where(is_zero, c_gt0, c_ge0))
    mid = mid_of(lo, hi)
    init = (lo, hi, cnt_lo, cnt_hi, mid, unsettled_rows(lo, cnt_lo, mid), jnp.int32(0))
    lo, hi, cnt_lo, cnt_hi = lax.while_loop(cond, body, init)[:4]

    thr = _key_to_f32(lo)
    thr = jnp.where(jnp.abs(thr) < F32_MIN_NORMAL, 0.0, thr)
    thr = jnp.maximum(thr, -F32_MAX)
    tie = (cnt_lo > kf) & (lo > KEY_NEG_INF)
    need = jnp.where(lo == 1, -1.0, jnp.where(tie, kf - cnt_hi, 1e9))

    rowi = lax.broadcasted_iota(jnp.int32, (LANES, LANES), 0)
    coli = lax.broadcasted_iota(jnp.int32, (LANES, LANES), 1)
    tri = jnp.where(rowi >= coli, 1.0, 0.0).astype(_BF16)

    def fill(j, base):
        for c in range(tk // LANES):
            x = sct_ref[j, c * LANES:(c + 1) * LANES, :]
            eq = x == thr
            rank = base + _dot(tri, jnp.where(eq, 1.0, 0.0).astype(_BF16))
            keep = (x > thr) | (eq & (rank <= need))
            bias_ref[j, :, c * LANES:(c + 1) * LANES] = jnp.where(keep, 0.0, MASK_BIAS).T
            base = rank[LANES - 1:LANES, :]
        return base

    lax.fori_loop(0, nkb, fill, jnp.zeros((1, nq), _F32))


def _dsa_prompt_kernel(q_ref, qit_ref, wit_ref, kt_ref, v_ref, ki_ref, o_ref, sct_ref, bias_ref, m_ref, acc_ref, *,
                       topk):
    i = pl.program_id(1)
    tq = q_ref.shape[1]
    tk = ki_ref.shape[2]
    nkb = ((i + 1) * tq + tk - 1) // tk

    wit = wit_ref[0]
    q_pos = i * tq + lax.broadcasted_iota(jnp.int32, (1, tq), 1)
    vis_lim = (q_pos // CHUNK + 1) * CHUNK
    key_in_block = lax.broadcasted_iota(jnp.int32, (tk, tq), 0)

    def score_body(j, carry):
        ki = ki_ref[0, j]
        acc = jnp.zeros((tk, tq), _F32)
        for h in range(IDX_HEADS):
            s = _dot(ki, qit_ref[0, h * IDX_DIM:(h + 1) * IDX_DIM, :])
            acc = acc + wit[h:h + 1, :] * jnp.maximum(s, 0.0)
        sct_ref[j] = jnp.where((j * tk + key_in_block) < vis_lim, acc, -jnp.inf)
        return carry

    lax.fori_loop(0, nkb, score_body, 0)
    _scores_to_bias(sct_ref, bias_ref, nkb, topk)

    reps = tk // LANES
    q_heads = [q_ref[0, :, h * HEAD_DIM:(h + 1) * HEAD_DIM] for h in range(N_HEADS_A)]

    def logits(j, h):
        return _dot(q_heads[h], kt_ref[0, j, h * HEAD_DIM:(h + 1) * HEAD_DIM, :]) + bias_ref[j]

    m_ref[...] = jnp.full(m_ref.shape, -jnp.inf, _F32)

    def max_body(j, carry):
        for h in range(N_HEADS_A):
            s = logits(j, h)
            part = s[:, :LANES]
            for c in range(1, reps):
                part = jnp.maximum(part, s[:, c * LANES:(c + 1) * LANES])
            m_ref[h] = jnp.maximum(m_ref[h], part)
        return carry

    lax.fori_loop(0, nkb, max_body, 0)
    for h in range(N_HEADS_A):
        m_ref[h] = jnp.broadcast_to(jnp.max(m_ref[h], axis=1, keepdims=True), (tq, LANES))

    acc_ref[...] = jnp.zeros(acc_ref.shape, _F32)

    def att_body(j, carry):
        for h in range(N_HEADS_A):
            p = jnp.exp2(logits(j, h) - jnp.concatenate([m_ref[h]] * reps, axis=1))
            acc_ref[h] += _dot(p.astype(_BF16), v_ref[0, j, :, h * LANES:(h + 1) * LANES])
        return carry

    lax.fori_loop(0, nkb, att_body, 0)
    lane = lax.broadcasted_iota(jnp.int32, (1, LANES), 1)
    for hp in range(N_HEADS_A // 2):
        a0, a1 = acc_ref[2 * hp], acc_ref[2 * hp + 1]
        out0 = a0 / pltpu.roll(a0, HEAD_DIM, 1)
        out1 = pltpu.roll(a1, HEAD_DIM, 1) / a1
        o_ref[0, :, hp * LANES:(hp + 1) * LANES] = jnp.where(lane < HEAD_DIM, out0, out1).astype(o_ref.dtype)


def _dsa_prompt_call(q_bf, qit_bf, wit, kt_bf, v_bf, ki_bf, *, topk):
    b, t, _ = q_bf.shape
    nt, tk = kt_bf.shape[1], kt_bf.shape[3]
    tq = min(DSA_TQ, t)
    v4 = v_bf.reshape(b, nt, tk, N_HEADS_A * LANES)
    ki4 = ki_bf.reshape(b, nt, tk, IDX_DIM)
    resident = lambda shape: pl.BlockSpec((1,) + shape, lambda bi, i: (bi, 0, 0, 0), pipeline_mode=pl.Buffered(1))
    return pl.pallas_call(
        functools.partial(_dsa_prompt_kernel, topk=topk),
        grid=(b, t // tq),
        in_specs=[
            pl.BlockSpec((1, tq, WIDTH_A), lambda bi, i: (bi, i, 0)),
            pl.BlockSpec((1, IDX_HEADS * IDX_DIM, tq), lambda bi, i: (bi, 0, i)),
            pl.BlockSpec((1, 8, tq), lambda bi, i: (bi, 0, i)),
            resident((nt, WIDTH_A, tk)),
            resident((nt, tk, N_HEADS_A * LANES)),
            resident((nt, tk, IDX_DIM)),
        ],
        out_specs=pl.BlockSpec((1, tq, WIDTH_A), lambda bi, i: (bi, i, 0)),
        out_shape=jax.ShapeDtypeStruct((b, t, WIDTH_A), _BF16),
        scratch_shapes=[
            pltpu.VMEM((nt, tk, tq), _F32),
            pltpu.VMEM((nt, tq, tk), _F32),
            pltpu.VMEM((N_HEADS_A, tq, LANES), _F32),
            pltpu.VMEM((N_HEADS_A, tq, LANES), _F32),
        ],
        compiler_params=pltpu.CompilerParams(
            dimension_semantics=("arbitrary", "arbitrary"), vmem_limit_bytes=VMEM_LIMIT),
        name="dsa_prompt",
    )(q_bf, qit_bf, wit, kt_bf, v4, ki4)


def _dsa_sample_kernel(q_ref, qi_ref, wi_ref, kn_ref, vn_ref, kin_ref, ck_ref, cv_ref, cki_ref, o_ref,
                       sct_ref, bias_ref, kall_ref, vall_ref, kiall_ref, *, topk):
    t = q_ref.shape[1]
    past = ck_ref.shape[2]
    s_pad, nq = sct_ref.shape[1], sct_ref.shape[2]
    tail = s_pad - past

    kall_ref[:past, :] = ck_ref[0, 0].astype(_BF16)
    vall_ref[:past, :] = cv_ref[0, 0].astype(_BF16)
    kiall_ref[:past, :] = cki_ref[0, 0].astype(_BF16)
    zpad = jnp.zeros((tail - t, WIDTH_A), _BF16)
    kall_ref[past:, :] = jnp.concatenate([kn_ref[0].astype(_BF16), zpad], axis=0)
    vall_ref[past:, :] = jnp.concatenate([vn_ref[0].astype(_BF16), zpad], axis=0)
    kiall_ref[past:, :] = jnp.concatenate([kin_ref[0].astype(_BF16), jnp.zeros((tail - t, IDX_DIM), _BF16)], axis=0)

    qit = jnp.concatenate([qi_ref[0].astype(_F32)] * (nq // t), axis=0).T.astype(_BF16)
    wit = jnp.concatenate([wi_ref[0]] * (nq // t), axis=0).T
    acc = jnp.zeros((s_pad, nq), _F32)
    for h in range(IDX_HEADS):
        s = _dot(kiall_ref[...], qit[h * IDX_DIM:(h + 1) * IDX_DIM, :])
        acc = acc + wit[h:h + 1, :] * jnp.maximum(s, 0.0)
    key_pos = lax.broadcasted_iota(jnp.int32, (s_pad, nq), 0)
    sct_ref[0] = jnp.where(key_pos < past + t, acc, -jnp.inf)
    _scores_to_bias(sct_ref, bias_ref, 1, topk)

    nrow = N_HEADS_A * t
    row_head = lax.broadcasted_iota(jnp.int32, (nrow, WIDTH_A), 0) // t
    col_head = lax.broadcasted_iota(jnp.int32, (nrow, WIDTH_A), 1) // HEAD_DIM
    own = row_head == col_head
    q_rep = jnp.concatenate([q_ref[0]] * N_HEADS_A, axis=0)
    q_bd = jnp.where(own, q_rep, jnp.zeros_like(q_rep))
    bias = bias_ref[0, :t, :]
    s = _dot_nt(q_bd, kall_ref[...]) + jnp.concatenate([bias] * N_HEADS_A, axis=0)
    m = jnp.max(s, axis=1, keepdims=True)
    p = jnp.exp2(s - m)
    l = jnp.sum(p, axis=1, keepdims=True)
    pv = _dot(p.astype(_BF16), vall_ref[...]) / l
    pv = jnp.where(own, pv, 0.0)
    out = pv[:t]
    for h in range(1, N_HEADS_A):
        out = out + pv[h * t:(h + 1) * t]
    o_ref[0] = out.astype(o_ref.dtype)


def _dsa_sample_call(q_bf, qi_bf, wi, k_new, v_new, ki_new, cache_k, cache_v, cache_kidx, *, topk):
    b, t, _ = q_bf.shape
    past = cache_k.shape[2]
    s_pad = past + LANES
    assert LANES % t == 0
    row = lambda c: pl.BlockSpec((1, t, c), lambda bi: (bi, 0, 0))
    cache = lambda c: pl.BlockSpec((1, 1, past, c), lambda bi: (0, bi, 0, 0))
    return pl.pallas_call(
        functools.partial(_dsa_sample_kernel, topk=topk),
        grid=(b,),
        in_specs=[row(WIDTH_A), row(IDX_HEADS * IDX_DIM), row(LANES), row(WIDTH_A), row(WIDTH_A), row(IDX_DIM),
                  cache(WIDTH_A), cache(WIDTH_A), cache(IDX_DIM)],
        out_specs=row(WIDTH_A),
        out_shape=jax.ShapeDtypeStruct((b, t, WIDTH_A), _BF16),
        scratch_shapes=[
            pltpu.VMEM((1, s_pad, LANES), _F32),
            pltpu.VMEM((1, LANES, s_pad), _F32),
            pltpu.VMEM((s_pad, WIDTH_A), _BF16),
            pltpu.VMEM((s_pad, WIDTH_A), _BF16),
            pltpu.VMEM((s_pad, IDX_DIM), _BF16),
        ],
        compiler_params=pltpu.CompilerParams(dimension_semantics=("arbitrary",), vmem_limit_bytes=VMEM_LIMIT),
        name="dsa_sample",
    )(q_bf, qi_bf, wi, k_new, v_new, ki_new, cache_k, cache_v, cache_kidx)


def _out_kernel(x_ref, at_ref, gm_ref, g1_ref, sh2_ref, sc2_ref, g2_ref, ng_ref, wo_ref, w1_ref, w2_ref, y_ref):
    nb, r, d = x_ref.shape
    m = nb * r
    at = at_ref[...].reshape(m, WIDTH_A)
    gm = gm_ref[...].reshape(m, WIDTH_B)
    y = _dot(at, wo_ref[:WIDTH_A, :]) + _dot(gm, wo_ref[WIDTH_A:, :])
    x1 = x_ref[...] + g1_ref[...] * y.reshape(nb, r, d)
    ms = jnp.mean(x1 * x1, axis=-1, keepdims=True)
    h2 = (x1 * lax.rsqrt(ms + EPS) * ng_ref[...]) * (1.0 + sc2_ref[...]) + sh2_ref[...]
    h2 = h2.reshape(m, d).astype(_BF16)
    dff = w1_ref.shape[1]
    fc = 1024
    ff = jnp.zeros((m, d), _F32)
    for c in range(dff // fc):
        a = jnp.maximum(_dot(h2, w1_ref[:, c * fc:(c + 1) * fc]), 0.0)
        ff = ff + _dot((a * a).astype(_BF16), w2_ref[c * fc:(c + 1) * fc, :])
    y_ref[...] = x1 + g2_ref[...] * ff.reshape(nb, r, d)


def _out_call(x3, attn, gm, mod3, norm_g, w_out, w_ff1, w_ff2, *, nb, r, name):
    nseq, t, d = x3.shape
    const2 = lambda i, j: (0, 0)
    row_spec = lambda c: pl.BlockSpec((nb, r, c), lambda i, j: (i, j, 0))
    mod_spec = lambda col: pl.BlockSpec((nb, 1, d), lambda i, j: (i, 0, col))
    return pl.pallas_call(
        _out_kernel,
        grid=(nseq // nb, t // r),
        in_specs=[row_spec(d), row_spec(WIDTH_A), row_spec(WIDTH_B),
                  mod_spec(2), mod_spec(3), mod_spec(4), mod_spec(5),
                  pl.BlockSpec((1, d), const2),
                  pl.BlockSpec(w_out.shape, const2, pipeline_mode=pl.Buffered(1)),
                  pl.BlockSpec(w_ff1.shape, const2, pipeline_mode=pl.Buffered(1)),
                  pl.BlockSpec(w_ff2.shape, const2, pipeline_mode=pl.Buffered(1))],
        out_specs=row_spec(d),
        out_shape=jax.ShapeDtypeStruct((nseq, t, d), _F32),
        compiler_params=pltpu.CompilerParams(
            dimension_semantics=("arbitrary", "arbitrary"), vmem_limit_bytes=VMEM_LIMIT),
        name=name,
    )(x3, attn, gm, mod3, mod3, mod3, mod3, norm_g, w_out, w_ff1, w_ff2)


def _rope_tables(pos):
    half = HEAD_DIM // 2
    inv_freq = jnp.power(jnp.float32(ROPE_THETA), -jnp.arange(half, dtype=_F32) / half)
    ang = pos.astype(_F32)[:, None] * inv_freq[None, :]
    cos, sin = jnp.cos(ang), jnp.sin(ang)
    reps = LANES // HEAD_DIM
    cos_t = jnp.tile(jnp.concatenate([cos, cos], axis=1), (1, reps))
    sin_t = jnp.tile(jnp.concatenate([-sin, sin], axis=1), (1, reps))
    return cos_t[None], sin_t[None]


def _pad_w_in(w_in):
    d = w_in.shape[0]
    a, i = WIDTH_A, IDX_HEADS * IDX_DIM
    o_ki = 3 * a + i
    o_wi = o_ki + IDX_DIM
    o_u = o_wi + IDX_HEADS
    z = lambda n: jnp.zeros((d, n), w_in.dtype)
    cols = [w_in[:, :o_ki], w_in[:, o_ki:o_wi], z(KI_PAD - IDX_DIM), w_in[:, o_wi:o_u], z(WI_PAD - IDX_HEADS),
            w_in[:, o_u:]]
    return jnp.concatenate(cols, axis=1).astype(_BF16)


def _layer(x_prompt, x_sample, c_prompt, c_sample, cache_k, cache_v, cache_kidx, w_ada, b_ada, norm1_g, norm2_g,
           w_in, q_norm_g, k_norm_g, gmlp_ln_g, gmlp_ln_b, gmlp_ws, gmlp_bs, w_out, w_ff1, w_ff2):
    bp, tp, d = x_prompt.shape
    bs_, ts, _ = x_sample.shape
    past = cache_k.shape[1]

    c_all = jnp.concatenate([c_prompt, c_sample], axis=0)
    rows = -(-c_all.shape[0] // 16) * 16
    c_all = jnp.pad(c_all, ((0, rows - c_all.shape[0]), (0, 0)))
    mod = _ada_call(c_all, w_ada, b_ada[None, :])
    mod_p = mod[:bp, None, :]
    mod_s = mod[bp:bp + bs_, None, :]

    w_in_p = _pad_w_in(w_in)
    w_out_b, w_ff1_b, w_ff2_b = w_out.astype(_BF16), w_ff1.astype(_BF16), w_ff2.astype(_BF16)
    head_of = jnp.arange(WIDTH_A) // HEAD_DIM
    bd = (head_of[:, None] == head_of[None, :]).astype(_BF16)
    qg = jnp.tile(q_norm_g, N_HEADS_A)[None, :]
    kg = jnp.tile(k_norm_g, N_HEADS_A)[None, :]
    lng, lnb = gmlp_ln_g[None, :], gmlp_ln_b[None, :]
    n1, n2 = norm1_g[None, :], norm2_g[None, :]

    rp = min(PROJ_ROWS, tp)
    cos_p, sin_p = _rope_tables(jnp.arange(tp))
    lp = min(tp, GMLP_CHUNK)
    bs_p = jnp.repeat(jnp.transpose(gmlp_bs[:, :lp]), GROUP_DIM_B, axis=1)
    (kp, vp, kip, q_bf, gm_p, kt_bf, v_bf, ki_bf, qit_bf, wit) = _proj_call(
        x_prompt, mod_p, n1, w_in_p, cos_p, sin_p, bd, qg, kg, lng, lnb, gmlp_ws[:, :lp, :lp], bs_p,
        nb=1, r=rp, prompt=True)
    attn_p = _dsa_prompt_call(q_bf, qit_bf, wit, kt_bf, v_bf, ki_bf, topk=min(TOPK_MAX, tp // 4))
    yp = _out_call(x_prompt, attn_p, gm_p, mod_p, n2, w_out_b, w_ff1_b, w_ff2_b, nb=1, r=rp, name="out_prompt")

    nb = min(SAMPLE_GROUP, bs_)
    cos_s, sin_s = _rope_tables(past + jnp.arange(ts))
    ls = min(ts, GMLP_CHUNK)
    assert ls == ts
    eye = jnp.eye(nb, dtype=gmlp_ws.dtype)
    ws_s = jax.vmap(lambda w: jnp.kron(eye, w))(gmlp_ws[:, :ls, :ls])
    bs_s = jnp.tile(jnp.repeat(jnp.transpose(gmlp_bs[:, :ls]), GROUP_DIM_B, axis=1), (nb, 1))
    (ks, vs, kis, qs_bf, gm_s, qis_bf, wi_s, gvs) = _proj_call(
        x_sample, mod_s, n1, w_in_p, cos_s, sin_s, bd, qg, kg, lng, lnb, ws_s, bs_s,
        nb=nb, r=ts, prompt=False)
    attn_s = _dsa_sample_call(
        qs_bf, qis_bf, wi_s, ks, vs, kis,
        cache_k.reshape(1, bs_, past, WIDTH_A), cache_v.reshape(1, bs_, past, WIDTH_A),
        cache_kidx.reshape(1, bs_, past, IDX_DIM), topk=min(TOPK_MAX, (past + ts) // 4))
    ys = _out_call(x_sample, attn_s, gm_s, mod_s, n2, w_out_b, w_ff1_b, w_ff2_b, nb=nb, r=ts, name="out_sample")

    heads = lambda a: a.reshape(a.shape[0], a.shape[1], N_HEADS_A, HEAD_DIM)
    return yp, ys, heads(kp), heads(vp), kip, heads(ks), heads(vs), kis, gvs


def kernel(x_prompt, x_sample, c_prompt, c_sample, cache_k, cache_v, cache_kidx, w_ada, b_ada, norm1_g, norm2_g,
           w_in, q_norm_g, k_norm_g, gmlp_ln_g, gmlp_ln_b, gmlp_ws, gmlp_bs, w_out, w_ff1, w_ff2):
    depth = w_ada.shape[0]
    yp, ys = x_prompt, x_sample
    outs = [[] for _ in range(7)]
    for l in range(depth):
        res = _layer(yp, ys, c_prompt, c_sample, cache_k[l], cache_v[l], cache_kidx[l], w_ada[l], b_ada[l],
                     norm1_g[l], norm2_g[l], w_in[l], q_norm_g[l], k_norm_g[l], gmlp_ln_g[l], gmlp_ln_b[l],
                     gmlp_ws[l], gmlp_bs[l], w_out[l], w_ff1[l], w_ff2[l])
        yp, ys = res[0], res[1]
        for acc, leaf in zip(outs, res[2:]):
            acc.append(leaf)
    return (yp, ys) + tuple(jnp.stack(o) for o in outs)
```

```python
import functools

import jax
import jax.numpy as jnp
from jax import lax
from jax.experimental import pallas as pl
from jax.experimental.pallas import tpu as pltpu

N_HEADS_A = 8
HEAD_DIM = 64
WIDTH_A = N_HEADS_A * HEAD_DIM
IDX_HEADS = 4
IDX_DIM = 64
N_GROUPS_B = 4
GROUP_DIM_B = 128
WIDTH_B = N_GROUPS_B * GROUP_DIM_B
GMLP_CHUNK = 128
CHUNK = 64
TOPK_MAX = 256
ROPE_THETA = 10000.0
EPS = 1e-6

LANES = 128
KI_PAD = LANES
WI_PAD = LANES
IN_WIDTH_PADDED = 3 * WIDTH_A + IDX_HEADS * IDX_DIM + KI_PAD + WI_PAD + 2 * WIDTH_B

PROJ_ROWS = 512
DSA_TQ = 256
DSA_TK = PROJ_ROWS
SAMPLE_GROUP = 8
COUNT_ACC_ROWS = 32
VMEM_LIMIT = 56 * 1024 * 1024

MASK_BIAS = -1e30
Q_SCALE = HEAD_DIM ** -0.5 * 1.4426950408889634
SHIFT_LIMIT = 30.0
KEY_NEG_INF = -2139095041
KEY_POS_INF = 2139095040
F32_MIN_NORMAL = 1.1754944e-38
F32_MAX = 3.4028235e38

_F32 = jnp.float32
_BF16 = jnp.bfloat16


def _dot(a, b):
    return jnp.dot(a, b, preferred_element_type=_F32)


def _dot_nt(a, b):
    return lax.dot_general(a, b, (((1,), (1,)), ((), ())), preferred_element_type=_F32)


def _split_bf16(a):
    hi = a.astype(_BF16)
    lo = (a - hi.astype(_F32)).astype(_BF16)
    return hi, lo


def _ada_kernel(c_ref, w_ref, b_ref, o_ref):
    c = c_ref[...]
    s_hi, s_lo = _split_bf16(c * jax.nn.sigmoid(c))
    w_hi, w_lo = _split_bf16(w_ref[...])
    o_ref[...] = _dot(s_hi, w_hi) + _dot(s_lo, w_hi) + _dot(s_hi, w_lo) + b_ref[...]


def _ada_call(c, w_ada, b_ada):
    rows, d = c.shape
    n = w_ada.shape[1]
    tn = 1024
    return pl.pallas_call(
        _ada_kernel,
        grid=(n // tn,),
        in_specs=[
            pl.BlockSpec((rows, d), lambda j: (0, 0)),
            pl.BlockSpec((d, tn), lambda j: (0, j)),
            pl.BlockSpec((1, tn), lambda j: (0, j)),
        ],
        out_specs=pl.BlockSpec((rows, tn), lambda j: (0, j)),
        out_shape=jax.ShapeDtypeStruct((rows, n), _F32),
        compiler_params=pltpu.CompilerParams(dimension_semantics=("arbitrary",), vmem_limit_bytes=VMEM_LIMIT),
        name="ada",
    )(c, w_ada, b_ada)


def _proj_kernel(x_ref, sh_ref, sc_ref, ng_ref, w_ref, cos_ref, sin_ref, bd_ref, qg_ref, kg_ref,
                 lng_ref, lnb_ref, ws_ref, bs_ref, *out_refs, prompt, cb):
    nb, r, d = x_ref.shape
    m = nb * r

    x = x_ref[...]
    ms = jnp.mean(x * x, axis=-1, keepdims=True)
    h = (x * lax.rsqrt(ms + EPS) * ng_ref[...]) * (1.0 + sc_ref[...]) + sh_ref[...]
    h = h.reshape(m, d).astype(_BF16)

    def seg(a, b):
        return _dot(h, w_ref[:, a:b])

    o = 0
    q = seg(o, o + WIDTH_A); o += WIDTH_A
    k = seg(o, o + WIDTH_A); o += WIDTH_A
    v = seg(o, o + WIDTH_A); o += WIDTH_A
    qi = seg(o, o + IDX_HEADS * IDX_DIM); o += IDX_HEADS * IDX_DIM
    ki = seg(o, o + KI_PAD); o += KI_PAD
    wi = seg(o, o + WI_PAD); o += WI_PAD
    u = seg(o, o + WIDTH_B); o += WIDTH_B
    vg = seg(o, o + WIDTH_B)

    bd = bd_ref[...]

    def head_norm(t, g):
        hi, lo = _split_bf16(t * t)
        ss = _dot(hi, bd) + _dot(lo, bd)
        return t * lax.rsqrt(ss * (1.0 / HEAD_DIM) + EPS) * g

    cos1 = jnp.broadcast_to(cos_ref[...], (nb, r, LANES)).reshape(m, LANES)
    sin1 = jnp.broadcast_to(sin_ref[...], (nb, r, LANES)).reshape(m, LANES)

    def rope(t):
        w = t.shape[1]
        reps = w // LANES
        cosw = cos1 if reps == 1 else jnp.concatenate([cos1] * reps, axis=1)
        sinw = sin1 if reps == 1 else jnp.concatenate([sin1] * reps, axis=1)
        lane = lax.broadcasted_iota(jnp.int32, (1, w), 1)
        first_half = (lane % HEAD_DIM) < (HEAD_DIM // 2)
        rot = jnp.where(first_half, pltpu.roll(t, w - HEAD_DIM // 2, 1), pltpu.roll(t, HEAD_DIM // 2, 1))
        return t * cosw + rot * sinw

    qr = rope(head_norm(q, qg_ref[...]))
    kr = rope(head_norm(k, kg_ref[...]))
    qir = rope(qi)
    kir = rope(ki)
    wis = wi * ((IDX_DIM * IDX_HEADS) ** -0.5)

    ug = jax.nn.gelu(u)
    vgg = jax.nn.gelu(vg)
    mu = jnp.mean(vgg, axis=-1, keepdims=True)
    xc = vgg - mu
    var = jnp.mean(xc * xc, axis=-1, keepdims=True)
    vn = xc * lax.rsqrt(var + EPS) * lng_ref[...] + lnb_ref[...]
    vnb = vn.astype(_BF16)
    rowi = lax.broadcasted_iota(jnp.int32, (cb, cb), 0)
    coli = lax.broadcasted_iota(jnp.int32, (cb, cb), 1)
    ws_m = [jnp.where(rowi >= coli, ws_ref[g], 0.0).astype(_BF16) for g in range(N_GROUPS_B)]
    gm_rows = []
    for c in range(m // cb):
        pieces = []
        for g in range(N_GROUPS_B):
            lanes = slice(g * GROUP_DIM_B, (g + 1) * GROUP_DIM_B)
            mixed = _dot(ws_m[g], vnb[c * cb:(c + 1) * cb, lanes]) + bs_ref[:, lanes]
            pieces.append(ug[c * cb:(c + 1) * cb, lanes] * mixed)
        gm_rows.append(jnp.concatenate(pieces, axis=1))
    gm = gm_rows[0] if len(gm_rows) == 1 else jnp.concatenate(gm_rows, axis=0)

    def put(ref, val):
        ref[...] = val.astype(ref.dtype).reshape(ref.shape)

    if prompt:
        k_out, v_out, ki_out, q_bf, gm_out, kt_bf, v_bf, ki_bf, qit_bf, wit_out = out_refs
        put(kt_bf, kr.T)
        put(ki_bf, kir[:, :IDX_DIM])
        put(qit_bf, qir.T)
        put(wit_out, wis.T[:8, :])
        lane = lax.broadcasted_iota(jnp.int32, (1, LANES), 1)
        pieces = []
        for hd in range(N_HEADS_A):
            src = v[:, (hd // 2) * LANES:(hd // 2 + 1) * LANES]
            if hd % 2:
                src = pltpu.roll(src, HEAD_DIM, 1)
            pieces.append(jnp.where(lane < HEAD_DIM, src, 1.0))
        put(v_bf, jnp.concatenate(pieces, axis=1))
    else:
        k_out, v_out, ki_out, q_bf, gm_out, qi_bf, wi_out, vn_out = out_refs
        put(qi_bf, qir)
        put(wi_out, wis)
        put(vn_out, vn)
    put(k_out, kr)
    put(v_out, v)
    put(ki_out, kir[:, :IDX_DIM])
    put(q_bf, qr * Q_SCALE)
    put(gm_out, gm)


def _proj_call(x3, mod3, norm_g, w_in_p, cos_t, sin_t, bd, qg, kg, lng, lnb, ws, bs_b, *, nb, r, prompt):
    nseq, t, d = x3.shape
    m = nb * r
    cb = ws.shape[1]
    grid = (nseq // nb, t // r)
    const2 = lambda i, j: (0, 0)
    const3 = lambda i, j: (0, 0, 0)
    row_spec = lambda c: pl.BlockSpec((nb, r, c), lambda i, j: (i, j, 0))
    in_specs = [
        row_spec(d),
        pl.BlockSpec((nb, 1, d), lambda i, j: (i, 0, 0)),
        pl.BlockSpec((nb, 1, d), lambda i, j: (i, 0, 1)),
        pl.BlockSpec((1, d), const2),
        pl.BlockSpec(w_in_p.shape, const2, pipeline_mode=pl.Buffered(1)),
        pl.BlockSpec((1, r, LANES), lambda i, j: (0, j, 0)),
        pl.BlockSpec((1, r, LANES), lambda i, j: (0, j, 0)),
        pl.BlockSpec(bd.shape, const2),
        pl.BlockSpec((1, WIDTH_A), const2),
        pl.BlockSpec((1, WIDTH_A), const2),
        pl.BlockSpec((1, WIDTH_B), const2),
        pl.BlockSpec((1, WIDTH_B), const2),
        pl.BlockSpec(ws.shape, const3),
        pl.BlockSpec(bs_b.shape, const2),
    ]
    out_shape = [
        jax.ShapeDtypeStruct((nseq, t, WIDTH_A), _F32),
        jax.ShapeDtypeStruct((nseq, t, WIDTH_A), _F32),
        jax.ShapeDtypeStruct((nseq, t, IDX_DIM), _F32),
        jax.ShapeDtypeStruct((nseq, t, WIDTH_A), _BF16),
        jax.ShapeDtypeStruct((nseq, t, WIDTH_B), _BF16),
    ]
    out_specs = [row_spec(WIDTH_A), row_spec(WIDTH_A), row_spec(IDX_DIM), row_spec(WIDTH_A), row_spec(WIDTH_B)]
    if prompt:
        assert nb == 1
        nt = t // r
        out_shape += [
            jax.ShapeDtypeStruct((nseq, nt, WIDTH_A, r), _BF16),
            jax.ShapeDtypeStruct((nseq, t, N_HEADS_A * LANES), _BF16),
            jax.ShapeDtypeStruct((nseq, t, IDX_DIM), _BF16),
            jax.ShapeDtypeStruct((nseq, IDX_HEADS * IDX_DIM, t), _BF16),
            jax.ShapeDtypeStruct((nseq, 8, t), _F32),
        ]
        out_specs += [
            pl.BlockSpec((1, 1, WIDTH_A, r), lambda i, j: (i, j, 0, 0)),
            row_spec(N_HEADS_A * LANES),
            row_spec(IDX_DIM),
            pl.BlockSpec((1, IDX_HEADS * IDX_DIM, r), lambda i, j: (i, 0, j)),
            pl.BlockSpec((1, 8, r), lambda i, j: (i, 0, j)),
        ]
    else:
        out_shape += [
            jax.ShapeDtypeStruct((nseq, t, IDX_HEADS * IDX_DIM), _BF16),
            jax.ShapeDtypeStruct((nseq, t, LANES), _F32),
            jax.ShapeDtypeStruct((nseq, t, WIDTH_B), _F32),
        ]
        out_specs += [row_spec(IDX_HEADS * IDX_DIM), row_spec(LANES), row_spec(WIDTH_B)]
    return pl.pallas_call(
        functools.partial(_proj_kernel, prompt=prompt, cb=cb),
        grid=grid,
        in_specs=in_specs,
        out_specs=out_specs,
        out_shape=out_shape,
        compiler_params=pltpu.CompilerParams(
            dimension_semantics=("arbitrary", "arbitrary"), vmem_limit_bytes=VMEM_LIMIT),
        name="proj_prompt" if prompt else "proj_sample",
    )(x3, mod3, mod3, norm_g, w_in_p, cos_t, sin_t, bd, qg, kg, lng, lnb, ws, bs_b)


def _key_to_f32(key):
    bits = jnp.where(key >= 0, key, key ^ jnp.int32(0x7FFFFFFF))
    return lax.bitcast_convert_type(bits, _F32)


def _f32_to_key(x):
    bits = lax.bitcast_convert_type(x, jnp.int32)
    return jnp.where(bits >= 0, bits, bits ^ jnp.int32(0x7FFFFFFF))


def _count_keys(sct_ref, nkb, preds):
    _, tk, nq = sct_ref.shape
    g = COUNT_ACC_ROWS

    def body(j, accs):
        x = sct_ref[j]
        return tuple(acc + jnp.where(pred(x), 1.0, 0.0).reshape(tk // g, g, nq).sum(axis=0)
                     for acc, pred in zip(accs, preds))

    accs = lax.fori_loop(0, nkb, body, tuple(jnp.zeros((g, nq), _F32) for _ in preds))
    return [jnp.sum(acc, axis=0, keepdims=True) for acc in accs]


def _scores_to_bias(sct_ref, bias_ref, score_max, nkb, topk):
    nblk, tk, nq = sct_ref.shape
    kf = jnp.float32(topk)
    total = jnp.asarray(nkb * tk).astype(_F32)

    def mid_of(lo, hi):
        return lo + lax.shift_right_logical(hi - lo, 1)

    def unsettled_rows(lo, cnt_lo, mid):
        return jnp.max(jnp.where((cnt_lo != kf) & (mid != lo), 1, 0))

    def cond(st):
        return (st[5] > 0) & (st[6] < 34)

    def body(st):
        return step(step(st))

    def step(st):
        lo, hi, cnt_lo, cnt_hi, mid, _, it = st
        thr_mid = _key_to_f32(mid)
        cnt = _count_keys(sct_ref, nkb, [lambda x: x >= thr_mid])[0]
        ge = cnt >= kf
        lo = jnp.where(ge, mid, lo)
        cnt_lo = jnp.where(ge, cnt, cnt_lo)
        hi = jnp.where(ge, hi, mid)
        cnt_hi = jnp.where(ge, cnt_hi, cnt)
        mid = mid_of(lo, hi)
        return lo, hi, cnt_lo, cnt_hi, mid, unsettled_rows(lo, cnt_lo, mid), it + 1

    key_p = _f32_to_key(score_max * 0.125)
    key_t = _f32_to_key(score_max * 2.0)
    probe_p, probe_t = _key_to_f32(key_p), _key_to_f32(key_t)
    c_ge0, c_gt0, c_p, c_t = _count_keys(
        sct_ref, nkb, [lambda x: x >= 0.0, lambda x: x > 0.0, lambda x: x >= probe_p, lambda x: x >= probe_t])
    is_pos = c_gt0 >= kf
    is_zero = c_ge0 >= kf
    p_low = (c_p >= kf) & (key_p > 1)
    p_high = (c_p < kf) & (key_p > 1)
    t_high = (c_t < kf) & (key_t > key_p) & (key_t > 1)
    lo = jnp.where(is_pos, jnp.where(p_low, key_p, 1), jnp.where(is_zero, 0, KEY_NEG_INF))
    cnt_lo = jnp.where(is_pos, jnp.where(p_low, c_p, c_gt0), jnp.where(is_zero, c_ge0, total))
    hi = jnp.where(is_pos, jnp.where(p_high, key_p, jnp.where(t_high, key_t, KEY_POS_INF)),
                   jnp.where(is_zero, 1, -1))
    cnt_hi = jnp.where(is_pos, jnp.where(p_high, c_p, jnp.where(t_high, c_t, 0.0)),
                       jnp.where(is_zero, c_gt0, c_ge0))
    mid = mid_of(lo, hi)
    init = (lo, hi, cnt_lo, cnt_hi, mid, unsettled_rows(lo, cnt_lo, mid), jnp.int32(0))
    lo, hi, cnt_lo, cnt_hi = lax.while_loop(cond, body, init)[:4]

    thr = _key_to_f32(lo)
    thr = jnp.where(jnp.abs(thr) < F32_MIN_NORMAL, 0.0, thr)
    thr = jnp.maximum(thr, -F32_MAX)
    tie = (cnt_lo > kf) & (lo > KEY_NEG_INF)
    need = jnp.where(lo == 1, -1.0, jnp.where(tie, kf - cnt_hi, 1e9))

    rowi = lax.broadcasted_iota(jnp.int32, (LANES, LANES), 0)
    coli = lax.broadcasted_iota(jnp.int32, (LANES, LANES), 1)
    tri = jnp.where(rowi >= coli, 1.0, 0.0).astype(_BF16)

    def fill(j, base):
        for c in range(tk // LANES):
            x = sct_ref[j, c * LANES:(c + 1) * LANES, :]
            eq = x == thr
            rank = base + _dot(tri, jnp.where(eq, 1.0, 0.0).astype(_BF16))
            keep = (x > thr) | (eq & (rank <= need))
            bias_ref[j, :, c * LANES:(c + 1) * LANES] = jnp.where(keep, 0.0, MASK_BIAS).T
            base = rank[LANES - 1:LANES, :]
        return base

    lax.fori_loop(0, nkb, fill, jnp.zeros((1, nq), _F32))


def _dsa_prompt_kernel(q_ref, qit_ref, wit_ref, kg_ref, kt_ref, v_ref, ki_ref, o_ref, sct_ref, bias_ref, m_ref,
                       acc_ref, *, topk):
    i = pl.program_id(1)
    tq = q_ref.shape[1]
    tk = ki_ref.shape[2]
    nkb = ((i + 1) * tq + tk - 1) // tk

    wit = wit_ref[0]
    q_pos = i * tq + lax.broadcasted_iota(jnp.int32, (1, tq), 1)
    vis_lim = (q_pos // CHUNK + 1) * CHUNK
    key_in_block = lax.broadcasted_iota(jnp.int32, (tk, tq), 0)

    g = COUNT_ACC_ROWS

    def score_body(j, run_max):
        ki = ki_ref[0, j]
        acc = jnp.zeros((tk, tq), _F32)
        for h in range(IDX_HEADS):
            s = _dot(ki, qit_ref[0, h * IDX_DIM:(h + 1) * IDX_DIM, :])
            acc = acc + wit[h:h + 1, :] * jnp.maximum(s, 0.0)
        sc = jnp.where((j * tk + key_in_block) < vis_lim, acc, -jnp.inf)
        sct_ref[j] = sc
        return jnp.maximum(run_max, sc.reshape(tk // g, g, tq).max(axis=0))

    run_max = lax.fori_loop(0, nkb, score_body, jnp.full((g, tq), -jnp.inf, _F32))
    _scores_to_bias(sct_ref, bias_ref, jnp.max(run_max, axis=0, keepdims=True), nkb, topk)

    reps = tk // LANES
    q_heads = [q_ref[0, :, h * HEAD_DIM:(h + 1) * HEAD_DIM] for h in range(N_HEADS_A)]

    def logits(j, h):
        return _dot(q_heads[h], kt_ref[0, j, h * HEAD_DIM:(h + 1) * HEAD_DIM, :]) + bias_ref[j]

    k_norm = jnp.max(jnp.abs(kg_ref[...])) * (HEAD_DIM ** 0.5)
    shift_max = jnp.float32(0.0)
    for h in range(N_HEADS_A):
        qf = q_heads[h].astype(_F32)
        bound = jnp.sqrt(jnp.sum(qf * qf, axis=1, keepdims=True)) * k_norm
        m_ref[h] = jnp.broadcast_to(bound, (tq, LANES))
        shift_max = jnp.maximum(shift_max, jnp.max(bound))

    @pl.when(shift_max > SHIFT_LIMIT)
    def _():
        m_ref[...] = jnp.full(m_ref.shape, -jnp.inf, _F32)

        def max_body(j, carry):
            for h in range(N_HEADS_A):
                s = logits(j, h)
                part = s[:, :LANES]
                for c in range(1, reps):
                    part = jnp.maximum(part, s[:, c * LANES:(c + 1) * LANES])
                m_ref[h] = jnp.maximum(m_ref[h], part)
            return carry

        lax.fori_loop(0, nkb, max_body, 0)
        for h in range(N_HEADS_A):
            m_ref[h] = jnp.broadcast_to(jnp.max(m_ref[h], axis=1, keepdims=True), (tq, LANES))

    acc_ref[...] = jnp.zeros(acc_ref.shape, _F32)

    def att_body(j, carry):
        for h in range(N_HEADS_A):
            p = jnp.exp2(logits(j, h) - jnp.concatenate([m_ref[h]] * reps, axis=1))
            acc_ref[h] += _dot(p.astype(_BF16), v_ref[0, j, :, h * LANES:(h + 1) * LANES])
        return carry

    lax.fori_loop(0, nkb, att_body, 0)
    lane = lax.broadcasted_iota(jnp.int32, (1, LANES), 1)
    for hp in range(N_HEADS_A // 2):
        a0, a1 = acc_ref[2 * hp], acc_ref[2 * hp + 1]
        out0 = a0 / pltpu.roll(a0, HEAD_DIM, 1)
        out1 = pltpu.roll(a1, HEAD_DIM, 1) / a1
        o_ref[0, :, hp * LANES:(hp + 1) * LANES] = jnp.where(lane < HEAD_DIM, out0, out1).astype(o_ref.dtype)


def _dsa_prompt_call(q_bf, qit_bf, wit, kg, kt_bf, v_bf, ki_bf, *, topk):
    b, t, _ = q_bf.shape
    nt, tk = kt_bf.shape[1], kt_bf.shape[3]
    tq = min(DSA_TQ, t)
    v4 = v_bf.reshape(b, nt, tk, N_HEADS_A * LANES)
    ki4 = ki_bf.reshape(b, nt, tk, IDX_DIM)
    resident = lambda shape: pl.BlockSpec((1,) + shape, lambda bi, i: (bi, 0, 0, 0), pipeline_mode=pl.Buffered(1))
    return pl.pallas_call(
        functools.partial(_dsa_prompt_kernel, topk=topk),
        grid=(b, t // tq),
        in_specs=[
            pl.BlockSpec((1, tq, WIDTH_A), lambda bi, i: (bi, i, 0)),
            pl.BlockSpec((1, IDX_HEADS * IDX_DIM, tq), lambda bi, i: (bi, 0, i)),
            pl.BlockSpec((1, 8, tq), lambda bi, i: (bi, 0, i)),
            pl.BlockSpec((1, WIDTH_A), lambda bi, i: (0, 0)),
            resident((nt, WIDTH_A, tk)),
            resident((nt, tk, N_HEADS_A * LANES)),
            resident((nt, tk, IDX_DIM)),
        ],
        out_specs=pl.BlockSpec((1, tq, WIDTH_A), lambda bi, i: (bi, i, 0)),
        out_shape=jax.ShapeDtypeStruct((b, t, WIDTH_A), _BF16),
        scratch_shapes=[
            pltpu.VMEM((nt, tk, tq), _F32),
            pltpu.VMEM((nt, tq, tk), _F32),
            pltpu.VMEM((N_HEADS_A, tq, LANES), _F32),
            pltpu.VMEM((N_HEADS_A, tq, LANES), _F32),
        ],
        compiler_params=pltpu.CompilerParams(
            dimension_semantics=("arbitrary", "arbitrary"), vmem_limit_bytes=VMEM_LIMIT),
        name="dsa_prompt",
    )(q_bf, qit_bf, wit, kg, kt_bf, v4, ki4)


def _dsa_sample_kernel(q_ref, qi_ref, wi_ref, kn_ref, vn_ref, kin_ref, ck_ref, cv_ref, cki_ref, o_ref,
                       sct_ref, bias_ref, kall_ref, vall_ref, kiall_ref, *, topk):
    t = q_ref.shape[1]
    past = ck_ref.shape[2]
    s_pad, nq = sct_ref.shape[1], sct_ref.shape[2]
    tail = s_pad - past

    kall_ref[:past, :] = ck_ref[0, 0].astype(_BF16)
    vall_ref[:past, :] = cv_ref[0, 0].astype(_BF16)
    kiall_ref[:past, :] = cki_ref[0, 0].astype(_BF16)
    zpad = jnp.zeros((tail - t, WIDTH_A), _BF16)
    kall_ref[past:, :] = jnp.concatenate([kn_ref[0].astype(_BF16), zpad], axis=0)
    vall_ref[past:, :] = jnp.concatenate([vn_ref[0].astype(_BF16), zpad], axis=0)
    kiall_ref[past:, :] = jnp.concatenate([kin_ref[0].astype(_BF16), jnp.zeros((tail - t, IDX_DIM), _BF16)], axis=0)

    qit = jnp.concatenate([qi_ref[0].astype(_F32)] * (nq // t), axis=0).T.astype(_BF16)
    wit = jnp.concatenate([wi_ref[0]] * (nq // t), axis=0).T
    acc = jnp.zeros((s_pad, nq), _F32)
    for h in range(IDX_HEADS):
        s = _dot(kiall_ref[...], qit[h * IDX_DIM:(h + 1) * IDX_DIM, :])
        acc = acc + wit[h:h + 1, :] * jnp.maximum(s, 0.0)
    key_pos = lax.broadcasted_iota(jnp.int32, (s_pad, nq), 0)
    sc = jnp.where(key_pos < past + t, acc, -jnp.inf)
    sct_ref[0] = sc
    _scores_to_bias(sct_ref, bias_ref, jnp.max(sc, axis=0, keepdims=True), 1, topk)

    nrow = N_HEADS_A * t
    row_head = lax.broadcasted_iota(jnp.int32, (nrow, WIDTH_A), 0) // t
    col_head = lax.broadcasted_iota(jnp.int32, (nrow, WIDTH_A), 1) // HEAD_DIM
    own = row_head == col_head
    q_rep = jnp.concatenate([q_ref[0]] * N_HEADS_A, axis=0)
    q_bd = jnp.where(own, q_rep, jnp.zeros_like(q_rep))
    bias = bias_ref[0, :t, :]
    s = _dot_nt(q_bd, kall_ref[...]) + jnp.concatenate([bias] * N_HEADS_A, axis=0)
    m = jnp.max(s, axis=1, keepdims=True)
    p = jnp.exp2(s - m)
    l = jnp.sum(p, axis=1, keepdims=True)
    pv = _dot(p.astype(_BF16), vall_ref[...]) / l
    pv = jnp.where(own, pv, 0.0)
    out = pv[:t]
    for h in range(1, N_HEADS_A):
        out = out + pv[h * t:(h + 1) * t]
    o_ref[0] = out.astype(o_ref.dtype)


def _dsa_sample_call(q_bf, qi_bf, wi, k_new, v_new, ki_new, cache_k, cache_v, cache_kidx, *, topk):
    b, t, _ = q_bf.shape
    past = cache_k.shape[2]
    s_pad = past + LANES
    assert LANES % t == 0
    row = lambda c: pl.BlockSpec((1, t, c), lambda bi: (bi, 0, 0))
    cache = lambda c: pl.BlockSpec((1, 1, past, c), lambda bi: (0, bi, 0, 0))
    return pl.pallas_call(
        functools.partial(_dsa_sample_kernel, topk=topk),
        grid=(b,),
        in_specs=[row(WIDTH_A), row(IDX_HEADS * IDX_DIM), row(LANES), row(WIDTH_A), row(WIDTH_A), row(IDX_DIM),
                  cache(WIDTH_A), cache(WIDTH_A), cache(IDX_DIM)],
        out_specs=row(WIDTH_A),
        out_shape=jax.ShapeDtypeStruct((b, t, WIDTH_A), _BF16),
        scratch_shapes=[
            pltpu.VMEM((1, s_pad, LANES), _F32),
            pltpu.VMEM((1, LANES, s_pad), _F32),
            pltpu.VMEM((s_pad, WIDTH_A), _BF16),
            pltpu.VMEM((s_pad, WIDTH_A), _BF16),
            pltpu.VMEM((s_pad, IDX_DIM), _BF16),
        ],
        compiler_params=pltpu.CompilerParams(dimension_semantics=("arbitrary",), vmem_limit_bytes=VMEM_LIMIT),
        name="dsa_sample",
    )(q_bf, qi_bf, wi, k_new, v_new, ki_new, cache_k, cache_v, cache_kidx)


def _out_kernel(x_ref, at_ref, gm_ref, g1_ref, sh2_ref, sc2_ref, g2_ref, ng_ref, wo_ref, w1_ref, w2_ref, y_ref):
    nb, r, d = x_ref.shape
    m = nb * r
    at = at_ref[...].reshape(m, WIDTH_A)
    gm = gm_ref[...].reshape(m, WIDTH_B)
    y = _dot(at, wo_ref[:WIDTH_A, :]) + _dot(gm, wo_ref[WIDTH_A:, :])
    x1 = x_ref[...] + g1_ref[...] * y.reshape(nb, r, d)
    ms = jnp.mean(x1 * x1, axis=-1, keepdims=True)
    h2 = (x1 * lax.rsqrt(ms + EPS) * ng_ref[...]) * (1.0 + sc2_ref[...]) + sh2_ref[...]
    h2 = h2.reshape(m, d).astype(_BF16)
    dff = w1_ref.shape[1]
    fc = 1024
    ff = jnp.zeros((m, d), _F32)
    for c in range(dff // fc):
        a = jnp.maximum(_dot(h2, w1_ref[:, c * fc:(c + 1) * fc]), 0.0)
        ff = ff + _dot((a * a).astype(_BF16), w2_ref[c * fc:(c + 1) * fc, :])
    y_ref[...] = x1 + g2_ref[...] * ff.reshape(nb, r, d)


def _out_call(x3, attn, gm, mod3, norm_g, w_out, w_ff1, w_ff2, *, nb, r, name):
    nseq, t, d = x3.shape
    const2 = lambda i, j: (0, 0)
    row_spec = lambda c: pl.BlockSpec((nb, r, c), lambda i, j: (i, j, 0))
    mod_spec = lambda col: pl.BlockSpec((nb, 1, d), lambda i, j: (i, 0, col))
    return pl.pallas_call(
        _out_kernel,
        grid=(nseq // nb, t // r),
        in_specs=[row_spec(d), row_spec(WIDTH_A), row_spec(WIDTH_B),
                  mod_spec(2), mod_spec(3), mod_spec(4), mod_spec(5),
                  pl.BlockSpec((1, d), const2),
                  pl.BlockSpec(w_out.shape, const2, pipeline_mode=pl.Buffered(1)),
                  pl.BlockSpec(w_ff1.shape, const2, pipeline_mode=pl.Buffered(1)),
                  pl.BlockSpec(w_ff2.shape, const2, pipeline_mode=pl.Buffered(1))],
        out_specs=row_spec(d),
        out_shape=jax.ShapeDtypeStruct((nseq, t, d), _F32),
        compiler_params=pltpu.CompilerParams(
            dimension_semantics=("arbitrary", "arbitrary"), vmem_limit_bytes=VMEM_LIMIT),
        name=name,
    )(x3, attn, gm, mod3, mod3, mod3, mod3, norm_g, w_out, w_ff1, w_ff2)


def _rope_tables(pos):
    half = HEAD_DIM // 2
    inv_freq = jnp.power(jnp.float32(ROPE_THETA), -jnp.arange(half, dtype=_F32) / half)
    ang = pos.astype(_F32)[:, None] * inv_freq[None, :]
    cos, sin = jnp.cos(ang), jnp.sin(ang)
    reps = LANES // HEAD_DIM
    cos_t = jnp.tile(jnp.concatenate([cos, cos], axis=1), (1, reps))
    sin_t = jnp.tile(jnp.concatenate([-sin, sin], axis=1), (1, reps))
    return cos_t[None], sin_t[None]


def _pad_w_in(w_in):
    d = w_in.shape[0]
    a, i = WIDTH_A, IDX_HEADS * IDX_DIM
    o_ki = 3 * a + i
    o_wi = o_ki + IDX_DIM
    o_u = o_wi + IDX_HEADS
    z = lambda n: jnp.zeros((d, n), w_in.dtype)
    cols = [w_in[:, :o_ki], w_in[:, o_ki:o_wi], z(KI_PAD - IDX_DIM), w_in[:, o_wi:o_u], z(WI_PAD - IDX_HEADS),
            w_in[:, o_u:]]
    return jnp.concatenate(cols, axis=1).astype(_BF16)


def _layer(x_prompt, x_sample, c_prompt, c_sample, cache_k, cache_v, cache_kidx, w_ada, b_ada, norm1_g, norm2_g,
           w_in, q_norm_g, k_norm_g, gmlp_ln_g, gmlp_ln_b, gmlp_ws, gmlp_bs, w_out, w_ff1, w_ff2):
    bp, tp, d = x_prompt.shape
    bs_, ts, _ = x_sample.shape
    past = cache_k.shape[1]

    c_all = jnp.concatenate([c_prompt, c_sample], axis=0)
    rows = -(-c_all.shape[0] // 16) * 16
    c_all = jnp.pad(c_all, ((0, rows - c_all.shape[0]), (0, 0)))
    mod = _ada_call(c_all, w_ada, b_ada[None, :])
    mod_p = mod[:bp, None, :]
    mod_s = mod[bp:bp + bs_, None, :]

    w_in_p = _pad_w_in(w_in)
    w_out_b, w_ff1_b, w_ff2_b = w_out.astype(_BF16), w_ff1.astype(_BF16), w_ff2.astype(_BF16)
    head_of = jnp.arange(WIDTH_A) // HEAD_DIM
    bd = (head_of[:, None] == head_of[None, :]).astype(_BF16)
    qg = jnp.tile(q_norm_g, N_HEADS_A)[None, :]
    kg = jnp.tile(k_norm_g, N_HEADS_A)[None, :]
    lng, lnb = gmlp_ln_g[None, :], gmlp_ln_b[None, :]
    n1, n2 = norm1_g[None, :], norm2_g[None, :]

    rp = min(PROJ_ROWS, tp)
    cos_p, sin_p = _rope_tables(jnp.arange(tp))
    lp = min(tp, GMLP_CHUNK)
    bs_p = jnp.repeat(jnp.transpose(gmlp_bs[:, :lp]), GROUP_DIM_B, axis=1)
    (kp, vp, kip, q_bf, gm_p, kt_bf, v_bf, ki_bf, qit_bf, wit) = _proj_call(
        x_prompt, mod_p, n1, w_in_p, cos_p, sin_p, bd, qg, kg, lng, lnb, gmlp_ws[:, :lp, :lp], bs_p,
        nb=1, r=rp, prompt=True)
    attn_p = _dsa_prompt_call(q_bf, qit_bf, wit, kg, kt_bf, v_bf, ki_bf, topk=min(TOPK_MAX, tp // 4))
    yp = _out_call(x_prompt, attn_p, gm_p, mod_p, n2, w_out_b, w_ff1_b, w_ff2_b, nb=1, r=rp, name="out_prompt")

    nb = min(SAMPLE_GROUP, bs_)
    cos_s, sin_s = _rope_tables(past + jnp.arange(ts))
    ls = min(ts, GMLP_CHUNK)
    assert ls == ts
    eye = jnp.eye(nb, dtype=gmlp_ws.dtype)
    ws_s = jax.vmap(lambda w: jnp.kron(eye, w))(gmlp_ws[:, :ls, :ls])
    bs_s = jnp.tile(jnp.repeat(jnp.transpose(gmlp_bs[:, :ls]), GROUP_DIM_B, axis=1), (nb, 1))
    (ks, vs, kis, qs_bf, gm_s, qis_bf, wi_s, gvs) = _proj_call(
        x_sample, mod_s, n1, w_in_p, cos_s, sin_s, bd, qg, kg, lng, lnb, ws_s, bs_s,
        nb=nb, r=ts, prompt=False)
    attn_s = _dsa_sample_call(
        qs_bf, qis_bf, wi_s, ks, vs, kis,
        cache_k.reshape(1, bs_, past, WIDTH_A), cache_v.reshape(1, bs_, past, WIDTH_A),
        cache_kidx.reshape(1, bs_, past, IDX_DIM), topk=min(TOPK_MAX, (past + ts) // 4))
    ys = _out_call(x_sample, attn_s, gm_s, mod_s, n2, w_out_b, w_ff1_b, w_ff2_b, nb=nb, r=ts, name="out_sample")

    heads = lambda a: a.reshape(a.shape[0], a.shape[1], N_HEADS_A, HEAD_DIM)
    return yp, ys, heads(kp), heads(vp), kip, heads(ks), heads(vs), kis, gvs


def kernel(x_prompt, x_sample, c_prompt, c_sample, cache_k, cache_v, cache_kidx, w_ada, b_ada, norm1_g, norm2_g,
           w_in, q_norm_g, k_norm_g, gmlp_ln_g, gmlp_ln_b, gmlp_ws, gmlp_bs, w_out, w_ff1, w_ff2):
    depth = w_ada.shape[0]
    yp, ys = x_prompt, x_sample
    outs = [[] for _ in range(7)]
    for l in range(depth):
        res = _layer(yp, ys, c_prompt, c_sample, cache_k[l], cache_v[l], cache_kidx[l], w_ada[l], b_ada[l],
                     norm1_g[l], norm2_g[l], w_in[l], q_norm_g[l], k_norm_g[l], gmlp_ln_g[l], gmlp_ln_b[l],
                     gmlp_ws[l], gmlp_bs[l], w_out[l], w_ff1[l], w_ff2[l])
        yp, ys = res[0], res[1]
        for acc, leaf in zip(outs, res[2:]):
            acc.append(leaf)
    return (yp, ys) + tuple(jnp.stack(o) for o in outs)
```

```python
import functools

import jax
import jax.numpy as jnp
from jax import lax
from jax.experimental import pallas as pl
from jax.experimental.pallas import tpu as pltpu

N_HEADS_A = 8
HEAD_DIM = 64
WIDTH_A = N_HEADS_A * HEAD_DIM
IDX_HEADS = 4
IDX_DIM = 64
N_GROUPS_B = 4
GROUP_DIM_B = 128
WIDTH_B = N_GROUPS_B * GROUP_DIM_B
GMLP_CHUNK = 128
CHUNK = 64
TOPK_MAX = 256
ROPE_THETA = 10000.0
EPS = 1e-6

LANES = 128
KI_PAD = LANES
WI_PAD = LANES
IN_WIDTH_PADDED = 3 * WIDTH_A + IDX_HEADS * IDX_DIM + KI_PAD + WI_PAD + 2 * WIDTH_B

PROJ_ROWS = 512
DSA_TQ = 256
DSA_TK = PROJ_ROWS
SAMPLE_GROUP = 8
COUNT_ACC_ROWS = 32
VMEM_LIMIT = 56 * 1024 * 1024

MASK_BIAS = -1e30
Q_SCALE = HEAD_DIM ** -0.5 * 1.4426950408889634
SHIFT_LIMIT = 30.0
KEY_NEG_INF = -2139095041
KEY_POS_INF = 2139095040
F32_MIN_NORMAL = 1.1754944e-38
F32_MAX = 3.4028235e38

_F32 = jnp.float32
_BF16 = jnp.bfloat16


def _dot(a, b):
    return jnp.dot(a, b, preferred_element_type=_F32)


def _dot_nt(a, b):
    return lax.dot_general(a, b, (((1,), (1,)), ((), ())), preferred_element_type=_F32)


def _split_bf16(a):
    hi = a.astype(_BF16)
    lo = (a - hi.astype(_F32)).astype(_BF16)
    return hi, lo


def _ada_kernel(c_ref, w_ref, b_ref, o_ref):
    c = c_ref[...]
    s_hi, s_lo = _split_bf16(c * jax.nn.sigmoid(c))
    w_hi, w_lo = _split_bf16(w_ref[...])
    o_ref[...] = _dot(s_hi, w_hi) + _dot(s_lo, w_hi) + _dot(s_hi, w_lo) + b_ref[...]


def _ada_call(c, w_ada, b_ada):
    rows, d = c.shape
    n = w_ada.shape[1]
    tn = 1024
    return pl.pallas_call(
        _ada_kernel,
        grid=(n // tn,),
        in_specs=[
            pl.BlockSpec((rows, d), lambda j: (0, 0)),
            pl.BlockSpec((d, tn), lambda j: (0, j)),
            pl.BlockSpec((1, tn), lambda j: (0, j)),
        ],
        out_specs=pl.BlockSpec((rows, tn), lambda j: (0, j)),
        out_shape=jax.ShapeDtypeStruct((rows, n), _F32),
        compiler_params=pltpu.CompilerParams(dimension_semantics=("arbitrary",), vmem_limit_bytes=VMEM_LIMIT),
        name="ada",
    )(c, w_ada, b_ada)


def _proj_kernel(x_ref, sh_ref, sc_ref, ng_ref, w_ref, cos_ref, sin_ref, bd_ref, qg_ref, kg_ref,
                 lng_ref, lnb_ref, ws_ref, bs_ref, *out_refs, prompt, cb):
    nb, r, d = x_ref.shape
    m = nb * r

    x = x_ref[...]
    ms = jnp.mean(x * x, axis=-1, keepdims=True)
    h = (x * lax.rsqrt(ms + EPS) * ng_ref[...]) * (1.0 + sc_ref[...]) + sh_ref[...]
    h = h.reshape(m, d).astype(_BF16)

    def seg(a, b):
        return _dot(h, w_ref[:, a:b])

    o = 0
    q = seg(o, o + WIDTH_A); o += WIDTH_A
    k = seg(o, o + WIDTH_A); o += WIDTH_A
    v = seg(o, o + WIDTH_A); o += WIDTH_A
    qi = seg(o, o + IDX_HEADS * IDX_DIM); o += IDX_HEADS * IDX_DIM
    ki = seg(o, o + KI_PAD); o += KI_PAD
    wi = seg(o, o + WI_PAD); o += WI_PAD
    u = seg(o, o + WIDTH_B); o += WIDTH_B
    vg = seg(o, o + WIDTH_B)

    bd = bd_ref[...]

    def head_norm(t, g):
        hi, lo = _split_bf16(t * t)
        ss = _dot(hi, bd) + _dot(lo, bd)
        return t * lax.rsqrt(ss * (1.0 / HEAD_DIM) + EPS) * g

    cos1 = jnp.broadcast_to(cos_ref[...], (nb, r, LANES)).reshape(m, LANES)
    sin1 = jnp.broadcast_to(sin_ref[...], (nb, r, LANES)).reshape(m, LANES)

    def rope(t):
        w = t.shape[1]
        reps = w // LANES
        cosw = cos1 if reps == 1 else jnp.concatenate([cos1] * reps, axis=1)
        sinw = sin1 if reps == 1 else jnp.concatenate([sin1] * reps, axis=1)
        lane = lax.broadcasted_iota(jnp.int32, (1, w), 1)
        first_half = (lane % HEAD_DIM) < (HEAD_DIM // 2)
        rot = jnp.where(first_half, pltpu.roll(t, w - HEAD_DIM // 2, 1), pltpu.roll(t, HEAD_DIM // 2, 1))
        return t * cosw + rot * sinw

    qr = rope(head_norm(q, qg_ref[...]))
    kr = rope(head_norm(k, kg_ref[...]))
    qir = rope(qi)
    kir = rope(ki)
    wis = wi * ((IDX_DIM * IDX_HEADS) ** -0.5)

    ug = jax.nn.gelu(u)
    vgg = jax.nn.gelu(vg)
    mu = jnp.mean(vgg, axis=-1, keepdims=True)
    xc = vgg - mu
    var = jnp.mean(xc * xc, axis=-1, keepdims=True)
    vn = xc * lax.rsqrt(var + EPS) * lng_ref[...] + lnb_ref[...]
    vnb = vn.astype(_BF16)
    rowi = lax.broadcasted_iota(jnp.int32, (cb, cb), 0)
    coli = lax.broadcasted_iota(jnp.int32, (cb, cb), 1)
    ws_m = [jnp.where(rowi >= coli, ws_ref[g], 0.0).astype(_BF16) for g in range(N_GROUPS_B)]
    gm_rows = []
    for c in range(m // cb):
        pieces = []
        for g in range(N_GROUPS_B):
            lanes = slice(g * GROUP_DIM_B, (g + 1) * GROUP_DIM_B)
            mixed = _dot(ws_m[g], vnb[c * cb:(c + 1) * cb, lanes]) + bs_ref[:, lanes]
            pieces.append(ug[c * cb:(c + 1) * cb, lanes] * mixed)
        gm_rows.append(jnp.concatenate(pieces, axis=1))
    gm = gm_rows[0] if len(gm_rows) == 1 else jnp.concatenate(gm_rows, axis=0)

    def put(ref, val):
        ref[...] = val.astype(ref.dtype).reshape(ref.shape)

    if prompt:
        k_out, v_out, ki_out, q_bf, gm_out, kt_bf, v_bf, ki_bf, qit_bf, wit_out = out_refs
        put(kt_bf, kr.T)
        put(ki_bf, kir[:, :IDX_DIM])
        put(qit_bf, qir.T)
        put(wit_out, wis.T[:8, :])
        lane = lax.broadcasted_iota(jnp.int32, (1, LANES), 1)
        pieces = []
        for hd in range(N_HEADS_A):
            src = v[:, (hd // 2) * LANES:(hd // 2 + 1) * LANES]
            if hd % 2:
                src = pltpu.roll(src, HEAD_DIM, 1)
            pieces.append(jnp.where(lane < HEAD_DIM, src, 1.0))
        put(v_bf, jnp.concatenate(pieces, axis=1))
    else:
        k_out, v_out, ki_out, q_bf, gm_out, qi_bf, wi_out, vn_out = out_refs
        put(qi_bf, qir)
        put(wi_out, wis)
        put(vn_out, vn)
    put(k_out, kr)
    put(v_out, v)
    put(ki_out, kir[:, :IDX_DIM])
    put(q_bf, qr * Q_SCALE)
    put(gm_out, gm)


def _proj_call(x3, mod3, norm_g, w_in_p, cos_t, sin_t, bd, qg, kg, lng, lnb, ws, bs_b, *, nb, r, prompt):
    nseq, t, d = x3.shape
    m = nb * r
    cb = ws.shape[1]
    grid = (nseq // nb, t // r)
    const2 = lambda i, j: (0, 0)
    const3 = lambda i, j: (0, 0, 0)
    row_spec = lambda c: pl.BlockSpec((nb, r, c), lambda i, j: (i, j, 0))
    in_specs = [
        row_spec(d),
        pl.BlockSpec((nb, 1, d), lambda i, j: (i, 0, 0)),
        pl.BlockSpec((nb, 1, d), lambda i, j: (i, 0, 1)),
        pl.BlockSpec((1, d), const2),
        pl.BlockSpec(w_in_p.shape, const2, pipeline_mode=pl.Buffered(1)),
        pl.BlockSpec((1, r, LANES), lambda i, j: (0, j, 0)),
        pl.BlockSpec((1, r, LANES), lambda i, j: (0, j, 0)),
        pl.BlockSpec(bd.shape, const2),
        pl.BlockSpec((1, WIDTH_A), const2),
        pl.BlockSpec((1, WIDTH_A), const2),
        pl.BlockSpec((1, WIDTH_B), const2),
        pl.BlockSpec((1, WIDTH_B), const2),
        pl.BlockSpec(ws.shape, const3),
        pl.BlockSpec(bs_b.shape, const2),
    ]
    out_shape = [
        jax.ShapeDtypeStruct((nseq, t, WIDTH_A), _F32),
        jax.ShapeDtypeStruct((nseq, t, WIDTH_A), _F32),
        jax.ShapeDtypeStruct((nseq, t, IDX_DIM), _F32),
        jax.ShapeDtypeStruct((nseq, t, WIDTH_A), _BF16),
        jax.ShapeDtypeStruct((nseq, t, WIDTH_B), _BF16),
    ]
    out_specs = [row_spec(WIDTH_A), row_spec(WIDTH_A), row_spec(IDX_DIM), row_spec(WIDTH_A), row_spec(WIDTH_B)]
    if prompt:
        assert nb == 1
        nt = t // r
        out_shape += [
            jax.ShapeDtypeStruct((nseq, nt, WIDTH_A, r), _BF16),
            jax.ShapeDtypeStruct((nseq, t, N_HEADS_A * LANES), _BF16),
            jax.ShapeDtypeStruct((nseq, t, IDX_DIM), _BF16),
            jax.ShapeDtypeStruct((nseq, IDX_HEADS * IDX_DIM, t), _BF16),
            jax.ShapeDtypeStruct((nseq, 8, t), _F32),
        ]
        out_specs += [
            pl.BlockSpec((1, 1, WIDTH_A, r), lambda i, j: (i, j, 0, 0)),
            row_spec(N_HEADS_A * LANES),
            row_spec(IDX_DIM),
            pl.BlockSpec((1, IDX_HEADS * IDX_DIM, r), lambda i, j: (i, 0, j)),
            pl.BlockSpec((1, 8, r), lambda i, j: (i, 0, j)),
        ]
    else:
        out_shape += [
            jax.ShapeDtypeStruct((nseq, t, IDX_HEADS * IDX_DIM), _BF16),
            jax.ShapeDtypeStruct((nseq, t, LANES), _F32),
            jax.ShapeDtypeStruct((nseq, t, WIDTH_B), _F32),
        ]
        out_specs += [row_spec(IDX_HEADS * IDX_DIM), row_spec(LANES), row_spec(WIDTH_B)]
    return pl.pallas_call(
        functools.partial(_proj_kernel, prompt=prompt, cb=cb),
        grid=grid,
        in_specs=in_specs,
        out_specs=out_specs,
        out_shape=out_shape,
        compiler_params=pltpu.CompilerParams(
            dimension_semantics=("arbitrary", "arbitrary"), vmem_limit_bytes=VMEM_LIMIT),
        name="proj_prompt" if prompt else "proj_sample",
    )(x3, mod3, mod3, norm_g, w_in_p, cos_t, sin_t, bd, qg, kg, lng, lnb, ws, bs_b)


def _fori_by_pairs(n, body, init):
    if isinstance(n, int):
        return lax.fori_loop(0, n, body, init)
    carry = lax.fori_loop(0, n // 2, lambda p, c: body(2 * p + 1, body(2 * p, c)), init)
    return lax.cond(n % 2 == 1, lambda c: body(n - 1, c), lambda c: c, carry)


def _key_to_f32(key):
    bits = jnp.where(key >= 0, key, key ^ jnp.int32(0x7FFFFFFF))
    return lax.bitcast_convert_type(bits, _F32)


def _f32_to_key(x):
    bits = lax.bitcast_convert_type(x, jnp.int32)
    return jnp.where(bits >= 0, bits, bits ^ jnp.int32(0x7FFFFFFF))


def _count_keys(sct_ref, nkb, preds):
    _, tk, nq = sct_ref.shape
    g = COUNT_ACC_ROWS

    def body(j, accs):
        x = sct_ref[j]
        return tuple(acc + jnp.where(pred(x), 1.0, 0.0).reshape(tk // g, g, nq).sum(axis=0)
                     for acc, pred in zip(accs, preds))

    accs = _fori_by_pairs(nkb, body, tuple(jnp.zeros((g, nq), _F32) for _ in preds))
    return [jnp.sum(acc, axis=0, keepdims=True) for acc in accs]


def _scores_to_bias(sct_ref, bias_ref, score_max, nkb, topk):
    nblk, tk, nq = sct_ref.shape
    kf = jnp.float32(topk)
    total = jnp.asarray(nkb * tk).astype(_F32)

    def mid_of(lo, hi):
        return lo + lax.shift_right_logical(hi - lo, 1)

    def unsettled_rows(lo, cnt_lo, mid):
        return jnp.max(jnp.where((cnt_lo != kf) & (mid != lo), 1, 0))

    def cond(st):
        return (st[5] > 0) & (st[6] < 34)

    def body(st):
        return step(step(st))

    def step(st):
        lo, hi, cnt_lo, cnt_hi, mid, _, it = st
        thr_mid = _key_to_f32(mid)
        cnt = _count_keys(sct_ref, nkb, [lambda x: x >= thr_mid])[0]
        ge = cnt >= kf
        lo = jnp.where(ge, mid, lo)
        cnt_lo = jnp.where(ge, cnt, cnt_lo)
        hi = jnp.where(ge, hi, mid)
        cnt_hi = jnp.where(ge, cnt_hi, cnt)
        mid = mid_of(lo, hi)
        return lo, hi, cnt_lo, cnt_hi, mid, unsettled_rows(lo, cnt_lo, mid), it + 1

    key_p = _f32_to_key(score_max * 0.125)
    key_t = _f32_to_key(score_max * 2.0)
    probe_p, probe_t = _key_to_f32(key_p), _key_to_f32(key_t)
    c_ge0, c_gt0, c_p, c_t = _count_keys(
        sct_ref, nkb, [lambda x: x >= 0.0, lambda x: x > 0.0, lambda x: x >= probe_p, lambda x: x >= probe_t])
    is_pos = c_gt0 >= kf
    is_zero = c_ge0 >= kf
    p_low = (c_p >= kf) & (key_p > 1)
    p_high = (c_p < kf) & (key_p > 1)
    t_high = (c_t < kf) & (key_t > key_p) & (key_t > 1)
    lo = jnp.where(is_pos, jnp.where(p_low, key_p, 1), jnp.where(is_zero, 0, KEY_NEG_INF))
    cnt_lo = jnp.where(is_pos, jnp.where(p_low, c_p, c_gt0), jnp.where(is_zero, c_ge0, total))
    hi = jnp.where(is_pos, jnp.where(p_high, key_p, jnp.where(t_high, key_t, KEY_POS_INF)),
                   jnp.where(is_zero, 1, -1))
    cnt_hi = jnp.where(is_pos, jnp.where(p_high, c_p, jnp.where(t_high, c_t, 0.0)),
                       jnp.where(is_zero, c_gt0, c_ge0))
    mid = mid_of(lo, hi)
    init = (lo, hi, cnt_lo, cnt_hi, mid, unsettled_rows(lo, cnt_lo, mid), jnp.int32(0))
    lo, hi, cnt_lo, cnt_hi = lax.while_loop(cond, body, init)[:4]

    thr = _key_to_f32(lo)
    thr = jnp.where(jnp.abs(thr) < F32_MIN_NORMAL, 0.0, thr)
    thr = jnp.maximum(thr, -F32_MAX)
    tie = (cnt_lo > kf) & (lo > KEY_NEG_INF)
    need = jnp.where(lo == 1, -1.0, jnp.where(tie, kf - cnt_hi, 1e9))

    rowi = lax.broadcasted_iota(jnp.int32, (LANES, LANES), 0)
    coli = lax.broadcasted_iota(jnp.int32, (LANES, LANES), 1)
    tri = jnp.where(rowi >= coli, 1.0, 0.0).astype(_BF16)

    def fill(j, base):
        for c in range(tk // LANES):
            x = sct_ref[j, c * LANES:(c + 1) * LANES, :]
            eq = x == thr
            rank = base + _dot(tri, jnp.where(eq, 1.0, 0.0).astype(_BF16))
            keep = (x > thr) | (eq & (rank <= need))
            bias_ref[j, :, c * LANES:(c + 1) * LANES] = jnp.where(keep, 0.0, MASK_BIAS).T
            base = rank[LANES - 1:LANES, :]
        return base

    _fori_by_pairs(nkb, fill, jnp.zeros((1, nq), _F32))


def _dsa_prompt_kernel(q_ref, qit_ref, wit_ref, kg_ref, kt_ref, v_ref, ki_ref, o_ref, sct_ref, bias_ref, m_ref,
                       acc_ref, *, topk):
    i = pl.program_id(1)
    tq = q_ref.shape[1]
    tk = ki_ref.shape[2]
    nkb = ((i + 1) * tq + tk - 1) // tk

    wit = wit_ref[0]
    q_pos = i * tq + lax.broadcasted_iota(jnp.int32, (1, tq), 1)
    vis_lim = (q_pos // CHUNK + 1) * CHUNK
    key_in_block = lax.broadcasted_iota(jnp.int32, (tk, tq), 0)

    g = COUNT_ACC_ROWS

    def score_body(j, run_max):
        ki = ki_ref[0, j]
        acc = jnp.zeros((tk, tq), _F32)
        for h in range(IDX_HEADS):
            s = _dot(ki, qit_ref[0, h * IDX_DIM:(h + 1) * IDX_DIM, :])
            acc = acc + wit[h:h + 1, :] * jnp.maximum(s, 0.0)
        sc = jnp.where((j * tk + key_in_block) < vis_lim, acc, -jnp.inf)
        sct_ref[j] = sc
        return jnp.maximum(run_max, sc.reshape(tk // g, g, tq).max(axis=0))

    run_max = _fori_by_pairs(nkb, score_body, jnp.full((g, tq), -jnp.inf, _F32))
    _scores_to_bias(sct_ref, bias_ref, jnp.max(run_max, axis=0, keepdims=True), nkb, topk)

    reps = tk // LANES
    q_heads = [q_ref[0, :, h * HEAD_DIM:(h + 1) * HEAD_DIM] for h in range(N_HEADS_A)]

    def logits(j, h):
        return _dot(q_heads[h], kt_ref[0, j, h * HEAD_DIM:(h + 1) * HEAD_DIM, :]) + bias_ref[j]

    k_norm = jnp.max(jnp.abs(kg_ref[...])) * (HEAD_DIM ** 0.5)
    shift_max = jnp.float32(0.0)
    for h in range(N_HEADS_A):
        qf = q_heads[h].astype(_F32)
        bound = jnp.sqrt(jnp.sum(qf * qf, axis=1, keepdims=True)) * k_norm
        m_ref[h] = jnp.broadcast_to(bound, (tq, LANES))
        shift_max = jnp.maximum(shift_max, jnp.max(bound))

    @pl.when(shift_max > SHIFT_LIMIT)
    def _():
        m_ref[...] = jnp.full(m_ref.shape, -jnp.inf, _F32)

        def max_body(j, carry):
            for h in range(N_HEADS_A):
                s = logits(j, h)
                part = s[:, :LANES]
                for c in range(1, reps):
                    part = jnp.maximum(part, s[:, c * LANES:(c + 1) * LANES])
                m_ref[h] = jnp.maximum(m_ref[h], part)
            return carry

        lax.fori_loop(0, nkb, max_body, 0)
        for h in range(N_HEADS_A):
            m_ref[h] = jnp.broadcast_to(jnp.max(m_ref[h], axis=1, keepdims=True), (tq, LANES))

    acc_ref[...] = jnp.zeros(acc_ref.shape, _F32)

    def att_body(j, carry):
        for h in range(N_HEADS_A):
            p = jnp.exp2(logits(j, h) - jnp.concatenate([m_ref[h]] * reps, axis=1))
            acc_ref[h] += _dot(p.astype(_BF16), v_ref[0, j, :, h * LANES:(h + 1) * LANES])
        return carry

    _fori_by_pairs(nkb, att_body, 0)
    lane = lax.broadcasted_iota(jnp.int32, (1, LANES), 1)
    for hp in range(N_HEADS_A // 2):
        a0, a1 = acc_ref[2 * hp], acc_ref[2 * hp + 1]
        out0 = a0 / pltpu.roll(a0, HEAD_DIM, 1)
        out1 = pltpu.roll(a1, HEAD_DIM, 1) / a1
        o_ref[0, :, hp * LANES:(hp + 1) * LANES] = jnp.where(lane < HEAD_DIM, out0, out1).astype(o_ref.dtype)


def _dsa_prompt_call(q_bf, qit_bf, wit, kg, kt_bf, v_bf, ki_bf, *, topk):
    b, t, _ = q_bf.shape
    nt, tk = kt_bf.shape[1], kt_bf.shape[3]
    tq = min(DSA_TQ, t)
    v4 = v_bf.reshape(b, nt, tk, N_HEADS_A * LANES)
    ki4 = ki_bf.reshape(b, nt, tk, IDX_DIM)
    resident = lambda shape: pl.BlockSpec((1,) + shape, lambda bi, i: (bi, 0, 0, 0), pipeline_mode=pl.Buffered(1))
    return pl.pallas_call(
        functools.partial(_dsa_prompt_kernel, topk=topk),
        grid=(b, t // tq),
        in_specs=[
            pl.BlockSpec((1, tq, WIDTH_A), lambda bi, i: (bi, i, 0)),
            pl.BlockSpec((1, IDX_HEADS * IDX_DIM, tq), lambda bi, i: (bi, 0, i)),
            pl.BlockSpec((1, 8, tq), lambda bi, i: (bi, 0, i)),
            pl.BlockSpec((1, WIDTH_A), lambda bi, i: (0, 0)),
            resident((nt, WIDTH_A, tk)),
            resident((nt, tk, N_HEADS_A * LANES)),
            resident((nt, tk, IDX_DIM)),
        ],
        out_specs=pl.BlockSpec((1, tq, WIDTH_A), lambda bi, i: (bi, i, 0)),
        out_shape=jax.ShapeDtypeStruct((b, t, WIDTH_A), _BF16),
        scratch_shapes=[
            pltpu.VMEM((nt, tk, tq), _F32),
            pltpu.VMEM((nt, tq, tk), _F32),
            pltpu.VMEM((N_HEADS_A, tq, LANES), _F32),
            pltpu.VMEM((N_HEADS_A, tq, LANES), _F32),
        ],
        compiler_params=pltpu.CompilerParams(
            dimension_semantics=("arbitrary", "arbitrary"), vmem_limit_bytes=VMEM_LIMIT),
        name="dsa_prompt",
    )(q_bf, qit_bf, wit, kg, kt_bf, v4, ki4)


def _dsa_sample_kernel(q_ref, qi_ref, wi_ref, kn_ref, vn_ref, kin_ref, ck_ref, cv_ref, cki_ref, o_ref,
                       sct_ref, bias_ref, kall_ref, vall_ref, kiall_ref, *, topk):
    t = q_ref.shape[1]
    past = cki_ref.shape[2]
    s_pad, nq = sct_ref.shape[1], sct_ref.shape[2]
    tail = s_pad - past

    for src_ref, dst_ref in ((ck_ref, kall_ref), (cv_ref, vall_ref)):
        for hp in range(N_HEADS_A // 2):
            pair = [src_ref[0, 0, pl.ds(2 * hp + e, past, stride=N_HEADS_A), :] for e in range(2)]
            dst_ref[:past, hp * LANES:(hp + 1) * LANES] = jnp.concatenate(pair, axis=1).astype(_BF16)
    kiall_ref[:past, :] = cki_ref[0, 0].astype(_BF16)
    zpad = jnp.zeros((tail - t, WIDTH_A), _BF16)
    kall_ref[past:, :] = jnp.concatenate([kn_ref[0].astype(_BF16), zpad], axis=0)
    vall_ref[past:, :] = jnp.concatenate([vn_ref[0].astype(_BF16), zpad], axis=0)
    kiall_ref[past:, :] = jnp.concatenate([kin_ref[0].astype(_BF16), jnp.zeros((tail - t, IDX_DIM), _BF16)], axis=0)

    qit = jnp.concatenate([qi_ref[0].astype(_F32)] * (nq // t), axis=0).T.astype(_BF16)
    wit = jnp.concatenate([wi_ref[0]] * (nq // t), axis=0).T
    acc = jnp.zeros((s_pad, nq), _F32)
    for h in range(IDX_HEADS):
        s = _dot(kiall_ref[...], qit[h * IDX_DIM:(h + 1) * IDX_DIM, :])
        acc = acc + wit[h:h + 1, :] * jnp.maximum(s, 0.0)
    key_pos = lax.broadcasted_iota(jnp.int32, (s_pad, nq), 0)
    sc = jnp.where(key_pos < past + t, acc, -jnp.inf)
    sct_ref[0] = sc
    _scores_to_bias(sct_ref, bias_ref, jnp.max(sc, axis=0, keepdims=True), 1, topk)

    nrow = N_HEADS_A * t
    row_head = lax.broadcasted_iota(jnp.int32, (nrow, WIDTH_A), 0) // t
    col_head = lax.broadcasted_iota(jnp.int32, (nrow, WIDTH_A), 1) // HEAD_DIM
    own = row_head == col_head
    q_rep = jnp.concatenate([q_ref[0]] * N_HEADS_A, axis=0)
    q_bd = jnp.where(own, q_rep, jnp.zeros_like(q_rep))
    bias = bias_ref[0, :t, :]
    s = _dot_nt(q_bd, kall_ref[...]) + jnp.concatenate([bias] * N_HEADS_A, axis=0)
    m = jnp.max(s, axis=1, keepdims=True)
    p = jnp.exp2(s - m)
    l = jnp.sum(p, axis=1, keepdims=True)
    pv = _dot(p.astype(_BF16), vall_ref[...]) / l
    pv = jnp.where(own, pv, 0.0)
    out = pv[:t]
    for h in range(1, N_HEADS_A):
        out = out + pv[h * t:(h + 1) * t]
    o_ref[0] = out.astype(o_ref.dtype)


def _dsa_sample_call(q_bf, qi_bf, wi, k_new, v_new, ki_new, cache_k, cache_v, cache_kidx, *, topk):
    b, t, _ = q_bf.shape
    past = cache_kidx.shape[2]
    s_pad = past + LANES
    assert LANES % t == 0
    row = lambda c: pl.BlockSpec((1, t, c), lambda bi: (bi, 0, 0))
    cache = lambda rows, c: pl.BlockSpec((1, 1, rows, c), lambda bi: (0, bi, 0, 0))
    return pl.pallas_call(
        functools.partial(_dsa_sample_kernel, topk=topk),
        grid=(b,),
        in_specs=[row(WIDTH_A), row(IDX_HEADS * IDX_DIM), row(LANES), row(WIDTH_A), row(WIDTH_A), row(IDX_DIM),
                  cache(past * N_HEADS_A, HEAD_DIM), cache(past * N_HEADS_A, HEAD_DIM), cache(past, IDX_DIM)],
        out_specs=row(WIDTH_A),
        out_shape=jax.ShapeDtypeStruct((b, t, WIDTH_A), _BF16),
        scratch_shapes=[
            pltpu.VMEM((1, s_pad, LANES), _F32),
            pltpu.VMEM((1, LANES, s_pad), _F32),
            pltpu.VMEM((s_pad, WIDTH_A), _BF16),
            pltpu.VMEM((s_pad, WIDTH_A), _BF16),
            pltpu.VMEM((s_pad, IDX_DIM), _BF16),
        ],
        compiler_params=pltpu.CompilerParams(dimension_semantics=("arbitrary",), vmem_limit_bytes=VMEM_LIMIT),
        name="dsa_sample",
    )(q_bf, qi_bf, wi, k_new, v_new, ki_new, cache_k, cache_v, cache_kidx)


def _out_kernel(x_ref, at_ref, gm_ref, g1_ref, sh2_ref, sc2_ref, g2_ref, ng_ref, wo_ref, w1_ref, w2_ref, y_ref):
    nb, r, d = x_ref.shape
    m = nb * r
    at = at_ref[...].reshape(m, WIDTH_A)
    gm = gm_ref[...].reshape(m, WIDTH_B)
    y = _dot(at, wo_ref[:WIDTH_A, :]) + _dot(gm, wo_ref[WIDTH_A:, :])
    x1 = x_ref[...] + g1_ref[...] * y.reshape(nb, r, d)
    ms = jnp.mean(x1 * x1, axis=-1, keepdims=True)
    h2 = (x1 * lax.rsqrt(ms + EPS) * ng_ref[...]) * (1.0 + sc2_ref[...]) + sh2_ref[...]
    h2 = h2.reshape(m, d).astype(_BF16)
    dff = w1_ref.shape[1]
    fc = 1024
    ff = jnp.zeros((m, d), _F32)
    for c in range(dff // fc):
        a = jnp.maximum(_dot(h2, w1_ref[:, c * fc:(c + 1) * fc]), 0.0)
        ff = ff + _dot((a * a).astype(_BF16), w2_ref[c * fc:(c + 1) * fc, :])
    y_ref[...] = x1 + g2_ref[...] * ff.reshape(nb, r, d)


def _out_call(x3, attn, gm, mod3, norm_g, w_out, w_ff1, w_ff2, *, nb, r, name):
    nseq, t, d = x3.shape
    const2 = lambda i, j: (0, 0)
    row_spec = lambda c: pl.BlockSpec((nb, r, c), lambda i, j: (i, j, 0))
    mod_spec = lambda col: pl.BlockSpec((nb, 1, d), lambda i, j: (i, 0, col))
    return pl.pallas_call(
        _out_kernel,
        grid=(nseq // nb, t // r),
        in_specs=[row_spec(d), row_spec(WIDTH_A), row_spec(WIDTH_B),
                  mod_spec(2), mod_spec(3), mod_spec(4), mod_spec(5),
                  pl.BlockSpec((1, d), const2),
                  pl.BlockSpec(w_out.shape, const2, pipeline_mode=pl.Buffered(1)),
                  pl.BlockSpec(w_ff1.shape, const2, pipeline_mode=pl.Buffered(1)),
                  pl.BlockSpec(w_ff2.shape, const2, pipeline_mode=pl.Buffered(1))],
        out_specs=row_spec(d),
        out_shape=jax.ShapeDtypeStruct((nseq, t, d), _F32),
        compiler_params=pltpu.CompilerParams(
            dimension_semantics=("arbitrary", "arbitrary"), vmem_limit_bytes=VMEM_LIMIT),
        name=name,
    )(x3, attn, gm, mod3, mod3, mod3, mod3, norm_g, w_out, w_ff1, w_ff2)


def _rope_tables(pos):
    half = HEAD_DIM // 2
    inv_freq = jnp.power(jnp.float32(ROPE_THETA), -jnp.arange(half, dtype=_F32) / half)
    ang = pos.astype(_F32)[:, None] * inv_freq[None, :]
    cos, sin = jnp.cos(ang), jnp.sin(ang)
    reps = LANES // HEAD_DIM
    cos_t = jnp.tile(jnp.concatenate([cos, cos], axis=1), (1, reps))
    sin_t = jnp.tile(jnp.concatenate([-sin, sin], axis=1), (1, reps))
    return cos_t[None], sin_t[None]


def _pad_w_in(w_in):
    d = w_in.shape[0]
    a, i = WIDTH_A, IDX_HEADS * IDX_DIM
    o_ki = 3 * a + i
    o_wi = o_ki + IDX_DIM
    o_u = o_wi + IDX_HEADS
    z = lambda n: jnp.zeros((d, n), w_in.dtype)
    cols = [w_in[:, :o_ki], w_in[:, o_ki:o_wi], z(KI_PAD - IDX_DIM), w_in[:, o_wi:o_u], z(WI_PAD - IDX_HEADS),
            w_in[:, o_u:]]
    return jnp.concatenate(cols, axis=1).astype(_BF16)


def _layer(x_prompt, x_sample, c_prompt, c_sample, cache_k, cache_v, cache_kidx, w_ada, b_ada, norm1_g, norm2_g,
           w_in, q_norm_g, k_norm_g, gmlp_ln_g, gmlp_ln_b, gmlp_ws, gmlp_bs, w_out, w_ff1, w_ff2):
    bp, tp, d = x_prompt.shape
    bs_, ts, _ = x_sample.shape
    past = cache_k.shape[1]

    c_all = jnp.concatenate([c_prompt, c_sample], axis=0)
    rows = -(-c_all.shape[0] // 16) * 16
    c_all = jnp.pad(c_all, ((0, rows - c_all.shape[0]), (0, 0)))
    mod = _ada_call(c_all, w_ada, b_ada[None, :])
    mod_p = mod[:bp, None, :]
    mod_s = mod[bp:bp + bs_, None, :]

    w_in_p = _pad_w_in(w_in)
    w_out_b, w_ff1_b, w_ff2_b = w_out.astype(_BF16), w_ff1.astype(_BF16), w_ff2.astype(_BF16)
    head_of = jnp.arange(WIDTH_A) // HEAD_DIM
    bd = (head_of[:, None] == head_of[None, :]).astype(_BF16)
    qg = jnp.tile(q_norm_g, N_HEADS_A)[None, :]
    kg = jnp.tile(k_norm_g, N_HEADS_A)[None, :]
    lng, lnb = gmlp_ln_g[None, :], gmlp_ln_b[None, :]
    n1, n2 = norm1_g[None, :], norm2_g[None, :]

    rp = min(PROJ_ROWS, tp)
    cos_p, sin_p = _rope_tables(jnp.arange(tp))
    lp = min(tp, GMLP_CHUNK)
    bs_p = jnp.repeat(jnp.transpose(gmlp_bs[:, :lp]), GROUP_DIM_B, axis=1)
    (kp, vp, kip, q_bf, gm_p, kt_bf, v_bf, ki_bf, qit_bf, wit) = _proj_call(
        x_prompt, mod_p, n1, w_in_p, cos_p, sin_p, bd, qg, kg, lng, lnb, gmlp_ws[:, :lp, :lp], bs_p,
        nb=1, r=rp, prompt=True)
    attn_p = _dsa_prompt_call(q_bf, qit_bf, wit, kg, kt_bf, v_bf, ki_bf, topk=min(TOPK_MAX, tp // 4))
    yp = _out_call(x_prompt, attn_p, gm_p, mod_p, n2, w_out_b, w_ff1_b, w_ff2_b, nb=1, r=rp, name="out_prompt")

    nb = min(SAMPLE_GROUP, bs_)
    cos_s, sin_s = _rope_tables(past + jnp.arange(ts))
    ls = min(ts, GMLP_CHUNK)
    assert ls == ts
    eye = jnp.eye(nb, dtype=gmlp_ws.dtype)
    ws_s = jax.vmap(lambda w: jnp.kron(eye, w))(gmlp_ws[:, :ls, :ls])
    bs_s = jnp.tile(jnp.repeat(jnp.transpose(gmlp_bs[:, :ls]), GROUP_DIM_B, axis=1), (nb, 1))
    (ks, vs, kis, qs_bf, gm_s, qis_bf, wi_s, gvs) = _proj_call(
        x_sample, mod_s, n1, w_in_p, cos_s, sin_s, bd, qg, kg, lng, lnb, ws_s, bs_s,
        nb=nb, r=ts, prompt=False)
    attn_s = _dsa_sample_call(
        qs_bf, qis_bf, wi_s, ks, vs, kis,
        cache_k.reshape(1, bs_, past * N_HEADS_A, HEAD_DIM), cache_v.reshape(1, bs_, past * N_HEADS_A, HEAD_DIM),
        cache_kidx.reshape(1, bs_, past, IDX_DIM), topk=min(TOPK_MAX, (past + ts) // 4))
    ys = _out_call(x_sample, attn_s, gm_s, mod_s, n2, w_out_b, w_ff1_b, w_ff2_b, nb=nb, r=ts, name="out_sample")

    heads = lambda a: a.reshape(a.shape[0], a.shape[1], N_HEADS_A, HEAD_DIM)
    return yp, ys, heads(kp), heads(vp), kip, heads(ks), heads(vs), kis, gvs


def kernel(x_prompt, x_sample, c_prompt, c_sample, cache_k, cache_v, cache_kidx, w_ada, b_ada, norm1_g, norm2_g,
           w_in, q_norm_g, k_norm_g, gmlp_ln_g, gmlp_ln_b, gmlp_ws, gmlp_bs, w_out, w_ff1, w_ff2):
    depth = w_ada.shape[0]
    yp, ys = x_prompt, x_sample
    outs = [[] for _ in range(7)]
    for l in range(depth):
        res = _layer(yp, ys, c_prompt, c_sample, cache_k[l], cache_v[l], cache_kidx[l], w_ada[l], b_ada[l],
                     norm1_g[l], norm2_g[l], w_in[l], q_norm_g[l], k_norm_g[l], gmlp_ln_g[l], gmlp_ln_b[l],
                     gmlp_ws[l], gmlp_bs[l], w_out[l], w_ff1[l], w_ff2[l])
        yp, ys = res[0], res[1]
        for acc, leaf in zip(outs, res[2:]):
            acc.append(leaf)
    return (yp, ys) + tuple(jnp.stack(o) for o in outs)
```

```python
import functools

import jax
import jax.numpy as jnp
from jax import lax
from jax.experimental import pallas as pl
from jax.experimental.pallas import tpu as pltpu

N_HEADS_A = 8
HEAD_DIM = 64
WIDTH_A = N_HEADS_A * HEAD_DIM
IDX_HEADS = 4
IDX_DIM = 64
N_GROUPS_B = 4
GROUP_DIM_B = 128
WIDTH_B = N_GROUPS_B * GROUP_DIM_B
GMLP_CHUNK = 128
CHUNK = 64
TOPK_MAX = 256
ROPE_THETA = 10000.0
EPS = 1e-6

LANES = 128
KI_PAD = LANES
WI_PAD = LANES
IN_WIDTH_PADDED = 3 * WIDTH_A + IDX_HEADS * IDX_DIM + KI_PAD + WI_PAD + 2 * WIDTH_B

PROJ_ROWS = 512
DSA_TQ = 256
DSA_TK = PROJ_ROWS
SAMPLE_GROUP = 8
COUNT_ACC_ROWS = 32
VMEM_LIMIT = 56 * 1024 * 1024

MASK_BIAS = -1e30
Q_SCALE = HEAD_DIM ** -0.5 * 1.4426950408889634
SHIFT_LIMIT = 30.0
KEY_NEG_INF = -2139095041
KEY_POS_INF = 2139095040
F32_MIN_NORMAL = 1.1754944e-38
F32_MAX = 3.4028235e38

_F32 = jnp.float32
_BF16 = jnp.bfloat16


def _dot(a, b):
    return jnp.dot(a, b, preferred_element_type=_F32)


def _dot_nt(a, b):
    return lax.dot_general(a, b, (((1,), (1,)), ((), ())), preferred_element_type=_F32)


def _split_bf16(a):
    hi = a.astype(_BF16)
    lo = (a - hi.astype(_F32)).astype(_BF16)
    return hi, lo


def _ada_kernel(c_ref, w_ref, b_ref, o_ref):
    c = c_ref[...]
    s_hi, s_lo = _split_bf16(c * jax.nn.sigmoid(c))
    w_hi, w_lo = _split_bf16(w_ref[...])
    o_ref[...] = _dot(s_hi, w_hi) + _dot(s_lo, w_hi) + _dot(s_hi, w_lo) + b_ref[...]


def _ada_call(c, w_ada, b_ada):
    rows, d = c.shape
    n = w_ada.shape[1]
    tn = 1024
    return pl.pallas_call(
        _ada_kernel,
        grid=(n // tn,),
        in_specs=[
            pl.BlockSpec((rows, d), lambda j: (0, 0)),
            pl.BlockSpec((d, tn), lambda j: (0, j)),
            pl.BlockSpec((1, tn), lambda j: (0, j)),
        ],
        out_specs=pl.BlockSpec((rows, tn), lambda j: (0, j)),
        out_shape=jax.ShapeDtypeStruct((rows, n), _F32),
        compiler_params=pltpu.CompilerParams(dimension_semantics=("arbitrary",), vmem_limit_bytes=VMEM_LIMIT),
        name="ada",
    )(c, w_ada, b_ada)


def _proj_kernel(x_ref, sh_ref, sc_ref, ng_ref, w_ref, cos_ref, sin_ref, bd_ref, qg_ref, kg_ref,
                 lng_ref, lnb_ref, ws_ref, bs_ref, *out_refs, prompt, cb):
    nb, r, d = x_ref.shape
    m = nb * r

    x = x_ref[...]
    ms = jnp.mean(x * x, axis=-1, keepdims=True)
    h = (x * lax.rsqrt(ms + EPS) * ng_ref[...]) * (1.0 + sc_ref[...]) + sh_ref[...]
    h = h.reshape(m, d).astype(_BF16)

    def seg(a, b):
        return _dot(h, w_ref[:, a:b])

    o = 0
    q = seg(o, o + WIDTH_A); o += WIDTH_A
    k = seg(o, o + WIDTH_A); o += WIDTH_A
    v = seg(o, o + WIDTH_A); o += WIDTH_A
    qi = seg(o, o + IDX_HEADS * IDX_DIM); o += IDX_HEADS * IDX_DIM
    ki = seg(o, o + KI_PAD); o += KI_PAD
    wi = seg(o, o + WI_PAD); o += WI_PAD
    u = seg(o, o + WIDTH_B); o += WIDTH_B
    vg = seg(o, o + WIDTH_B)

    bd = bd_ref[...]

    def head_norm(t, g):
        hi, lo = _split_bf16(t * t)
        ss = _dot(hi, bd) + _dot(lo, bd)
        return t * lax.rsqrt(ss * (1.0 / HEAD_DIM) + EPS) * g

    cos1 = jnp.broadcast_to(cos_ref[...], (nb, r, LANES)).reshape(m, LANES)
    sin1 = jnp.broadcast_to(sin_ref[...], (nb, r, LANES)).reshape(m, LANES)

    def rope(t):
        w = t.shape[1]
        reps = w // LANES
        cosw = cos1 if reps == 1 else jnp.concatenate([cos1] * reps, axis=1)
        sinw = sin1 if reps == 1 else jnp.concatenate([sin1] * reps, axis=1)
        lane = lax.broadcasted_iota(jnp.int32, (1, w), 1)
        first_half = (lane % HEAD_DIM) < (HEAD_DIM // 2)
        rot = jnp.where(first_half, pltpu.roll(t, w - HEAD_DIM // 2, 1), pltpu.roll(t, HEAD_DIM // 2, 1))
        return t * cosw + rot * sinw

    qr = rope(head_norm(q, qg_ref[...]))
    kr = rope(head_norm(k, kg_ref[...]))
    qir = rope(qi)
    kir = rope(ki)
    wis = wi * ((IDX_DIM * IDX_HEADS) ** -0.5)

    ug = jax.nn.gelu(u)
    vgg = jax.nn.gelu(vg)
    mu = jnp.mean(vgg, axis=-1, keepdims=True)
    xc = vgg - mu
    var = jnp.mean(xc * xc, axis=-1, keepdims=True)
    vn = xc * lax.rsqrt(var + EPS) * lng_ref[...] + lnb_ref[...]
    vnb = vn.astype(_BF16)
    rowi = lax.broadcasted_iota(jnp.int32, (cb, cb), 0)
    coli = lax.broadcasted_iota(jnp.int32, (cb, cb), 1)
    ws_m = [jnp.where(rowi >= coli, ws_ref[g], 0.0).astype(_BF16) for g in range(N_GROUPS_B)]
    gm_rows = []
    for c in range(m // cb):
        pieces = []
        for g in range(N_GROUPS_B):
            lanes = slice(g * GROUP_DIM_B, (g + 1) * GROUP_DIM_B)
            mixed = _dot(ws_m[g], vnb[c * cb:(c + 1) * cb, lanes]) + bs_ref[:, lanes]
            pieces.append(ug[c * cb:(c + 1) * cb, lanes] * mixed)
        gm_rows.append(jnp.concatenate(pieces, axis=1))
    gm = gm_rows[0] if len(gm_rows) == 1 else jnp.concatenate(gm_rows, axis=0)

    def put(ref, val):
        ref[...] = val.astype(ref.dtype).reshape(ref.shape)

    if prompt:
        k_out, v_out, ki_out, q_bf, gm_out, kt_bf, v_bf, ki_bf, qit_bf, wit_out = out_refs
        put(kt_bf, kr.T)
        put(ki_bf, kir[:, :IDX_DIM])
        put(qit_bf, qir.T)
        put(wit_out, wis.T[:8, :])
        lane = lax.broadcasted_iota(jnp.int32, (1, LANES), 1)
        pieces = []
        for hd in range(N_HEADS_A):
            src = v[:, (hd // 2) * LANES:(hd // 2 + 1) * LANES]
            if hd % 2:
                src = pltpu.roll(src, HEAD_DIM, 1)
            pieces.append(jnp.where(lane < HEAD_DIM, src, 1.0))
        put(v_bf, jnp.concatenate(pieces, axis=1))
    else:
        k_out, v_out, ki_out, q_bf, gm_out, qi_bf, wi_out, vn_out = out_refs
        put(qi_bf, qir)
        put(wi_out, wis)
        put(vn_out, vn)
    put(k_out, kr)
    put(v_out, v)
    put(ki_out, kir[:, :IDX_DIM])
    put(q_bf, qr * Q_SCALE)
    put(gm_out, gm)


def _proj_call(x3, mod3, norm_g, w_in_p, cos_t, sin_t, bd, qg, kg, lng, lnb, ws, bs_b, *, nb, r, prompt):
    nseq, t, d = x3.shape
    m = nb * r
    cb = ws.shape[1]
    grid = (nseq // nb, t // r)
    const2 = lambda i, j: (0, 0)
    const3 = lambda i, j: (0, 0, 0)
    row_spec = lambda c: pl.BlockSpec((nb, r, c), lambda i, j: (i, j, 0))
    in_specs = [
        row_spec(d),
        pl.BlockSpec((nb, 1, d), lambda i, j: (i, 0, 0)),
        pl.BlockSpec((nb, 1, d), lambda i, j: (i, 0, 1)),
        pl.BlockSpec((1, d), const2),
        pl.BlockSpec(w_in_p.shape, const2, pipeline_mode=pl.Buffered(1)),
        pl.BlockSpec((1, r, LANES), lambda i, j: (0, j, 0)),
        pl.BlockSpec((1, r, LANES), lambda i, j: (0, j, 0)),
        pl.BlockSpec(bd.shape, const2),
        pl.BlockSpec((1, WIDTH_A), const2),
        pl.BlockSpec((1, WIDTH_A), const2),
        pl.BlockSpec((1, WIDTH_B), const2),
        pl.BlockSpec((1, WIDTH_B), const2),
        pl.BlockSpec(ws.shape, const3),
        pl.BlockSpec(bs_b.shape, const2),
    ]
    out_shape = [
        jax.ShapeDtypeStruct((nseq, t, WIDTH_A), _F32),
        jax.ShapeDtypeStruct((nseq, t, WIDTH_A), _F32),
        jax.ShapeDtypeStruct((nseq, t, IDX_DIM), _F32),
        jax.ShapeDtypeStruct((nseq, t, WIDTH_A), _BF16),
        jax.ShapeDtypeStruct((nseq, t, WIDTH_B), _BF16),
    ]
    out_specs = [row_spec(WIDTH_A), row_spec(WIDTH_A), row_spec(IDX_DIM), row_spec(WIDTH_A), row_spec(WIDTH_B)]
    if prompt:
        assert nb == 1
        nt = t // r
        out_shape += [
            jax.ShapeDtypeStruct((nseq, nt, WIDTH_A, r), _BF16),
            jax.ShapeDtypeStruct((nseq, t, N_HEADS_A * LANES), _BF16),
            jax.ShapeDtypeStruct((nseq, t, IDX_DIM), _BF16),
            jax.ShapeDtypeStruct((nseq, IDX_HEADS * IDX_DIM, t), _BF16),
            jax.ShapeDtypeStruct((nseq, 8, t), _F32),
        ]
        out_specs += [
            pl.BlockSpec((1, 1, WIDTH_A, r), lambda i, j: (i, j, 0, 0)),
            row_spec(N_HEADS_A * LANES),
            row_spec(IDX_DIM),
            pl.BlockSpec((1, IDX_HEADS * IDX_DIM, r), lambda i, j: (i, 0, j)),
            pl.BlockSpec((1, 8, r), lambda i, j: (i, 0, j)),
        ]
    else:
        out_shape += [
            jax.ShapeDtypeStruct((nseq, t, IDX_HEADS * IDX_DIM), _BF16),
            jax.ShapeDtypeStruct((nseq, t, LANES), _F32),
            jax.ShapeDtypeStruct((nseq, t, WIDTH_B), _F32),
        ]
        out_specs += [row_spec(IDX_HEADS * IDX_DIM), row_spec(LANES), row_spec(WIDTH_B)]
    return pl.pallas_call(
        functools.partial(_proj_kernel, prompt=prompt, cb=cb),
        grid=grid,
        in_specs=in_specs,
        out_specs=out_specs,
        out_shape=out_shape,
        compiler_params=pltpu.CompilerParams(
            dimension_semantics=("arbitrary", "arbitrary"), vmem_limit_bytes=VMEM_LIMIT),
        name="proj_prompt" if prompt else "proj_sample",
    )(x3, mod3, mod3, norm_g, w_in_p, cos_t, sin_t, bd, qg, kg, lng, lnb, ws, bs_b)


def _fori_by_pairs(n, body, init):
    if isinstance(n, int):
        return lax.fori_loop(0, n, body, init)
    carry = lax.fori_loop(0, n // 2, lambda p, c: body(2 * p + 1, body(2 * p, c)), init)
    return lax.cond(n % 2 == 1, lambda c: body(n - 1, c), lambda c: c, carry)


def _key_to_f32(key):
    bits = jnp.where(key >= 0, key, key ^ jnp.int32(0x7FFFFFFF))
    return lax.bitcast_convert_type(bits, _F32)


def _f32_to_key(x):
    bits = lax.bitcast_convert_type(x, jnp.int32)
    return jnp.where(bits >= 0, bits, bits ^ jnp.int32(0x7FFFFFFF))


def _count_keys(sct_ref, nkb, preds):
    _, tk, nq = sct_ref.shape
    g = COUNT_ACC_ROWS

    def body(j, accs):
        x = sct_ref[j]
        return tuple(acc + jnp.where(pred(x), 1.0, 0.0).reshape(tk // g, g, nq).sum(axis=0)
                     for acc, pred in zip(accs, preds))

    accs = _fori_by_pairs(nkb, body, tuple(jnp.zeros((g, nq), _F32) for _ in preds))
    return [jnp.sum(acc, axis=0, keepdims=True) for acc in accs]


def _scores_to_bias(sct_ref, bias_ref, score_max, nkb, topk):
    nblk, tk, nq = sct_ref.shape
    kf = jnp.float32(topk)
    total = jnp.asarray(nkb * tk).astype(_F32)

    def mid_of(lo, hi):
        return lo + lax.shift_right_logical(hi - lo, 1)

    def unsettled_rows(lo, cnt_lo, mid):
        return jnp.max(jnp.where((cnt_lo != kf) & (mid != lo), 1, 0))

    def cond(st):
        return (st[5] > 0) & (st[6] < 34)

    def body(st):
        return step(step(st))

    def step(st):
        lo, hi, cnt_lo, cnt_hi, mid, _, it = st
        thr_mid = _key_to_f32(mid)
        cnt = _count_keys(sct_ref, nkb, [lambda x: x >= thr_mid])[0]
        ge = cnt >= kf
        lo = jnp.where(ge, mid, lo)
        cnt_lo = jnp.where(ge, cnt, cnt_lo)
        hi = jnp.where(ge, hi, mid)
        cnt_hi = jnp.where(ge, cnt_hi, cnt)
        mid = mid_of(lo, hi)
        return lo, hi, cnt_lo, cnt_hi, mid, unsettled_rows(lo, cnt_lo, mid), it + 1

    key_p = _f32_to_key(score_max * 0.125)
    key_t = _f32_to_key(score_max * 2.0)
    probe_p, probe_t = _key_to_f32(key_p), _key_to_f32(key_t)
    c_ge0, c_gt0, c_p, c_t = _count_keys(
        sct_ref, nkb, [lambda x: x >= 0.0, lambda x: x > 0.0, lambda x: x >= probe_p, lambda x: x >= probe_t])
    is_pos = c_gt0 >= kf
    is_zero = c_ge0 >= kf
    p_low = (c_p >= kf) & (key_p > 1)
    p_high = (c_p < kf) & (key_p > 1)
    t_high = (c_t < kf) & (key_t > key_p) & (key_t > 1)
    lo = jnp.where(is_pos, jnp.where(p_low, key_p, 1), jnp.where(is_zero, 0, KEY_NEG_INF))
    cnt_lo = jnp.where(is_pos, jnp.where(p_low, c_p, c_gt0), jnp.where(is_zero, c_ge0, total))
    hi = jnp.where(is_pos, jnp.where(p_high, key_p, jnp.where(t_high, key_t, KEY_POS_INF)),
                   jnp.where(is_zero, 1, -1))
    cnt_hi = jnp.where(is_pos, jnp.where(p_high, c_p, jnp.where(t_high, c_t, 0.0)),
                       jnp.where(is_zero, c_gt0, c_ge0))
    mid = mid_of(lo, hi)
    init = (lo, hi, cnt_lo, cnt_hi, mid, unsettled_rows(lo, cnt_lo, mid), jnp.int32(0))
    lo, hi, cnt_lo, cnt_hi = lax.while_loop(cond, body, init)[:4]

    thr = _key_to_f32(lo)
    thr = jnp.where(jnp.abs(thr) < F32_MIN_NORMAL, 0.0, thr)
    thr = jnp.maximum(thr, -F32_MAX)
    tie = (cnt_lo > kf) & (lo > KEY_NEG_INF)
    need = jnp.where(lo == 1, -1.0, jnp.where(tie, kf - cnt_hi, 1e9))

    rowi = lax.broadcasted_iota(jnp.int32, (LANES, LANES), 0)
    coli = lax.broadcasted_iota(jnp.int32, (LANES, LANES), 1)
    tri = jnp.where(rowi >= coli, 1.0, 0.0).astype(_BF16)

    def fill(j, base):
        for c in range(tk // LANES):
            x = sct_ref[j, c * LANES:(c + 1) * LANES, :]
            eq = x == thr
            rank = base + _dot(tri, jnp.where(eq, 1.0, 0.0).astype(_BF16))
            keep = (x > thr) | (eq & (rank <= need))
            bias_ref[j, :, c * LANES:(c + 1) * LANES] = jnp.where(keep, 0.0, MASK_BIAS).T
            base = rank[LANES - 1:LANES, :]
        return base

    _fori_by_pairs(nkb, fill, jnp.zeros((1, nq), _F32))


def _dsa_prompt_kernel(q_ref, qit_ref, wit_ref, kg_ref, kt_ref, v_ref, ki_ref, o_ref, sct_ref, bias_ref, m_ref,
                       acc_ref, *, topk):
    i = pl.program_id(1)
    tq = q_ref.shape[1]
    tk = ki_ref.shape[2]
    nkb = ((i + 1) * tq + tk - 1) // tk

    wit = wit_ref[0]
    q_pos = i * tq + lax.broadcasted_iota(jnp.int32, (1, tq), 1)
    vis_lim = (q_pos // CHUNK + 1) * CHUNK
    key_in_block = lax.broadcasted_iota(jnp.int32, (tk, tq), 0)

    g = COUNT_ACC_ROWS

    def score_body(j, run_max):
        ki = ki_ref[0, j]
        acc = jnp.zeros((tk, tq), _F32)
        for h in range(IDX_HEADS):
            s = _dot(ki, qit_ref[0, h * IDX_DIM:(h + 1) * IDX_DIM, :])
            acc = acc + wit[h:h + 1, :] * jnp.maximum(s, 0.0)
        sc = jnp.where((j * tk + key_in_block) < vis_lim, acc, -jnp.inf)
        sct_ref[j] = sc
        return jnp.maximum(run_max, sc.reshape(tk // g, g, tq).max(axis=0))

    run_max = _fori_by_pairs(nkb, score_body, jnp.full((g, tq), -jnp.inf, _F32))
    _scores_to_bias(sct_ref, bias_ref, jnp.max(run_max, axis=0, keepdims=True), nkb, topk)

    reps = tk // LANES
    q_heads = [q_ref[0, :, h * HEAD_DIM:(h + 1) * HEAD_DIM] for h in range(N_HEADS_A)]

    def logits(j, h):
        return _dot(q_heads[h], kt_ref[0, j, h * HEAD_DIM:(h + 1) * HEAD_DIM, :]) + bias_ref[j]

    k_norm = jnp.max(jnp.abs(kg_ref[...])) * (HEAD_DIM ** 0.5)
    shift_max = jnp.float32(0.0)
    for h in range(N_HEADS_A):
        qf = q_heads[h].astype(_F32)
        bound = jnp.sqrt(jnp.sum(qf * qf, axis=1, keepdims=True)) * k_norm
        m_ref[h] = jnp.broadcast_to(bound, (tq, LANES))
        shift_max = jnp.maximum(shift_max, jnp.max(bound))

    @pl.when(shift_max > SHIFT_LIMIT)
    def _():
        m_ref[...] = jnp.full(m_ref.shape, -jnp.inf, _F32)

        def max_body(j, carry):
            for h in range(N_HEADS_A):
                s = logits(j, h)
                part = s[:, :LANES]
                for c in range(1, reps):
                    part = jnp.maximum(part, s[:, c * LANES:(c + 1) * LANES])
                m_ref[h] = jnp.maximum(m_ref[h], part)
            return carry

        lax.fori_loop(0, nkb, max_body, 0)
        for h in range(N_HEADS_A):
            m_ref[h] = jnp.broadcast_to(jnp.max(m_ref[h], axis=1, keepdims=True), (tq, LANES))

    acc_ref[...] = jnp.zeros(acc_ref.shape, _F32)

    def att_body(j, carry):
        for h in range(N_HEADS_A):
            p = jnp.exp2(logits(j, h) - jnp.concatenate([m_ref[h]] * reps, axis=1))
            acc_ref[h] += _dot(p.astype(_BF16), v_ref[0, j, :, h * LANES:(h + 1) * LANES])
        return carry

    _fori_by_pairs(nkb, att_body, 0)
    lane = lax.broadcasted_iota(jnp.int32, (1, LANES), 1)
    for hp in range(N_HEADS_A // 2):
        a0, a1 = acc_ref[2 * hp], acc_ref[2 * hp + 1]
        out0 = a0 / pltpu.roll(a0, HEAD_DIM, 1)
        out1 = pltpu.roll(a1, HEAD_DIM, 1) / a1
        o_ref[0, :, hp * LANES:(hp + 1) * LANES] = jnp.where(lane < HEAD_DIM, out0, out1).astype(o_ref.dtype)


def _dsa_prompt_call(q_bf, qit_bf, wit, kg, kt_bf, v_bf, ki_bf, *, topk):
    b, t, _ = q_bf.shape
    nt, tk = kt_bf.shape[1], kt_bf.shape[3]
    tq = min(DSA_TQ, t)
    v4 = v_bf.reshape(b, nt, tk, N_HEADS_A * LANES)
    ki4 = ki_bf.reshape(b, nt, tk, IDX_DIM)
    resident = lambda shape: pl.BlockSpec((1,) + shape, lambda bi, i: (bi, 0, 0, 0), pipeline_mode=pl.Buffered(1))
    return pl.pallas_call(
        functools.partial(_dsa_prompt_kernel, topk=topk),
        grid=(b, t // tq),
        in_specs=[
            pl.BlockSpec((1, tq, WIDTH_A), lambda bi, i: (bi, i, 0)),
            pl.BlockSpec((1, IDX_HEADS * IDX_DIM, tq), lambda bi, i: (bi, 0, i)),
            pl.BlockSpec((1, 8, tq), lambda bi, i: (bi, 0, i)),
            pl.BlockSpec((1, WIDTH_A), lambda bi, i: (0, 0)),
            resident((nt, WIDTH_A, tk)),
            resident((nt, tk, N_HEADS_A * LANES)),
            resident((nt, tk, IDX_DIM)),
        ],
        out_specs=pl.BlockSpec((1, tq, WIDTH_A), lambda bi, i: (bi, i, 0)),
        out_shape=jax.ShapeDtypeStruct((b, t, WIDTH_A), _BF16),
        scratch_shapes=[
            pltpu.VMEM((nt, tk, tq), _F32),
            pltpu.VMEM((nt, tq, tk), _F32),
            pltpu.VMEM((N_HEADS_A, tq, LANES), _F32),
            pltpu.VMEM((N_HEADS_A, tq, LANES), _F32),
        ],
        compiler_params=pltpu.CompilerParams(
            dimension_semantics=("arbitrary", "arbitrary"), vmem_limit_bytes=VMEM_LIMIT),
        name="dsa_prompt",
    )(q_bf, qit_bf, wit, kg, kt_bf, v4, ki4)


def _dsa_sample_kernel(q_ref, qi_ref, wi_ref, kn_ref, vn_ref, kin_ref, ck_ref, cv_ref, cki_ref, o_ref,
                       sct_ref, bias_ref, kall_ref, vall_ref, kiall_ref, *, topk):
    t = q_ref.shape[1]
    past = cki_ref.shape[2]
    s_pad, nq = sct_ref.shape[1], sct_ref.shape[2]
    tail = s_pad - past

    for src_ref, dst_ref in ((ck_ref, kall_ref), (cv_ref, vall_ref)):
        flat = src_ref.reshape(past * N_HEADS_A, HEAD_DIM)
        for hp in range(N_HEADS_A // 2):
            pair = [flat[pl.ds(2 * hp + e, past, stride=N_HEADS_A), :] for e in range(2)]
            dst_ref[:past, hp * LANES:(hp + 1) * LANES] = jnp.concatenate(pair, axis=1).astype(_BF16)
    kiall_ref[:past, :] = cki_ref[0, 0].astype(_BF16)
    zpad = jnp.zeros((tail - t, WIDTH_A), _BF16)
    kall_ref[past:, :] = jnp.concatenate([kn_ref[0].astype(_BF16), zpad], axis=0)
    vall_ref[past:, :] = jnp.concatenate([vn_ref[0].astype(_BF16), zpad], axis=0)
    kiall_ref[past:, :] = jnp.concatenate([kin_ref[0].astype(_BF16), jnp.zeros((tail - t, IDX_DIM), _BF16)], axis=0)

    qit = jnp.concatenate([qi_ref[0].astype(_F32)] * (nq // t), axis=0).T.astype(_BF16)
    wit = jnp.concatenate([wi_ref[0]] * (nq // t), axis=0).T
    acc = jnp.zeros((s_pad, nq), _F32)
    for h in range(IDX_HEADS):
        s = _dot(kiall_ref[...], qit[h * IDX_DIM:(h + 1) * IDX_DIM, :])
        acc = acc + wit[h:h + 1, :] * jnp.maximum(s, 0.0)
    key_pos = lax.broadcasted_iota(jnp.int32, (s_pad, nq), 0)
    sc = jnp.where(key_pos < past + t, acc, -jnp.inf)
    sct_ref[0] = sc
    _scores_to_bias(sct_ref, bias_ref, jnp.max(sc, axis=0, keepdims=True), 1, topk)

    nrow = N_HEADS_A * t
    row_head = lax.broadcasted_iota(jnp.int32, (nrow, WIDTH_A), 0) // t
    col_head = lax.broadcasted_iota(jnp.int32, (nrow, WIDTH_A), 1) // HEAD_DIM
    own = row_head == col_head
    q_rep = jnp.concatenate([q_ref[0]] * N_HEADS_A, axis=0)
    q_bd = jnp.where(own, q_rep, jnp.zeros_like(q_rep))
    bias = bias_ref[0, :t, :]
    s = _dot_nt(q_bd, kall_ref[...]) + jnp.concatenate([bias] * N_HEADS_A, axis=0)
    m = jnp.max(s, axis=1, keepdims=True)
    p = jnp.exp2(s - m)
    l = jnp.sum(p, axis=1, keepdims=True)
    pv = _dot(p.astype(_BF16), vall_ref[...]) / l
    pv = jnp.where(own, pv, 0.0)
    out = pv[:t]
    for h in range(1, N_HEADS_A):
        out = out + pv[h * t:(h + 1) * t]
    o_ref[0] = out.astype(o_ref.dtype)


def _dsa_sample_call(q_bf, qi_bf, wi, k_new, v_new, ki_new, cache_k, cache_v, cache_kidx, *, topk):
    b, t, _ = q_bf.shape
    past = cache_kidx.shape[2]
    s_pad = past + LANES
    assert LANES % t == 0
    row = lambda c: pl.BlockSpec((1, t, c), lambda bi: (bi, 0, 0))
    cache = lambda rows, c: pl.BlockSpec((1, 1, rows, c), lambda bi: (0, bi, 0, 0))
    return pl.pallas_call(
        functools.partial(_dsa_sample_kernel, topk=topk),
        grid=(b,),
        in_specs=[row(WIDTH_A), row(IDX_HEADS * IDX_DIM), row(LANES), row(WIDTH_A), row(WIDTH_A), row(IDX_DIM),
                  pl.BlockSpec((1, past, N_HEADS_A, HEAD_DIM), lambda bi: (bi, 0, 0, 0)),
                  pl.BlockSpec((1, past, N_HEADS_A, HEAD_DIM), lambda bi: (bi, 0, 0, 0)),
                  cache(past, IDX_DIM)],
        out_specs=row(WIDTH_A),
        out_shape=jax.ShapeDtypeStruct((b, t, WIDTH_A), _BF16),
        scratch_shapes=[
            pltpu.VMEM((1, s_pad, LANES), _F32),
            pltpu.VMEM((1, LANES, s_pad), _F32),
            pltpu.VMEM((s_pad, WIDTH_A), _BF16),
            pltpu.VMEM((s_pad, WIDTH_A), _BF16),
            pltpu.VMEM((s_pad, IDX_DIM), _BF16),
        ],
        compiler_params=pltpu.CompilerParams(dimension_semantics=("arbitrary",), vmem_limit_bytes=VMEM_LIMIT),
        name="dsa_sample",
    )(q_bf, qi_bf, wi, k_new, v_new, ki_new, cache_k, cache_v, cache_kidx)


def _out_kernel(x_ref, at_ref, gm_ref, g1_ref, sh2_ref, sc2_ref, g2_ref, ng_ref, wo_ref, w1_ref, w2_ref, y_ref):
    nb, r, d = x_ref.shape
    m = nb * r
    at = at_ref[...].reshape(m, WIDTH_A)
    gm = gm_ref[...].reshape(m, WIDTH_B)
    y = _dot(at, wo_ref[:WIDTH_A, :]) + _dot(gm, wo_ref[WIDTH_A:, :])
    x1 = x_ref[...] + g1_ref[...] * y.reshape(nb, r, d)
    ms = jnp.mean(x1 * x1, axis=-1, keepdims=True)
    h2 = (x1 * lax.rsqrt(ms + EPS) * ng_ref[...]) * (1.0 + sc2_ref[...]) + sh2_ref[...]
    h2 = h2.reshape(m, d).astype(_BF16)
    dff = w1_ref.shape[1]
    fc = 1024
    ff = jnp.zeros((m, d), _F32)
    for c in range(dff // fc):
        a = jnp.maximum(_dot(h2, w1_ref[:, c * fc:(c + 1) * fc]), 0.0)
        ff = ff + _dot((a * a).astype(_BF16), w2_ref[c * fc:(c + 1) * fc, :])
    y_ref[...] = x1 + g2_ref[...] * ff.reshape(nb, r, d)


def _out_call(x3, attn, gm, mod3, norm_g, w_out, w_ff1, w_ff2, *, nb, r, name):
    nseq, t, d = x3.shape
    const2 = lambda i, j: (0, 0)
    row_spec = lambda c: pl.BlockSpec((nb, r, c), lambda i, j: (i, j, 0))
    mod_spec = lambda col: pl.BlockSpec((nb, 1, d), lambda i, j: (i, 0, col))
    return pl.pallas_call(
        _out_kernel,
        grid=(nseq // nb, t // r),
        in_specs=[row_spec(d), row_spec(WIDTH_A), row_spec(WIDTH_B),
                  mod_spec(2), mod_spec(3), mod_spec(4), mod_spec(5),
                  pl.BlockSpec((1, d), const2),
                  pl.BlockSpec(w_out.shape, const2, pipeline_mode=pl.Buffered(1)),
                  pl.BlockSpec(w_ff1.shape, const2, pipeline_mode=pl.Buffered(1)),
                  pl.BlockSpec(w_ff2.shape, const2, pipeline_mode=pl.Buffered(1))],
        out_specs=row_spec(d),
        out_shape=jax.ShapeDtypeStruct((nseq, t, d), _F32),
        compiler_params=pltpu.CompilerParams(
            dimension_semantics=("arbitrary", "arbitrary"), vmem_limit_bytes=VMEM_LIMIT),
        name=name,
    )(x3, attn, gm, mod3, mod3, mod3, mod3, norm_g, w_out, w_ff1, w_ff2)


def _rope_tables(pos):
    half = HEAD_DIM // 2
    inv_freq = jnp.power(jnp.float32(ROPE_THETA), -jnp.arange(half, dtype=_F32) / half)
    ang = pos.astype(_F32)[:, None] * inv_freq[None, :]
    cos, sin = jnp.cos(ang), jnp.sin(ang)
    reps = LANES // HEAD_DIM
    cos_t = jnp.tile(jnp.concatenate([cos, cos], axis=1), (1, reps))
    sin_t = jnp.tile(jnp.concatenate([-sin, sin], axis=1), (1, reps))
    return cos_t[None], sin_t[None]


def _pad_w_in(w_in):
    d = w_in.shape[0]
    a, i = WIDTH_A, IDX_HEADS * IDX_DIM
    o_ki = 3 * a + i
    o_wi = o_ki + IDX_DIM
    o_u = o_wi + IDX_HEADS
    z = lambda n: jnp.zeros((d, n), w_in.dtype)
    cols = [w_in[:, :o_ki], w_in[:, o_ki:o_wi], z(KI_PAD - IDX_DIM), w_in[:, o_wi:o_u], z(WI_PAD - IDX_HEADS),
            w_in[:, o_u:]]
    return jnp.concatenate(cols, axis=1).astype(_BF16)


def _layer(x_prompt, x_sample, c_prompt, c_sample, cache_k, cache_v, cache_kidx, w_ada, b_ada, norm1_g, norm2_g,
           w_in, q_norm_g, k_norm_g, gmlp_ln_g, gmlp_ln_b, gmlp_ws, gmlp_bs, w_out, w_ff1, w_ff2):
    bp, tp, d = x_prompt.shape
    bs_, ts, _ = x_sample.shape
    past = cache_k.shape[1]

    c_all = jnp.concatenate([c_prompt, c_sample], axis=0)
    rows = -(-c_all.shape[0] // 16) * 16
    c_all = jnp.pad(c_all, ((0, rows - c_all.shape[0]), (0, 0)))
    mod = _ada_call(c_all, w_ada, b_ada[None, :])
    mod_p = mod[:bp, None, :]
    mod_s = mod[bp:bp + bs_, None, :]

    w_in_p = _pad_w_in(w_in)
    w_out_b, w_ff1_b, w_ff2_b = w_out.astype(_BF16), w_ff1.astype(_BF16), w_ff2.astype(_BF16)
    head_of = jnp.arange(WIDTH_A) // HEAD_DIM
    bd = (head_of[:, None] == head_of[None, :]).astype(_BF16)
    qg = jnp.tile(q_norm_g, N_HEADS_A)[None, :]
    kg = jnp.tile(k_norm_g, N_HEADS_A)[None, :]
    lng, lnb = gmlp_ln_g[None, :], gmlp_ln_b[None, :]
    n1, n2 = norm1_g[None, :], norm2_g[None, :]

    rp = min(PROJ_ROWS, tp)
    cos_p, sin_p = _rope_tables(jnp.arange(tp))
    lp = min(tp, GMLP_CHUNK)
    bs_p = jnp.repeat(jnp.transpose(gmlp_bs[:, :lp]), GROUP_DIM_B, axis=1)
    (kp, vp, kip, q_bf, gm_p, kt_bf, v_bf, ki_bf, qit_bf, wit) = _proj_call(
        x_prompt, mod_p, n1, w_in_p, cos_p, sin_p, bd, qg, kg, lng, lnb, gmlp_ws[:, :lp, :lp], bs_p,
        nb=1, r=rp, prompt=True)
    attn_p = _dsa_prompt_call(q_bf, qit_bf, wit, kg, kt_bf, v_bf, ki_bf, topk=min(TOPK_MAX, tp // 4))
    yp = _out_call(x_prompt, attn_p, gm_p, mod_p, n2, w_out_b, w_ff1_b, w_ff2_b, nb=1, r=rp, name="out_prompt")

    nb = min(SAMPLE_GROUP, bs_)
    cos_s, sin_s = _rope_tables(past + jnp.arange(ts))
    ls = min(ts, GMLP_CHUNK)
    assert ls == ts
    eye = jnp.eye(nb, dtype=gmlp_ws.dtype)
    ws_s = jax.vmap(lambda w: jnp.kron(eye, w))(gmlp_ws[:, :ls, :ls])
    bs_s = jnp.tile(jnp.repeat(jnp.transpose(gmlp_bs[:, :ls]), GROUP_DIM_B, axis=1), (nb, 1))
    (ks, vs, kis, qs_bf, gm_s, qis_bf, wi_s, gvs) = _proj_call(
        x_sample, mod_s, n1, w_in_p, cos_s, sin_s, bd, qg, kg, lng, lnb, ws_s, bs_s,
        nb=nb, r=ts, prompt=False)
    attn_s = _dsa_sample_call(
        qs_bf, qis_bf, wi_s, ks, vs, kis,
        cache_k, cache_v,
        cache_kidx.reshape(1, bs_, past, IDX_DIM), topk=min(TOPK_MAX, (past + ts) // 4))
    ys = _out_call(x_sample, attn_s, gm_s, mod_s, n2, w_out_b, w_ff1_b, w_ff2_b, nb=nb, r=ts, name="out_sample")

    heads = lambda a: a.reshape(a.shape[0], a.shape[1], N_HEADS_A, HEAD_DIM)
    return yp, ys, heads(kp), heads(vp), kip, heads(ks), heads(vs), kis, gvs


def kernel(x_prompt, x_sample, c_prompt, c_sample, cache_k, cache_v, cache_kidx, w_ada, b_ada, norm1_g, norm2_g,
           w_in, q_norm_g, k_norm_g, gmlp_ln_g, gmlp_ln_b, gmlp_ws, gmlp_bs, w_out, w_ff1, w_ff2):
    depth = w_ada.shape[0]
    yp, ys = x_prompt, x_sample
    outs = [[] for _ in range(7)]
    for l in range(depth):
        res = _layer(yp, ys, c_prompt, c_sample, cache_k[l], cache_v[l], cache_kidx[l], w_ada[l], b_ada[l],
                     norm1_g[l], norm2_g[l], w_in[l], q_norm_g[l], k_norm_g[l], gmlp_ln_g[l], gmlp_ln_b[l],
                     gmlp_ws[l], gmlp_bs[l], w_out[l], w_ff1[l], w_ff2[l])
        yp, ys = res[0], res[1]
        for acc, leaf in zip(outs, res[2:]):
            acc.append(leaf)
    return (yp, ys) + tuple(jnp.stack(o) for o in outs)
```

```python
import functools

import jax
import jax.numpy as jnp
from jax import lax
from jax.experimental import pallas as pl
from jax.experimental.pallas import tpu as pltpu

N_HEADS_A = 8
HEAD_DIM = 64
WIDTH_A = N_HEADS_A * HEAD_DIM
IDX_HEADS = 4
IDX_DIM = 64
N_GROUPS_B = 4
GROUP_DIM_B = 128
WIDTH_B = N_GROUPS_B * GROUP_DIM_B
GMLP_CHUNK = 128
CHUNK = 64
TOPK_MAX = 256
ROPE_THETA = 10000.0
EPS = 1e-6

LANES = 128
KI_PAD = LANES
WI_PAD = LANES
IN_WIDTH_PADDED = 3 * WIDTH_A + IDX_HEADS * IDX_DIM + KI_PAD + WI_PAD + 2 * WIDTH_B

PROJ_ROWS = 512
DSA_TQ = 256
DSA_TK = PROJ_ROWS
SAMPLE_GROUP = 8
COUNT_ACC_ROWS = 32
VMEM_LIMIT = 56 * 1024 * 1024

MASK_BIAS = -1e30
Q_SCALE = HEAD_DIM ** -0.5 * 1.4426950408889634
SHIFT_LIMIT = 30.0
KEY_NEG_INF = -2139095041
KEY_POS_INF = 2139095040
F32_MIN_NORMAL = 1.1754944e-38
F32_MAX = 3.4028235e38

_F32 = jnp.float32
_BF16 = jnp.bfloat16


def _dot(a, b):
    return jnp.dot(a, b, preferred_element_type=_F32)


def _dot_nt(a, b):
    return lax.dot_general(a, b, (((1,), (1,)), ((), ())), preferred_element_type=_F32)


def _split_bf16(a):
    hi = a.astype(_BF16)
    lo = (a - hi.astype(_F32)).astype(_BF16)
    return hi, lo


def _ada_kernel(c_ref, w_ref, b_ref, o_ref):
    c = c_ref[...]
    s_hi, s_lo = _split_bf16(c * jax.nn.sigmoid(c))
    w_hi, w_lo = _split_bf16(w_ref[...])
    o_ref[...] = _dot(s_hi, w_hi) + _dot(s_lo, w_hi) + _dot(s_hi, w_lo) + b_ref[...]


def _ada_call(c, w_ada, b_ada):
    rows, d = c.shape
    n = w_ada.shape[1]
    tn = 1024
    return pl.pallas_call(
        _ada_kernel,
        grid=(n // tn,),
        in_specs=[
            pl.BlockSpec((rows, d), lambda j: (0, 0)),
            pl.BlockSpec((d, tn), lambda j: (0, j)),
            pl.BlockSpec((1, tn), lambda j: (0, j)),
        ],
        out_specs=pl.BlockSpec((rows, tn), lambda j: (0, j)),
        out_shape=jax.ShapeDtypeStruct((rows, n), _F32),
        compiler_params=pltpu.CompilerParams(dimension_semantics=("arbitrary",), vmem_limit_bytes=VMEM_LIMIT),
        name="ada",
    )(c, w_ada, b_ada)


def _proj_kernel(x_ref, sh_ref, sc_ref, ng_ref, w_ref, cos_ref, sin_ref, bd_ref, qg_ref, kg_ref,
                 lng_ref, lnb_ref, ws_ref, bs_ref, *out_refs, prompt, cb):
    nb, r, d = x_ref.shape
    m = nb * r

    x = x_ref[...]
    ms = jnp.mean(x * x, axis=-1, keepdims=True)
    h = (x * lax.rsqrt(ms + EPS) * ng_ref[...]) * (1.0 + sc_ref[...]) + sh_ref[...]
    h = h.reshape(m, d).astype(_BF16)

    def seg(a, b):
        return _dot(h, w_ref[:, a:b])

    o = 0
    q = seg(o, o + WIDTH_A); o += WIDTH_A
    k = seg(o, o + WIDTH_A); o += WIDTH_A
    v = seg(o, o + WIDTH_A); o += WIDTH_A
    qi = seg(o, o + IDX_HEADS * IDX_DIM); o += IDX_HEADS * IDX_DIM
    ki = seg(o, o + KI_PAD); o += KI_PAD
    wi = seg(o, o + WI_PAD); o += WI_PAD
    u = seg(o, o + WIDTH_B); o += WIDTH_B
    vg = seg(o, o + WIDTH_B)

    bd = bd_ref[...]

    def head_norm(t, g):
        hi, lo = _split_bf16(t * t)
        ss = _dot(hi, bd) + _dot(lo, bd)
        return t * lax.rsqrt(ss * (1.0 / HEAD_DIM) + EPS) * g

    cos1 = jnp.broadcast_to(cos_ref[...], (nb, r, LANES)).reshape(m, LANES)
    sin1 = jnp.broadcast_to(sin_ref[...], (nb, r, LANES)).reshape(m, LANES)

    def rope(t):
        w = t.shape[1]
        reps = w // LANES
        cosw = cos1 if reps == 1 else jnp.concatenate([cos1] * reps, axis=1)
        sinw = sin1 if reps == 1 else jnp.concatenate([sin1] * reps, axis=1)
        lane = lax.broadcasted_iota(jnp.int32, (1, w), 1)
        first_half = (lane % HEAD_DIM) < (HEAD_DIM // 2)
        rot = jnp.where(first_half, pltpu.roll(t, w - HEAD_DIM // 2, 1), pltpu.roll(t, HEAD_DIM // 2, 1))
        return t * cosw + rot * sinw

    qr = rope(head_norm(q, qg_ref[...]))
    kr = rope(head_norm(k, kg_ref[...]))
    qir = rope(qi)
    kir = rope(ki)
    wis = wi * ((IDX_DIM * IDX_HEADS) ** -0.5)

    ug = jax.nn.gelu(u)
    vgg = jax.nn.gelu(vg)
    mu = jnp.mean(vgg, axis=-1, keepdims=True)
    xc = vgg - mu
    var = jnp.mean(xc * xc, axis=-1, keepdims=True)
    vn = xc * lax.rsqrt(var + EPS) * lng_ref[...] + lnb_ref[...]
    vnb = vn.astype(_BF16)
    rowi = lax.broadcasted_iota(jnp.int32, (cb, cb), 0)
    coli = lax.broadcasted_iota(jnp.int32, (cb, cb), 1)
    ws_m = [jnp.where(rowi >= coli, ws_ref[g], 0.0).astype(_BF16) for g in range(N_GROUPS_B)]
    gm_rows = []
    for c in range(m // cb):
        pieces = []
        for g in range(N_GROUPS_B):
            lanes = slice(g * GROUP_DIM_B, (g + 1) * GROUP_DIM_B)
            mixed = _dot(ws_m[g], vnb[c * cb:(c + 1) * cb, lanes]) + bs_ref[:, lanes]
            pieces.append(ug[c * cb:(c + 1) * cb, lanes] * mixed)
        gm_rows.append(jnp.concatenate(pieces, axis=1))
    gm = gm_rows[0] if len(gm_rows) == 1 else jnp.concatenate(gm_rows, axis=0)

    def put(ref, val):
        ref[...] = val.astype(ref.dtype).reshape(ref.shape)

    if prompt:
        k_out, v_out, ki_out, q_bf, gm_out, kt_bf, v_bf, ki_bf, qit_bf, wit_out = out_refs
        krt = kr.T
        put(k_out, krt)
        put(v_out, v.T)
        put(ki_out, kir.T[:IDX_DIM, :])
        put(kt_bf, krt)
        put(ki_bf, kir[:, :IDX_DIM])
        put(qit_bf, qir.T)
        put(wit_out, wis.T[:8, :])
        lane = lax.broadcasted_iota(jnp.int32, (1, LANES), 1)
        pieces = []
        for hd in range(N_HEADS_A):
            src = v[:, (hd // 2) * LANES:(hd // 2 + 1) * LANES]
            if hd % 2:
                src = pltpu.roll(src, HEAD_DIM, 1)
            pieces.append(jnp.where(lane < HEAD_DIM, src, 1.0))
        put(v_bf, jnp.concatenate(pieces, axis=1))
    else:
        k_out, v_out, ki_out, q_bf, gm_out, qi_bf, wi_out, vn_out = out_refs
        put(qi_bf, qir)
        put(wi_out, wis)
        put(vn_out, vn)
        put(k_out, kr)
        put(v_out, v)
        put(ki_out, kir[:, :IDX_DIM])
    put(q_bf, qr * Q_SCALE)
    put(gm_out, gm)


def _proj_call(x3, mod3, norm_g, w_in_p, cos_t, sin_t, bd, qg, kg, lng, lnb, ws, bs_b, *, nb, r, prompt):
    nseq, t, d = x3.shape
    m = nb * r
    cb = ws.shape[1]
    grid = (nseq // nb, t // r)
    const2 = lambda i, j: (0, 0)
    const3 = lambda i, j: (0, 0, 0)
    row_spec = lambda c: pl.BlockSpec((nb, r, c), lambda i, j: (i, j, 0))
    in_specs = [
        row_spec(d),
        pl.BlockSpec((nb, 1, d), lambda i, j: (i, 0, 0)),
        pl.BlockSpec((nb, 1, d), lambda i, j: (i, 0, 1)),
        pl.BlockSpec((1, d), const2),
        pl.BlockSpec(w_in_p.shape, const2, pipeline_mode=pl.Buffered(1)),
        pl.BlockSpec((1, r, LANES), lambda i, j: (0, j, 0)),
        pl.BlockSpec((1, r, LANES), lambda i, j: (0, j, 0)),
        pl.BlockSpec(bd.shape, const2),
        pl.BlockSpec((1, WIDTH_A), const2),
        pl.BlockSpec((1, WIDTH_A), const2),
        pl.BlockSpec((1, WIDTH_B), const2),
        pl.BlockSpec((1, WIDTH_B), const2),
        pl.BlockSpec(ws.shape, const3),
        pl.BlockSpec(bs_b.shape, const2),
    ]
    if prompt:
        leaf = lambda c: jax.ShapeDtypeStruct((nseq, c, t), _F32)
        leaf_spec = lambda c: pl.BlockSpec((nb, c, r), lambda i, j: (i, 0, j))
    else:
        leaf = lambda c: jax.ShapeDtypeStruct((nseq, t, c), _F32)
        leaf_spec = row_spec
    out_shape = [
        leaf(WIDTH_A), leaf(WIDTH_A), leaf(IDX_DIM),
        jax.ShapeDtypeStruct((nseq, t, WIDTH_A), _BF16),
        jax.ShapeDtypeStruct((nseq, t, WIDTH_B), _BF16),
    ]
    out_specs = [leaf_spec(WIDTH_A), leaf_spec(WIDTH_A), leaf_spec(IDX_DIM), row_spec(WIDTH_A), row_spec(WIDTH_B)]
    if prompt:
        assert nb == 1
        nt = t // r
        out_shape += [
            jax.ShapeDtypeStruct((nseq, nt, WIDTH_A, r), _BF16),
            jax.ShapeDtypeStruct((nseq, t, N_HEADS_A * LANES), _BF16),
            jax.ShapeDtypeStruct((nseq, t, IDX_DIM), _BF16),
            jax.ShapeDtypeStruct((nseq, IDX_HEADS * IDX_DIM, t), _BF16),
            jax.ShapeDtypeStruct((nseq, 8, t), _F32),
        ]
        out_specs += [
            pl.BlockSpec((1, 1, WIDTH_A, r), lambda i, j: (i, j, 0, 0)),
            row_spec(N_HEADS_A * LANES),
            row_spec(IDX_DIM),
            pl.BlockSpec((1, IDX_HEADS * IDX_DIM, r), lambda i, j: (i, 0, j)),
            pl.BlockSpec((1, 8, r), lambda i, j: (i, 0, j)),
        ]
    else:
        out_shape += [
            jax.ShapeDtypeStruct((nseq, t, IDX_HEADS * IDX_DIM), _BF16),
            jax.ShapeDtypeStruct((nseq, t, LANES), _F32),
            jax.ShapeDtypeStruct((nseq, t, WIDTH_B), _F32),
        ]
        out_specs += [row_spec(IDX_HEADS * IDX_DIM), row_spec(LANES), row_spec(WIDTH_B)]
    return pl.pallas_call(
        functools.partial(_proj_kernel, prompt=prompt, cb=cb),
        grid=grid,
        in_specs=in_specs,
        out_specs=out_specs,
        out_shape=out_shape,
        compiler_params=pltpu.CompilerParams(
            dimension_semantics=("arbitrary", "arbitrary"), vmem_limit_bytes=VMEM_LIMIT),
        name="proj_prompt" if prompt else "proj_sample",
    )(x3, mod3, mod3, norm_g, w_in_p, cos_t, sin_t, bd, qg, kg, lng, lnb, ws, bs_b)


def _fori_by_pairs(n, body, init):
    if isinstance(n, int):
        return lax.fori_loop(0, n, body, init)
    carry = lax.fori_loop(0, n // 2, lambda p, c: body(2 * p + 1, body(2 * p, c)), init)
    return lax.cond(n % 2 == 1, lambda c: body(n - 1, c), lambda c: c, carry)


def _key_to_f32(key):
    bits = jnp.where(key >= 0, key, key ^ jnp.int32(0x7FFFFFFF))
    return lax.bitcast_convert_type(bits, _F32)


def _f32_to_key(x):
    bits = lax.bitcast_convert_type(x, jnp.int32)
    return jnp.where(bits >= 0, bits, bits ^ jnp.int32(0x7FFFFFFF))


def _count_keys(sct_ref, nkb, preds):
    _, tk, nq = sct_ref.shape
    g = COUNT_ACC_ROWS

    def body(j, accs):
        x = sct_ref[j]
        return tuple(acc + jnp.where(pred(x), 1.0, 0.0).reshape(tk // g, g, nq).sum(axis=0)
                     for acc, pred in zip(accs, preds))

    accs = _fori_by_pairs(nkb, body, tuple(jnp.zeros((g, nq), _F32) for _ in preds))
    return [jnp.sum(acc, axis=0, keepdims=True) for acc in accs]


def _scores_to_bias(sct_ref, bias_ref, score_max, nkb, topk):
    nblk, tk, nq = sct_ref.shape
    kf = jnp.float32(topk)
    total = jnp.asarray(nkb * tk).astype(_F32)

    def mid_of(lo, hi):
        return lo + lax.shift_right_logical(hi - lo, 1)

    def unsettled_rows(lo, cnt_lo, mid):
        return jnp.max(jnp.where((cnt_lo != kf) & (mid != lo), 1, 0))

    def cond(st):
        return (st[5] > 0) & (st[6] < 34)

    def body(st):
        return step(step(st))

    def step(st):
        lo, hi, cnt_lo, cnt_hi, mid, _, it = st
        thr_mid = _key_to_f32(mid)
        cnt = _count_keys(sct_ref, nkb, [lambda x: x >= thr_mid])[0]
        ge = cnt >= kf
        lo = jnp.where(ge, mid, lo)
        cnt_lo = jnp.where(ge, cnt, cnt_lo)
        hi = jnp.where(ge, hi, mid)
        cnt_hi = jnp.where(ge, cnt_hi, cnt)
        mid = mid_of(lo, hi)
        return lo, hi, cnt_lo, cnt_hi, mid, unsettled_rows(lo, cnt_lo, mid), it + 1

    key_p = _f32_to_key(score_max * 0.125)
    key_t = _f32_to_key(score_max * 2.0)
    probe_p, probe_t = _key_to_f32(key_p), _key_to_f32(key_t)
    c_ge0, c_gt0, c_p, c_t = _count_keys(
        sct_ref, nkb, [lambda x: x >= 0.0, lambda x: x > 0.0, lambda x: x >= probe_p, lambda x: x >= probe_t])
    is_pos = c_gt0 >= kf
    is_zero = c_ge0 >= kf
    p_low = (c_p >= kf) & (key_p > 1)
    p_high = (c_p < kf) & (key_p > 1)
    t_high = (c_t < kf) & (key_t > key_p) & (key_t > 1)
    lo = jnp.where(is_pos, jnp.where(p_low, key_p, 1), jnp.where(is_zero, 0, KEY_NEG_INF))
    cnt_lo = jnp.where(is_pos, jnp.where(p_low, c_p, c_gt0), jnp.where(is_zero, c_ge0, total))
    hi = jnp.where(is_pos, jnp.where(p_high, key_p, jnp.where(t_high, key_t, KEY_POS_INF)),
                   jnp.where(is_zero, 1, -1))
    cnt_hi = jnp.where(is_pos, jnp.where(p_high, c_p, jnp.where(t_high, c_t, 0.0)),
                       jnp.where(is_zero, c_gt0, c_ge0))
    mid = mid_of(lo, hi)
    init = (lo, hi, cnt_lo, cnt_hi, mid, unsettled_rows(lo, cnt_lo, mid), jnp.int32(0))
    lo, hi, cnt_lo, cnt_hi = lax.while_loop(cond, body, init)[:4]

    thr = _key_to_f32(lo)
    thr = jnp.where(jnp.abs(thr) < F32_MIN_NORMAL, 0.0, thr)
    thr = jnp.maximum(thr, -F32_MAX)
    tie = (cnt_lo > kf) & (lo > KEY_NEG_INF)
    need = jnp.where(lo == 1, -1.0, jnp.where(tie, kf - cnt_hi, 1e9))

    rowi = lax.broadcasted_iota(jnp.int32, (LANES, LANES), 0)
    coli = lax.broadcasted_iota(jnp.int32, (LANES, LANES), 1)
    tri = jnp.where(rowi >= coli, 1.0, 0.0).astype(_BF16)

    def fill(j, base):
        for c in range(tk // LANES):
            x = sct_ref[j, c * LANES:(c + 1) * LANES, :]
            eq = x == thr
            rank = base + _dot(tri, jnp.where(eq, 1.0, 0.0).astype(_BF16))
            keep = (x > thr) | (eq & (rank <= need))
            bias_ref[j, :, c * LANES:(c + 1) * LANES] = jnp.where(keep, 0.0, MASK_BIAS).T
            base = rank[LANES - 1:LANES, :]
        return base

    _fori_by_pairs(nkb, fill, jnp.zeros((1, nq), _F32))


def _dsa_prompt_kernel(q_ref, qit_ref, wit_ref, kg_ref, kt_ref, v_ref, ki_ref, o_ref, sct_ref, bias_ref, m_ref,
                       acc_ref, *, topk):
    i = pl.program_id(1)
    tq = q_ref.shape[1]
    tk = ki_ref.shape[2]
    nkb = ((i + 1) * tq + tk - 1) // tk

    wit = wit_ref[0]
    q_pos = i * tq + lax.broadcasted_iota(jnp.int32, (1, tq), 1)
    vis_lim = (q_pos // CHUNK + 1) * CHUNK
    key_in_block = lax.broadcasted_iota(jnp.int32, (tk, tq), 0)

    g = COUNT_ACC_ROWS

    def score_body(j, run_max):
        ki = ki_ref[0, j]
        acc = jnp.zeros((tk, tq), _F32)
        for h in range(IDX_HEADS):
            s = _dot(ki, qit_ref[0, h * IDX_DIM:(h + 1) * IDX_DIM, :])
            acc = acc + wit[h:h + 1, :] * jnp.maximum(s, 0.0)
        sc = jnp.where((j * tk + key_in_block) < vis_lim, acc, -jnp.inf)
        sct_ref[j] = sc
        return jnp.maximum(run_max, sc.reshape(tk // g, g, tq).max(axis=0))

    run_max = _fori_by_pairs(nkb, score_body, jnp.full((g, tq), -jnp.inf, _F32))
    _scores_to_bias(sct_ref, bias_ref, jnp.max(run_max, axis=0, keepdims=True), nkb, topk)

    reps = tk // LANES
    q_heads = [q_ref[0, :, h * HEAD_DIM:(h + 1) * HEAD_DIM] for h in range(N_HEADS_A)]

    def logits(j, h):
        return _dot(q_heads[h], kt_ref[0, j, h * HEAD_DIM:(h + 1) * HEAD_DIM, :]) + bias_ref[j]

    k_norm = jnp.max(jnp.abs(kg_ref[...])) * (HEAD_DIM ** 0.5)
    shift_max = jnp.float32(0.0)
    for h in range(N_HEADS_A):
        qf = q_heads[h].astype(_F32)
        bound = jnp.sqrt(jnp.sum(qf * qf, axis=1, keepdims=True)) * k_norm
        m_ref[h] = jnp.broadcast_to(bound, (tq, LANES))
        shift_max = jnp.maximum(shift_max, jnp.max(bound))

    @pl.when(shift_max > SHIFT_LIMIT)
    def _():
        m_ref[...] = jnp.full(m_ref.shape, -jnp.inf, _F32)

        def max_body(j, carry):
            for h in range(N_HEADS_A):
                s = logits(j, h)
                part = s[:, :LANES]
                for c in range(1, reps):
                    part = jnp.maximum(part, s[:, c * LANES:(c + 1) * LANES])
                m_ref[h] = jnp.maximum(m_ref[h], part)
            return carry

        lax.fori_loop(0, nkb, max_body, 0)
        for h in range(N_HEADS_A):
            m_ref[h] = jnp.broadcast_to(jnp.max(m_ref[h], axis=1, keepdims=True), (tq, LANES))

    acc_ref[...] = jnp.zeros(acc_ref.shape, _F32)

    def att_body(j, carry):
        for h in range(N_HEADS_A):
            p = jnp.exp2(logits(j, h) - jnp.concatenate([m_ref[h]] * reps, axis=1))
            acc_ref[h] += _dot(p.astype(_BF16), v_ref[0, j, :, h * LANES:(h + 1) * LANES])
        return carry

    _fori_by_pairs(nkb, att_body, 0)
    lane = lax.broadcasted_iota(jnp.int32, (1, LANES), 1)
    for hp in range(N_HEADS_A // 2):
        a0, a1 = acc_ref[2 * hp], acc_ref[2 * hp + 1]
        out0 = a0 / pltpu.roll(a0, HEAD_DIM, 1)
        out1 = pltpu.roll(a1, HEAD_DIM, 1) / a1
        o_ref[0, :, hp * LANES:(hp + 1) * LANES] = jnp.where(lane < HEAD_DIM, out0, out1).astype(o_ref.dtype)


def _dsa_prompt_call(q_bf, qit_bf, wit, kg, kt_bf, v_bf, ki_bf, *, topk):
    b, t, _ = q_bf.shape
    nt, tk = kt_bf.shape[1], kt_bf.shape[3]
    tq = min(DSA_TQ, t)
    v4 = v_bf.reshape(b, nt, tk, N_HEADS_A * LANES)
    ki4 = ki_bf.reshape(b, nt, tk, IDX_DIM)
    resident = lambda shape: pl.BlockSpec((1,) + shape, lambda bi, i: (bi, 0, 0, 0), pipeline_mode=pl.Buffered(1))
    return pl.pallas_call(
        functools.partial(_dsa_prompt_kernel, topk=topk),
        grid=(b, t // tq),
        in_specs=[
            pl.BlockSpec((1, tq, WIDTH_A), lambda bi, i: (bi, i, 0)),
            pl.BlockSpec((1, IDX_HEADS * IDX_DIM, tq), lambda bi, i: (bi, 0, i)),
            pl.BlockSpec((1, 8, tq), lambda bi, i: (bi, 0, i)),
            pl.BlockSpec((1, WIDTH_A), lambda bi, i: (0, 0)),
            resident((nt, WIDTH_A, tk)),
            resident((nt, tk, N_HEADS_A * LANES)),
            resident((nt, tk, IDX_DIM)),
        ],
        out_specs=pl.BlockSpec((1, tq, WIDTH_A), lambda bi, i: (bi, i, 0)),
        out_shape=jax.ShapeDtypeStruct((b, t, WIDTH_A), _BF16),
        scratch_shapes=[
            pltpu.VMEM((nt, tk, tq), _F32),
            pltpu.VMEM((nt, tq, tk), _F32),
            pltpu.VMEM((N_HEADS_A, tq, LANES), _F32),
            pltpu.VMEM((N_HEADS_A, tq, LANES), _F32),
        ],
        compiler_params=pltpu.CompilerParams(
            dimension_semantics=("arbitrary", "arbitrary"), vmem_limit_bytes=VMEM_LIMIT),
        name="dsa_prompt",
    )(q_bf, qit_bf, wit, kg, kt_bf, v4, ki4)


def _dsa_sample_kernel(q_ref, qi_ref, wi_ref, kn_ref, vn_ref, kin_ref, ckt_ref, cvt_ref, ckit_ref, o_ref,
                       sct_ref, bias_ref, kt_ref, vt_ref, kit_ref, *, topk):
    t = q_ref.shape[1]
    past = ckt_ref.shape[2]
    s_pad, nq = sct_ref.shape[1], sct_ref.shape[2]
    tail = s_pad - past

    def new_cols(x):
        return jnp.concatenate([x, jnp.zeros((tail - t, x.shape[1]), _F32)], axis=0).T

    kt_ref[:, :past] = ckt_ref[0].astype(_BF16)
    vt_ref[:, :past] = cvt_ref[0].astype(_BF16)
    kit_ref[:, :past] = ckit_ref[0].astype(_BF16)
    kt_ref[:, past:] = new_cols(kn_ref[0]).astype(_BF16)
    vt_ref[:, past:] = new_cols(vn_ref[0]).astype(_BF16)
    kin = jnp.concatenate([kin_ref[0], jnp.zeros((t, LANES - IDX_DIM), _F32)], axis=1)
    kit_ref[:, past:] = new_cols(kin)[:IDX_DIM].astype(_BF16)

    qi = qi_ref[0]
    wi = wi_ref[0]
    qi_stack = jnp.concatenate([qi[:, h * IDX_DIM:(h + 1) * IDX_DIM] for h in range(IDX_HEADS)], axis=0)
    s_idx = _dot(qi_stack, kit_ref[...])
    score = jnp.zeros((t, s_pad), _F32)
    for h in range(IDX_HEADS):
        score = score + wi[:, h:h + 1] * jnp.maximum(s_idx[h * t:(h + 1) * t, :], 0.0)
    key_pos = lax.broadcasted_iota(jnp.int32, (s_pad, nq), 0)
    sc = jnp.where(key_pos < past + t, jnp.concatenate([score] * (nq // t), axis=0).T, -jnp.inf)
    sct_ref[0] = sc
    _scores_to_bias(sct_ref, bias_ref, jnp.max(sc, axis=0, keepdims=True), 1, topk)

    nrow = N_HEADS_A * t
    row_head = lax.broadcasted_iota(jnp.int32, (nrow, WIDTH_A), 0) // t
    col_head = lax.broadcasted_iota(jnp.int32, (nrow, WIDTH_A), 1) // HEAD_DIM
    own = row_head == col_head
    q_rep = jnp.concatenate([q_ref[0]] * N_HEADS_A, axis=0)
    q_bd = jnp.where(own, q_rep, jnp.zeros_like(q_rep))
    bias = bias_ref[0, :t, :]
    s = _dot(q_bd, kt_ref[...]) + jnp.concatenate([bias] * N_HEADS_A, axis=0)
    m = jnp.max(s, axis=1, keepdims=True)
    p = jnp.exp2(s - m)
    l = jnp.sum(p, axis=1, keepdims=True)
    pv = _dot_nt(p.astype(_BF16), vt_ref[...]) / l
    pv = jnp.where(own, pv, 0.0)
    out = pv[:t]
    for h in range(1, N_HEADS_A):
        out = out + pv[h * t:(h + 1) * t]
    o_ref[0] = out.astype(o_ref.dtype)


def _dsa_sample_call(q_bf, qi_bf, wi, k_new, v_new, ki_new, cache_kt, cache_vt, cache_kit, *, topk):
    b, t, _ = q_bf.shape
    past = cache_kt.shape[2]
    s_pad = past + LANES
    assert LANES % t == 0
    row = lambda c: pl.BlockSpec((1, t, c), lambda bi: (bi, 0, 0))
    cache = lambda c: pl.BlockSpec((1, c, past), lambda bi: (bi, 0, 0))
    return pl.pallas_call(
        functools.partial(_dsa_sample_kernel, topk=topk),
        grid=(b,),
        in_specs=[row(WIDTH_A), row(IDX_HEADS * IDX_DIM), row(LANES), row(WIDTH_A), row(WIDTH_A), row(IDX_DIM),
                  cache(WIDTH_A), cache(WIDTH_A), cache(IDX_DIM)],
        out_specs=row(WIDTH_A),
        out_shape=jax.ShapeDtypeStruct((b, t, WIDTH_A), _BF16),
        scratch_shapes=[
            pltpu.VMEM((1, s_pad, LANES), _F32),
            pltpu.VMEM((1, LANES, s_pad), _F32),
            pltpu.VMEM((WIDTH_A, s_pad), _BF16),
            pltpu.VMEM((WIDTH_A, s_pad), _BF16),
            pltpu.VMEM((IDX_DIM, s_pad), _BF16),
        ],
        compiler_params=pltpu.CompilerParams(dimension_semantics=("arbitrary",), vmem_limit_bytes=VMEM_LIMIT),
        name="dsa_sample",
    )(q_bf, qi_bf, wi, k_new, v_new, ki_new, cache_kt, cache_vt, cache_kit)


def _out_kernel(x_ref, at_ref, gm_ref, g1_ref, sh2_ref, sc2_ref, g2_ref, ng_ref, wo_ref, w1_ref, w2_ref, y_ref):
    nb, r, d = x_ref.shape
    m = nb * r
    at = at_ref[...].reshape(m, WIDTH_A)
    gm = gm_ref[...].reshape(m, WIDTH_B)
    y = _dot(at, wo_ref[:WIDTH_A, :]) + _dot(gm, wo_ref[WIDTH_A:, :])
    x1 = x_ref[...] + g1_ref[...] * y.reshape(nb, r, d)
    ms = jnp.mean(x1 * x1, axis=-1, keepdims=True)
    h2 = (x1 * lax.rsqrt(ms + EPS) * ng_ref[...]) * (1.0 + sc2_ref[...]) + sh2_ref[...]
    h2 = h2.reshape(m, d).astype(_BF16)
    dff = w1_ref.shape[1]
    fc = 1024
    ff = jnp.zeros((m, d), _F32)
    for c in range(dff // fc):
        a = jnp.maximum(_dot(h2, w1_ref[:, c * fc:(c + 1) * fc]), 0.0)
        ff = ff + _dot((a * a).astype(_BF16), w2_ref[c * fc:(c + 1) * fc, :])
    y_ref[...] = x1 + g2_ref[...] * ff.reshape(nb, r, d)


def _out_call(x3, attn, gm, mod3, norm_g, w_out, w_ff1, w_ff2, *, nb, r, name):
    nseq, t, d = x3.shape
    const2 = lambda i, j: (0, 0)
    row_spec = lambda c: pl.BlockSpec((nb, r, c), lambda i, j: (i, j, 0))
    mod_spec = lambda col: pl.BlockSpec((nb, 1, d), lambda i, j: (i, 0, col))
    return pl.pallas_call(
        _out_kernel,
        grid=(nseq // nb, t // r),
        in_specs=[row_spec(d), row_spec(WIDTH_A), row_spec(WIDTH_B),
                  mod_spec(2), mod_spec(3), mod_spec(4), mod_spec(5),
                  pl.BlockSpec((1, d), const2),
                  pl.BlockSpec(w_out.shape, const2, pipeline_mode=pl.Buffered(1)),
                  pl.BlockSpec(w_ff1.shape, const2, pipeline_mode=pl.Buffered(1)),
                  pl.BlockSpec(w_ff2.shape, const2, pipeline_mode=pl.Buffered(1))],
        out_specs=row_spec(d),
        out_shape=jax.ShapeDtypeStruct((nseq, t, d), _F32),
        compiler_params=pltpu.CompilerParams(
            dimension_semantics=("arbitrary", "arbitrary"), vmem_limit_bytes=VMEM_LIMIT),
        name=name,
    )(x3, attn, gm, mod3, mod3, mod3, mod3, norm_g, w_out, w_ff1, w_ff2)


def _rope_tables(pos):
    half = HEAD_DIM // 2
    inv_freq = jnp.power(jnp.float32(ROPE_THETA), -jnp.arange(half, dtype=_F32) / half)
    ang = pos.astype(_F32)[:, None] * inv_freq[None, :]
    cos, sin = jnp.cos(ang), jnp.sin(ang)
    reps = LANES // HEAD_DIM
    cos_t = jnp.tile(jnp.concatenate([cos, cos], axis=1), (1, reps))
    sin_t = jnp.tile(jnp.concatenate([-sin, sin], axis=1), (1, reps))
    return cos_t[None], sin_t[None]


def _pad_w_in(w_in):
    d = w_in.shape[0]
    a, i = WIDTH_A, IDX_HEADS * IDX_DIM
    o_ki = 3 * a + i
    o_wi = o_ki + IDX_DIM
    o_u = o_wi + IDX_HEADS
    z = lambda n: jnp.zeros((d, n), w_in.dtype)
    cols = [w_in[:, :o_ki], w_in[:, o_ki:o_wi], z(KI_PAD - IDX_DIM), w_in[:, o_wi:o_u], z(WI_PAD - IDX_HEADS),
            w_in[:, o_u:]]
    return jnp.concatenate(cols, axis=1).astype(_BF16)


def _layer(x_prompt, x_sample, c_prompt, c_sample, cache_k, cache_v, cache_kidx, w_ada, b_ada, norm1_g, norm2_g,
           w_in, q_norm_g, k_norm_g, gmlp_ln_g, gmlp_ln_b, gmlp_ws, gmlp_bs, w_out, w_ff1, w_ff2):
    bp, tp, d = x_prompt.shape
    bs_, ts, _ = x_sample.shape
    past = cache_k.shape[1]

    c_all = jnp.concatenate([c_prompt, c_sample], axis=0)
    rows = -(-c_all.shape[0] // 16) * 16
    c_all = jnp.pad(c_all, ((0, rows - c_all.shape[0]), (0, 0)))
    mod = _ada_call(c_all, w_ada, b_ada[None, :])
    mod_p = mod[:bp, None, :]
    mod_s = mod[bp:bp + bs_, None, :]

    w_in_p = _pad_w_in(w_in)
    w_out_b, w_ff1_b, w_ff2_b = w_out.astype(_BF16), w_ff1.astype(_BF16), w_ff2.astype(_BF16)
    head_of = jnp.arange(WIDTH_A) // HEAD_DIM
    bd = (head_of[:, None] == head_of[None, :]).astype(_BF16)
    qg = jnp.tile(q_norm_g, N_HEADS_A)[None, :]
    kg = jnp.tile(k_norm_g, N_HEADS_A)[None, :]
    lng, lnb = gmlp_ln_g[None, :], gmlp_ln_b[None, :]
    n1, n2 = norm1_g[None, :], norm2_g[None, :]

    rp = min(PROJ_ROWS, tp)
    cos_p, sin_p = _rope_tables(jnp.arange(tp))
    lp = min(tp, GMLP_CHUNK)
    bs_p = jnp.repeat(jnp.transpose(gmlp_bs[:, :lp]), GROUP_DIM_B, axis=1)
    (kp, vp, kip, q_bf, gm_p, kt_bf, v_bf, ki_bf, qit_bf, wit) = _proj_call(
        x_prompt, mod_p, n1, w_in_p, cos_p, sin_p, bd, qg, kg, lng, lnb, gmlp_ws[:, :lp, :lp], bs_p,
        nb=1, r=rp, prompt=True)
    attn_p = _dsa_prompt_call(q_bf, qit_bf, wit, kg, kt_bf, v_bf, ki_bf, topk=min(TOPK_MAX, tp // 4))
    yp = _out_call(x_prompt, attn_p, gm_p, mod_p, n2, w_out_b, w_ff1_b, w_ff2_b, nb=1, r=rp, name="out_prompt")

    nb = min(SAMPLE_GROUP, bs_)
    cos_s, sin_s = _rope_tables(past + jnp.arange(ts))
    ls = min(ts, GMLP_CHUNK)
    assert ls == ts
    eye = jnp.eye(nb, dtype=gmlp_ws.dtype)
    ws_s = jax.vmap(lambda w: jnp.kron(eye, w))(gmlp_ws[:, :ls, :ls])
    bs_s = jnp.tile(jnp.repeat(jnp.transpose(gmlp_bs[:, :ls]), GROUP_DIM_B, axis=1), (nb, 1))
    (ks, vs, kis, qs_bf, gm_s, qis_bf, wi_s, gvs) = _proj_call(
        x_sample, mod_s, n1, w_in_p, cos_s, sin_s, bd, qg, kg, lng, lnb, ws_s, bs_s,
        nb=nb, r=ts, prompt=False)
    cache_kt = jnp.transpose(cache_k, (0, 2, 3, 1)).reshape(bs_, WIDTH_A, past)
    cache_vt = jnp.transpose(cache_v, (0, 2, 3, 1)).reshape(bs_, WIDTH_A, past)
    cache_kit = jnp.transpose(cache_kidx, (0, 2, 1))
    attn_s = _dsa_sample_call(qs_bf, qis_bf, wi_s, ks, vs, kis, cache_kt, cache_vt, cache_kit,
                              topk=min(TOPK_MAX, (past + ts) // 4))
    ys = _out_call(x_sample, attn_s, gm_s, mod_s, n2, w_out_b, w_ff1_b, w_ff2_b, nb=nb, r=ts, name="out_sample")

    heads = lambda a: a.reshape(a.shape[0], a.shape[1], N_HEADS_A, HEAD_DIM)
    heads_t = lambda a: jnp.transpose(a.reshape(a.shape[0], N_HEADS_A, HEAD_DIM, a.shape[2]), (0, 3, 1, 2))
    return yp, ys, heads_t(kp), heads_t(vp), jnp.transpose(kip, (0, 2, 1)), heads(ks), heads(vs), kis, gvs


def kernel(x_prompt, x_sample, c_prompt, c_sample, cache_k, cache_v, cache_kidx, w_ada, b_ada, norm1_g, norm2_g,
           w_in, q_norm_g, k_norm_g, gmlp_ln_g, gmlp_ln_b, gmlp_ws, gmlp_bs, w_out, w_ff1, w_ff2):
    depth = w_ada.shape[0]
    yp, ys = x_prompt, x_sample
    outs = [[] for _ in range(7)]
    for l in range(depth):
        res = _layer(yp, ys, c_prompt, c_sample, cache_k[l], cache_v[l], cache_kidx[l], w_ada[l], b_ada[l],
                     norm1_g[l], norm2_g[l], w_in[l], q_norm_g[l], k_norm_g[l], gmlp_ln_g[l], gmlp_ln_b[l],
                     gmlp_ws[l], gmlp_bs[l], w_out[l], w_ff1[l], w_ff2[l])
        yp, ys = res[0], res[1]
        for acc, leaf in zip(outs, res[2:]):
            acc.append(leaf)
    return (yp, ys) + tuple(jnp.stack(o) for o in outs)
```

```python
import functools

import jax
import jax.numpy as jnp
from jax import lax
from jax.experimental import pallas as pl
from jax.experimental.pallas import tpu as pltpu

N_HEADS_A = 8
HEAD_DIM = 64
WIDTH_A = N_HEADS_A * HEAD_DIM
IDX_HEADS = 4
IDX_DIM = 64
N_GROUPS_B = 4
GROUP_DIM_B = 128
WIDTH_B = N_GROUPS_B * GROUP_DIM_B
GMLP_CHUNK = 128
CHUNK = 64
TOPK_MAX = 256
ROPE_THETA = 10000.0
EPS = 1e-6

LANES = 128
KI_PAD = LANES
WI_PAD = LANES
IN_WIDTH_PADDED = 3 * WIDTH_A + IDX_HEADS * IDX_DIM + KI_PAD + WI_PAD + 2 * WIDTH_B

PROJ_ROWS = 512
DSA_TQ = 256
DSA_TK = PROJ_ROWS
SAMPLE_GROUP = 8
COUNT_ACC_ROWS = 32
VMEM_LIMIT = 56 * 1024 * 1024

MASK_BIAS = -1e30
Q_SCALE = HEAD_DIM ** -0.5 * 1.4426950408889634
SHIFT_LIMIT = 30.0
KEY_NEG_INF = -2139095041
KEY_POS_INF = 2139095040
F32_MIN_NORMAL = 1.1754944e-38
F32_MAX = 3.4028235e38

_F32 = jnp.float32
_BF16 = jnp.bfloat16


def _dot(a, b):
    return jnp.dot(a, b, preferred_element_type=_F32)


def _dot_nt(a, b):
    return lax.dot_general(a, b, (((1,), (1,)), ((), ())), preferred_element_type=_F32)


def _split_bf16(a):
    hi = a.astype(_BF16)
    lo = (a - hi.astype(_F32)).astype(_BF16)
    return hi, lo


def _ada_kernel(c_ref, w_ref, b_ref, o_ref):
    c = c_ref[...]
    s_hi, s_lo = _split_bf16(c * jax.nn.sigmoid(c))
    w_hi, w_lo = _split_bf16(w_ref[...])
    o_ref[...] = _dot(s_hi, w_hi) + _dot(s_lo, w_hi) + _dot(s_hi, w_lo) + b_ref[...]


def _ada_call(c, w_ada, b_ada):
    rows, d = c.shape
    n = w_ada.shape[1]
    tn = 1024
    return pl.pallas_call(
        _ada_kernel,
        grid=(n // tn,),
        in_specs=[
            pl.BlockSpec((rows, d), lambda j: (0, 0)),
            pl.BlockSpec((d, tn), lambda j: (0, j)),
            pl.BlockSpec((1, tn), lambda j: (0, j)),
        ],
        out_specs=pl.BlockSpec((rows, tn), lambda j: (0, j)),
        out_shape=jax.ShapeDtypeStruct((rows, n), _F32),
        compiler_params=pltpu.CompilerParams(dimension_semantics=("arbitrary",), vmem_limit_bytes=VMEM_LIMIT),
        name="ada",
    )(c, w_ada, b_ada)


def _proj_kernel(x_ref, sh_ref, sc_ref, ng_ref, w_ref, cos_ref, sin_ref, bd_ref, qg_ref, kg_ref,
                 lng_ref, lnb_ref, ws_ref, bs_ref, *out_refs, prompt, cb):
    nb, r, d = x_ref.shape
    m = nb * r

    x = x_ref[...]
    ms = jnp.mean(x * x, axis=-1, keepdims=True)
    h = (x * lax.rsqrt(ms + EPS) * ng_ref[...]) * (1.0 + sc_ref[...]) + sh_ref[...]
    h = h.reshape(m, d).astype(_BF16)

    def seg(a, b):
        return _dot(h, w_ref[:, a:b])

    o = 0
    q = seg(o, o + WIDTH_A); o += WIDTH_A
    k = seg(o, o + WIDTH_A); o += WIDTH_A
    v = seg(o, o + WIDTH_A); o += WIDTH_A
    qi = seg(o, o + IDX_HEADS * IDX_DIM); o += IDX_HEADS * IDX_DIM
    ki = seg(o, o + KI_PAD); o += KI_PAD
    wi = seg(o, o + WI_PAD); o += WI_PAD
    u = seg(o, o + WIDTH_B); o += WIDTH_B
    vg = seg(o, o + WIDTH_B)

    bd = bd_ref[...]

    def head_norm(t, g):
        hi, lo = _split_bf16(t * t)
        ss = _dot(hi, bd) + _dot(lo, bd)
        return t * lax.rsqrt(ss * (1.0 / HEAD_DIM) + EPS) * g

    cos1 = jnp.broadcast_to(cos_ref[...], (nb, r, LANES)).reshape(m, LANES)
    sin1 = jnp.broadcast_to(sin_ref[...], (nb, r, LANES)).reshape(m, LANES)

    def rope(t):
        w = t.shape[1]
        reps = w // LANES
        cosw = cos1 if reps == 1 else jnp.concatenate([cos1] * reps, axis=1)
        sinw = sin1 if reps == 1 else jnp.concatenate([sin1] * reps, axis=1)
        lane = lax.broadcasted_iota(jnp.int32, (1, w), 1)
        first_half = (lane % HEAD_DIM) < (HEAD_DIM // 2)
        rot = jnp.where(first_half, pltpu.roll(t, w - HEAD_DIM // 2, 1), pltpu.roll(t, HEAD_DIM // 2, 1))
        return t * cosw + rot * sinw

    qr = rope(head_norm(q, qg_ref[...]))
    kr = rope(head_norm(k, kg_ref[...]))
    qir = rope(qi)
    kir = rope(ki)
    wis = wi * ((IDX_DIM * IDX_HEADS) ** -0.5)

    ug = jax.nn.gelu(u)
    vgg = jax.nn.gelu(vg)
    mu = jnp.mean(vgg, axis=-1, keepdims=True)
    xc = vgg - mu
    var = jnp.mean(xc * xc, axis=-1, keepdims=True)
    vn = xc * lax.rsqrt(var + EPS) * lng_ref[...] + lnb_ref[...]
    vnb = vn.astype(_BF16)
    rowi = lax.broadcasted_iota(jnp.int32, (cb, cb), 0)
    coli = lax.broadcasted_iota(jnp.int32, (cb, cb), 1)
    ws_m = [jnp.where(rowi >= coli, ws_ref[g], 0.0).astype(_BF16) for g in range(N_GROUPS_B)]
    gm_rows = []
    for c in range(m // cb):
        pieces = []
        for g in range(N_GROUPS_B):
            lanes = slice(g * GROUP_DIM_B, (g + 1) * GROUP_DIM_B)
            mixed = _dot(ws_m[g], vnb[c * cb:(c + 1) * cb, lanes]) + bs_ref[:, lanes]
            pieces.append(ug[c * cb:(c + 1) * cb, lanes] * mixed)
        gm_rows.append(jnp.concatenate(pieces, axis=1))
    gm = gm_rows[0] if len(gm_rows) == 1 else jnp.concatenate(gm_rows, axis=0)

    def put(ref, val):
        ref[...] = val.astype(ref.dtype).reshape(ref.shape)

    if prompt:
        k_out, v_out, ki_out, q_bf, gm_out, kt_bf, v_bf, ki_bf, qit_bf, wit_out = out_refs
        krt = kr.T
        put(k_out, krt)
        put(v_out, v.T)
        put(ki_out, kir.T[:IDX_DIM, :])
        put(kt_bf, krt)
        put(ki_bf, kir[:, :IDX_DIM])
        put(qit_bf, qir.T)
        put(wit_out, wis.T[:8, :])
        lane = lax.broadcasted_iota(jnp.int32, (1, LANES), 1)
        pieces = []
        for hd in range(N_HEADS_A):
            src = v[:, (hd // 2) * LANES:(hd // 2 + 1) * LANES]
            if hd % 2:
                src = pltpu.roll(src, HEAD_DIM, 1)
            pieces.append(jnp.where(lane < HEAD_DIM, src, 1.0))
        put(v_bf, jnp.concatenate(pieces, axis=1))
    else:
        k_out, v_out, ki_out, q_bf, gm_out, qi_bf, wi_out, vn_out = out_refs
        put(qi_bf, qir)
        put(wi_out, wis)
        put(vn_out, vn)
        put(k_out, kr)
        put(v_out, v)
        put(ki_out, kir[:, :IDX_DIM])
    put(q_bf, qr * Q_SCALE)
    put(gm_out, gm)


def _proj_call(x3, mod3, norm_g, w_in_p, cos_t, sin_t, bd, qg, kg, lng, lnb, ws, bs_b, *, nb, r, prompt):
    nseq, t, d = x3.shape
    m = nb * r
    cb = ws.shape[1]
    grid = (nseq // nb, t // r)
    const2 = lambda i, j: (0, 0)
    const3 = lambda i, j: (0, 0, 0)
    row_spec = lambda c: pl.BlockSpec((nb, r, c), lambda i, j: (i, j, 0))
    in_specs = [
        row_spec(d),
        pl.BlockSpec((nb, 1, d), lambda i, j: (i, 0, 0)),
        pl.BlockSpec((nb, 1, d), lambda i, j: (i, 0, 1)),
        pl.BlockSpec((1, d), const2),
        pl.BlockSpec(w_in_p.shape, const2, pipeline_mode=pl.Buffered(1)),
        pl.BlockSpec((1, r, LANES), lambda i, j: (0, j, 0)),
        pl.BlockSpec((1, r, LANES), lambda i, j: (0, j, 0)),
        pl.BlockSpec(bd.shape, const2),
        pl.BlockSpec((1, WIDTH_A), const2),
        pl.BlockSpec((1, WIDTH_A), const2),
        pl.BlockSpec((1, WIDTH_B), const2),
        pl.BlockSpec((1, WIDTH_B), const2),
        pl.BlockSpec(ws.shape, const3),
        pl.BlockSpec(bs_b.shape, const2),
    ]
    if prompt:
        leaf = lambda c: jax.ShapeDtypeStruct((nseq, c, t), _F32)
        leaf_spec = lambda c: pl.BlockSpec((nb, c, r), lambda i, j: (i, 0, j))
    else:
        leaf = lambda c: jax.ShapeDtypeStruct((nseq, t, c), _F32)
        leaf_spec = row_spec
    out_shape = [
        leaf(WIDTH_A), leaf(WIDTH_A), leaf(IDX_DIM),
        jax.ShapeDtypeStruct((nseq, t, WIDTH_A), _BF16),
        jax.ShapeDtypeStruct((nseq, t, WIDTH_B), _BF16),
    ]
    out_specs = [leaf_spec(WIDTH_A), leaf_spec(WIDTH_A), leaf_spec(IDX_DIM), row_spec(WIDTH_A), row_spec(WIDTH_B)]
    if prompt:
        assert nb == 1
        nt = t // r
        out_shape += [
            jax.ShapeDtypeStruct((nseq, nt, WIDTH_A, r), _BF16),
            jax.ShapeDtypeStruct((nseq, t, N_HEADS_A * LANES), _BF16),
            jax.ShapeDtypeStruct((nseq, t, IDX_DIM), _BF16),
            jax.ShapeDtypeStruct((nseq, IDX_HEADS * IDX_DIM, t), _BF16),
            jax.ShapeDtypeStruct((nseq, 8, t), _F32),
        ]
        out_specs += [
            pl.BlockSpec((1, 1, WIDTH_A, r), lambda i, j: (i, j, 0, 0)),
            row_spec(N_HEADS_A * LANES),
            row_spec(IDX_DIM),
            pl.BlockSpec((1, IDX_HEADS * IDX_DIM, r), lambda i, j: (i, 0, j)),
            pl.BlockSpec((1, 8, r), lambda i, j: (i, 0, j)),
        ]
    else:
        out_shape += [
            jax.ShapeDtypeStruct((nseq, t, IDX_HEADS * IDX_DIM), _BF16),
            jax.ShapeDtypeStruct((nseq, t, LANES), _F32),
            jax.ShapeDtypeStruct((nseq, t, WIDTH_B), _F32),
        ]
        out_specs += [row_spec(IDX_HEADS * IDX_DIM), row_spec(LANES), row_spec(WIDTH_B)]
    return pl.pallas_call(
        functools.partial(_proj_kernel, prompt=prompt, cb=cb),
        grid=grid,
        in_specs=in_specs,
        out_specs=out_specs,
        out_shape=out_shape,
        compiler_params=pltpu.CompilerParams(
            dimension_semantics=("arbitrary", "arbitrary"), vmem_limit_bytes=VMEM_LIMIT),
        name="proj_prompt" if prompt else "proj_sample",
    )(x3, mod3, mod3, norm_g, w_in_p, cos_t, sin_t, bd, qg, kg, lng, lnb, ws, bs_b)


def _fori_by_pairs(n, body, init):
    if isinstance(n, int):
        return lax.fori_loop(0, n, body, init)
    carry = lax.fori_loop(0, n // 2, lambda p, c: body(2 * p + 1, body(2 * p, c)), init)
    return lax.cond(n % 2 == 1, lambda c: body(n - 1, c), lambda c: c, carry)


def _key_to_f32(key):
    bits = jnp.where(key >= 0, key, key ^ jnp.int32(0x7FFFFFFF))
    return lax.bitcast_convert_type(bits, _F32)


def _f32_to_key(x):
    bits = lax.bitcast_convert_type(x, jnp.int32)
    return jnp.where(bits >= 0, bits, bits ^ jnp.int32(0x7FFFFFFF))


def _count_keys(sct_ref, nkb, preds):
    _, tk, nq = sct_ref.shape
    g = COUNT_ACC_ROWS

    def body(j, accs):
        x = sct_ref[j]
        return tuple(acc + jnp.where(pred(x), 1.0, 0.0).reshape(tk // g, g, nq).sum(axis=0)
                     for acc, pred in zip(accs, preds))

    accs = _fori_by_pairs(nkb, body, tuple(jnp.zeros((g, nq), _F32) for _ in preds))
    return [jnp.sum(acc, axis=0, keepdims=True) for acc in accs]


def _scores_to_bias(sct_ref, bias_ref, score_max, nkb, topk):
    nblk, tk, nq = sct_ref.shape
    kf = jnp.float32(topk)
    total = jnp.asarray(nkb * tk).astype(_F32)

    def mid_of(lo, hi):
        return lo + lax.shift_right_logical(hi - lo, 1)

    def unsettled_rows(lo, cnt_lo, mid):
        return jnp.max(jnp.where((cnt_lo != kf) & (mid != lo), 1, 0))

    def cond(st):
        return (st[5] > 0) & (st[6] < 34)

    def body(st):
        return step(step(st))

    def step(st):
        lo, hi, cnt_lo, cnt_hi, mid, _, it = st
        thr_mid = _key_to_f32(mid)
        cnt = _count_keys(sct_ref, nkb, [lambda x: x >= thr_mid])[0]
        ge = cnt >= kf
        lo = jnp.where(ge, mid, lo)
        cnt_lo = jnp.where(ge, cnt, cnt_lo)
        hi = jnp.where(ge, hi, mid)
        cnt_hi = jnp.where(ge, cnt_hi, cnt)
        mid = mid_of(lo, hi)
        return lo, hi, cnt_lo, cnt_hi, mid, unsettled_rows(lo, cnt_lo, mid), it + 1

    key_p = _f32_to_key(score_max * 0.125)
    key_t = _f32_to_key(score_max * 2.0)
    probe_p, probe_t = _key_to_f32(key_p), _key_to_f32(key_t)
    c_ge0, c_gt0, c_p, c_t = _count_keys(
        sct_ref, nkb, [lambda x: x >= 0.0, lambda x: x > 0.0, lambda x: x >= probe_p, lambda x: x >= probe_t])
    is_pos = c_gt0 >= kf
    is_zero = c_ge0 >= kf
    p_low = (c_p >= kf) & (key_p > 1)
    p_high = (c_p < kf) & (key_p > 1)
    t_high = (c_t < kf) & (key_t > key_p) & (key_t > 1)
    lo = jnp.where(is_pos, jnp.where(p_low, key_p, 1), jnp.where(is_zero, 0, KEY_NEG_INF))
    cnt_lo = jnp.where(is_pos, jnp.where(p_low, c_p, c_gt0), jnp.where(is_zero, c_ge0, total))
    hi = jnp.where(is_pos, jnp.where(p_high, key_p, jnp.where(t_high, key_t, KEY_POS_INF)),
                   jnp.where(is_zero, 1, -1))
    cnt_hi = jnp.where(is_pos, jnp.where(p_high, c_p, jnp.where(t_high, c_t, 0.0)),
                       jnp.where(is_zero, c_gt0, c_ge0))
    mid = mid_of(lo, hi)
    init = (lo, hi, cnt_lo, cnt_hi, mid, unsettled_rows(lo, cnt_lo, mid), jnp.int32(0))
    lo, hi, cnt_lo, cnt_hi = lax.while_loop(cond, body, init)[:4]

    thr = _key_to_f32(lo)
    thr = jnp.where(jnp.abs(thr) < F32_MIN_NORMAL, 0.0, thr)
    thr = jnp.maximum(thr, -F32_MAX)
    tie = (cnt_lo > kf) & (lo > KEY_NEG_INF)
    need = jnp.where(lo == 1, -1.0, jnp.where(tie, kf - cnt_hi, 1e9))

    rowi = lax.broadcasted_iota(jnp.int32, (LANES, LANES), 0)
    coli = lax.broadcasted_iota(jnp.int32, (LANES, LANES), 1)
    tri = jnp.where(rowi >= coli, 1.0, 0.0).astype(_BF16)

    def fill(j, base):
        for c in range(tk // LANES):
            x = sct_ref[j, c * LANES:(c + 1) * LANES, :]
            eq = x == thr
            rank = base + _dot(tri, jnp.where(eq, 1.0, 0.0).astype(_BF16))
            tied = jnp.where(eq, jnp.where(rank <= need, 0.0, MASK_BIAS), MASK_BIAS)
            bias_ref[j, :, c * LANES:(c + 1) * LANES] = jnp.where(x > thr, 0.0, tied).T
            base = rank[LANES - 1:LANES, :]
        return base

    _fori_by_pairs(nkb, fill, jnp.zeros((1, nq), _F32))


def _dsa_prompt_kernel(q_ref, qit_ref, wit_ref, kg_ref, kt_ref, v_ref, ki_ref, o_ref, sct_ref, bias_ref, m_ref,
                       acc_ref, *, topk):
    i = pl.program_id(1)
    tq = q_ref.shape[1]
    tk = ki_ref.shape[2]
    nkb = ((i + 1) * tq + tk - 1) // tk

    wit = wit_ref[0]
    q_pos = i * tq + lax.broadcasted_iota(jnp.int32, (1, tq), 1)
    vis_lim = (q_pos // CHUNK + 1) * CHUNK
    key_in_block = lax.broadcasted_iota(jnp.int32, (tk, tq), 0)

    g = COUNT_ACC_ROWS

    def score_body(j, run_max):
        ki = ki_ref[0, j]
        s_all = [_dot(ki, qit_ref[0, h * IDX_DIM:(h + 1) * IDX_DIM, :]) for h in range(IDX_HEADS)]
        acc = jnp.zeros((tk, tq), _F32)
        for h in range(IDX_HEADS):
            acc = acc + wit[h:h + 1, :] * jnp.maximum(s_all[h], 0.0)
        sc = jnp.where((j * tk + key_in_block) < vis_lim, acc, -jnp.inf)
        sct_ref[j] = sc
        return jnp.maximum(run_max, sc.reshape(tk // g, g, tq).max(axis=0))

    run_max = _fori_by_pairs(nkb, score_body, jnp.full((g, tq), -jnp.inf, _F32))
    _scores_to_bias(sct_ref, bias_ref, jnp.max(run_max, axis=0, keepdims=True), nkb, topk)

    reps = tk // LANES
    q_heads = [q_ref[0, :, h * HEAD_DIM:(h + 1) * HEAD_DIM] for h in range(N_HEADS_A)]

    def logits(j, h):
        return _dot(q_heads[h], kt_ref[0, j, h * HEAD_DIM:(h + 1) * HEAD_DIM, :]) + bias_ref[j]

    k_norm = jnp.max(jnp.abs(kg_ref[...])) * (HEAD_DIM ** 0.5)
    shift_max = jnp.float32(0.0)
    for h in range(N_HEADS_A):
        qf = q_heads[h].astype(_F32)
        bound = jnp.sqrt(jnp.sum(qf * qf, axis=1, keepdims=True)) * k_norm
        m_ref[h] = jnp.broadcast_to(bound, (tq, LANES))
        shift_max = jnp.maximum(shift_max, jnp.max(bound))

    @pl.when(shift_max > SHIFT_LIMIT)
    def _():
        m_ref[...] = jnp.full(m_ref.shape, -jnp.inf, _F32)

        def max_body(j, carry):
            for h in range(N_HEADS_A):
                s = logits(j, h)
                part = s[:, :LANES]
                for c in range(1, reps):
                    part = jnp.maximum(part, s[:, c * LANES:(c + 1) * LANES])
                m_ref[h] = jnp.maximum(m_ref[h], part)
            return carry

        lax.fori_loop(0, nkb, max_body, 0)
        for h in range(N_HEADS_A):
            m_ref[h] = jnp.broadcast_to(jnp.max(m_ref[h], axis=1, keepdims=True), (tq, LANES))

    acc_ref[...] = jnp.zeros(acc_ref.shape, _F32)

    def att_body(j, carry):
        s_all = [logits(j, h) for h in range(N_HEADS_A)]
        for h in range(N_HEADS_A):
            p = jnp.exp2(s_all[h] - jnp.concatenate([m_ref[h]] * reps, axis=1))
            acc_ref[h] += _dot(p.astype(_BF16), v_ref[0, j, :, h * LANES:(h + 1) * LANES])
        return carry

    _fori_by_pairs(nkb, att_body, 0)
    lane = lax.broadcasted_iota(jnp.int32, (1, LANES), 1)
    for hp in range(N_HEADS_A // 2):
        a0, a1 = acc_ref[2 * hp], acc_ref[2 * hp + 1]
        out0 = a0 / pltpu.roll(a0, HEAD_DIM, 1)
        out1 = pltpu.roll(a1, HEAD_DIM, 1) / a1
        o_ref[0, :, hp * LANES:(hp + 1) * LANES] = jnp.where(lane < HEAD_DIM, out0, out1).astype(o_ref.dtype)


def _dsa_prompt_call(q_bf, qit_bf, wit, kg, kt_bf, v_bf, ki_bf, *, topk):
    b, t, _ = q_bf.shape
    nt, tk = kt_bf.shape[1], kt_bf.shape[3]
    tq = min(DSA_TQ, t)
    v4 = v_bf.reshape(b, nt, tk, N_HEADS_A * LANES)
    ki4 = ki_bf.reshape(b, nt, tk, IDX_DIM)
    resident = lambda shape: pl.BlockSpec((1,) + shape, lambda bi, i: (bi, 0, 0, 0), pipeline_mode=pl.Buffered(1))
    return pl.pallas_call(
        functools.partial(_dsa_prompt_kernel, topk=topk),
        grid=(b, t // tq),
        in_specs=[
            pl.BlockSpec((1, tq, WIDTH_A), lambda bi, i: (bi, i, 0)),
            pl.BlockSpec((1, IDX_HEADS * IDX_DIM, tq), lambda bi, i: (bi, 0, i)),
            pl.BlockSpec((1, 8, tq), lambda bi, i: (bi, 0, i)),
            pl.BlockSpec((1, WIDTH_A), lambda bi, i: (0, 0)),
            resident((nt, WIDTH_A, tk)),
            resident((nt, tk, N_HEADS_A * LANES)),
            resident((nt, tk, IDX_DIM)),
        ],
        out_specs=pl.BlockSpec((1, tq, WIDTH_A), lambda bi, i: (bi, i, 0)),
        out_shape=jax.ShapeDtypeStruct((b, t, WIDTH_A), _BF16),
        scratch_shapes=[
            pltpu.VMEM((nt, tk, tq), _F32),
            pltpu.VMEM((nt, tq, tk), _F32),
            pltpu.VMEM((N_HEADS_A, tq, LANES), _F32),
            pltpu.VMEM((N_HEADS_A, tq, LANES), _F32),
        ],
        compiler_params=pltpu.CompilerParams(
            dimension_semantics=("arbitrary", "arbitrary"), vmem_limit_bytes=VMEM_LIMIT),
        name="dsa_prompt",
    )(q_bf, qit_bf, wit, kg, kt_bf, v4, ki4)


def _dsa_sample_kernel(q_ref, qi_ref, wi_ref, kn_ref, vn_ref, kin_ref, ckt_ref, cvt_ref, ckit_ref, o_ref,
                       sct_ref, bias_ref, kt_ref, vt_ref, kit_ref, *, topk):
    t = q_ref.shape[1]
    past = ckt_ref.shape[2]
    s_pad, nq = sct_ref.shape[1], sct_ref.shape[2]
    tail = s_pad - past

    def new_cols(x):
        return jnp.concatenate([x, jnp.zeros((tail - t, x.shape[1]), _F32)], axis=0).T

    kt_ref[:, :past] = ckt_ref[0].astype(_BF16)
    vt_ref[:, :past] = cvt_ref[0].astype(_BF16)
    kit_ref[:, :past] = ckit_ref[0].astype(_BF16)
    kt_ref[:, past:] = new_cols(kn_ref[0]).astype(_BF16)
    vt_ref[:, past:] = new_cols(vn_ref[0]).astype(_BF16)
    kin = jnp.concatenate([kin_ref[0], jnp.zeros((t, LANES - IDX_DIM), _F32)], axis=1)
    kit_ref[:, past:] = new_cols(kin)[:IDX_DIM].astype(_BF16)

    qi = qi_ref[0]
    wi = wi_ref[0]
    qi_stack = jnp.concatenate([qi[:, h * IDX_DIM:(h + 1) * IDX_DIM] for h in range(IDX_HEADS)], axis=0)
    s_idx = _dot(qi_stack, kit_ref[...])
    score = jnp.zeros((t, s_pad), _F32)
    for h in range(IDX_HEADS):
        score = score + wi[:, h:h + 1] * jnp.maximum(s_idx[h * t:(h + 1) * t, :], 0.0)
    key_pos = lax.broadcasted_iota(jnp.int32, (s_pad, nq), 0)
    sc = jnp.where(key_pos < past + t, jnp.concatenate([score] * (nq // t), axis=0).T, -jnp.inf)
    sct_ref[0] = sc
    _scores_to_bias(sct_ref, bias_ref, jnp.max(sc, axis=0, keepdims=True), 1, topk)

    nrow = N_HEADS_A * t
    row_head = lax.broadcasted_iota(jnp.int32, (nrow, WIDTH_A), 0) // t
    col_head = lax.broadcasted_iota(jnp.int32, (nrow, WIDTH_A), 1) // HEAD_DIM
    own = row_head == col_head
    q_rep = jnp.concatenate([q_ref[0]] * N_HEADS_A, axis=0)
    q_bd = jnp.where(own, q_rep, jnp.zeros_like(q_rep))
    bias = bias_ref[0, :t, :]
    s = _dot(q_bd, kt_ref[...]) + jnp.concatenate([bias] * N_HEADS_A, axis=0)
    m = jnp.max(s, axis=1, keepdims=True)
    p = jnp.exp2(s - m)
    l = jnp.sum(p, axis=1, keepdims=True)
    pv = _dot_nt(p.astype(_BF16), vt_ref[...]) / l
    pv = jnp.where(own, pv, 0.0)
    out = pv[:t]
    for h in range(1, N_HEADS_A):
        out = out + pv[h * t:(h + 1) * t]
    o_ref[0] = out.astype(o_ref.dtype)


def _dsa_sample_call(q_bf, qi_bf, wi, k_new, v_new, ki_new, cache_kt, cache_vt, cache_kit, *, topk):
    b, t, _ = q_bf.shape
    past = cache_kt.shape[2]
    s_pad = past + LANES
    assert LANES % t == 0
    row = lambda c: pl.BlockSpec((1, t, c), lambda bi: (bi, 0, 0))
    cache = lambda c: pl.BlockSpec((1, c, past), lambda bi: (bi, 0, 0))
    return pl.pallas_call(
        functools.partial(_dsa_sample_kernel, topk=topk),
        grid=(b,),
        in_specs=[row(WIDTH_A), row(IDX_HEADS * IDX_DIM), row(LANES), row(WIDTH_A), row(WIDTH_A), row(IDX_DIM),
                  cache(WIDTH_A), cache(WIDTH_A), cache(IDX_DIM)],
        out_specs=row(WIDTH_A),
        out_shape=jax.ShapeDtypeStruct((b, t, WIDTH_A), _BF16),
        scratch_shapes=[
            pltpu.VMEM((1, s_pad, LANES), _F32),
            pltpu.VMEM((1, LANES, s_pad), _F32),
            pltpu.VMEM((WIDTH_A, s_pad), _BF16),
            pltpu.VMEM((WIDTH_A, s_pad), _BF16),
            pltpu.VMEM((IDX_DIM, s_pad), _BF16),
        ],
        compiler_params=pltpu.CompilerParams(dimension_semantics=("arbitrary",), vmem_limit_bytes=VMEM_LIMIT),
        name="dsa_sample",
    )(q_bf, qi_bf, wi, k_new, v_new, ki_new, cache_kt, cache_vt, cache_kit)


def _out_kernel(x_ref, at_ref, gm_ref, g1_ref, sh2_ref, sc2_ref, g2_ref, ng_ref, wo_ref, w1_ref, w2_ref, y_ref):
    nb, r, d = x_ref.shape
    m = nb * r
    at = at_ref[...].reshape(m, WIDTH_A)
    gm = gm_ref[...].reshape(m, WIDTH_B)
    y = _dot(at, wo_ref[:WIDTH_A, :]) + _dot(gm, wo_ref[WIDTH_A:, :])
    x1 = x_ref[...] + g1_ref[...] * y.reshape(nb, r, d)
    ms = jnp.mean(x1 * x1, axis=-1, keepdims=True)
    h2 = (x1 * lax.rsqrt(ms + EPS) * ng_ref[...]) * (1.0 + sc2_ref[...]) + sh2_ref[...]
    h2 = h2.reshape(m, d).astype(_BF16)
    dff = w1_ref.shape[1]
    fc = 1024
    ff = jnp.zeros((m, d), _F32)
    for c in range(dff // fc):
        a = jnp.maximum(_dot(h2, w1_ref[:, c * fc:(c + 1) * fc]), 0.0)
        ff = ff + _dot((a * a).astype(_BF16), w2_ref[c * fc:(c + 1) * fc, :])
    y_ref[...] = x1 + g2_ref[...] * ff.reshape(nb, r, d)


def _out_call(x3, attn, gm, mod3, norm_g, w_out, w_ff1, w_ff2, *, nb, r, name):
    nseq, t, d = x3.shape
    const2 = lambda i, j: (0, 0)
    row_spec = lambda c: pl.BlockSpec((nb, r, c), lambda i, j: (i, j, 0))
    mod_spec = lambda col: pl.BlockSpec((nb, 1, d), lambda i, j: (i, 0, col))
    return pl.pallas_call(
        _out_kernel,
        grid=(nseq // nb, t // r),
        in_specs=[row_spec(d), row_spec(WIDTH_A), row_spec(WIDTH_B),
                  mod_spec(2), mod_spec(3), mod_spec(4), mod_spec(5),
                  pl.BlockSpec((1, d), const2),
                  pl.BlockSpec(w_out.shape, const2, pipeline_mode=pl.Buffered(1)),
                  pl.BlockSpec(w_ff1.shape, const2, pipeline_mode=pl.Buffered(1)),
                  pl.BlockSpec(w_ff2.shape, const2, pipeline_mode=pl.Buffered(1))],
        out_specs=row_spec(d),
        out_shape=jax.ShapeDtypeStruct((nseq, t, d), _F32),
        compiler_params=pltpu.CompilerParams(
            dimension_semantics=("arbitrary", "arbitrary"), vmem_limit_bytes=VMEM_LIMIT),
        name=name,
    )(x3, attn, gm, mod3, mod3, mod3, mod3, norm_g, w_out, w_ff1, w_ff2)


def _rope_tables(pos):
    half = HEAD_DIM // 2
    inv_freq = jnp.power(jnp.float32(ROPE_THETA), -jnp.arange(half, dtype=_F32) / half)
    ang = pos.astype(_F32)[:, None] * inv_freq[None, :]
    cos, sin = jnp.cos(ang), jnp.sin(ang)
    reps = LANES // HEAD_DIM
    cos_t = jnp.tile(jnp.concatenate([cos, cos], axis=1), (1, reps))
    sin_t = jnp.tile(jnp.concatenate([-sin, sin], axis=1), (1, reps))
    return cos_t[None], sin_t[None]


def _pad_w_in(w_in):
    d = w_in.shape[0]
    a, i = WIDTH_A, IDX_HEADS * IDX_DIM
    o_ki = 3 * a + i
    o_wi = o_ki + IDX_DIM
    o_u = o_wi + IDX_HEADS
    z = lambda n: jnp.zeros((d, n), w_in.dtype)
    cols = [w_in[:, :o_ki], w_in[:, o_ki:o_wi], z(KI_PAD - IDX_DIM), w_in[:, o_wi:o_u], z(WI_PAD - IDX_HEADS),
            w_in[:, o_u:]]
    return jnp.concatenate(cols, axis=1).astype(_BF16)


def _layer(x_prompt, x_sample, c_prompt, c_sample, cache_k, cache_v, cache_kidx, w_ada, b_ada, norm1_g, norm2_g,
           w_in, q_norm_g, k_norm_g, gmlp_ln_g, gmlp_ln_b, gmlp_ws, gmlp_bs, w_out, w_ff1, w_ff2):
    bp, tp, d = x_prompt.shape
    bs_, ts, _ = x_sample.shape
    past = cache_k.shape[1]

    c_all = jnp.concatenate([c_prompt, c_sample], axis=0)
    rows = -(-c_all.shape[0] // 16) * 16
    c_all = jnp.pad(c_all, ((0, rows - c_all.shape[0]), (0, 0)))
    mod = _ada_call(c_all, w_ada, b_ada[None, :])
    mod_p = mod[:bp, None, :]
    mod_s = mod[bp:bp + bs_, None, :]

    w_in_p = _pad_w_in(w_in)
    w_out_b, w_ff1_b, w_ff2_b = w_out.astype(_BF16), w_ff1.astype(_BF16), w_ff2.astype(_BF16)
    head_of = jnp.arange(WIDTH_A) // HEAD_DIM
    bd = (head_of[:, None] == head_of[None, :]).astype(_BF16)
    qg = jnp.tile(q_norm_g, N_HEADS_A)[None, :]
    kg = jnp.tile(k_norm_g, N_HEADS_A)[None, :]
    lng, lnb = gmlp_ln_g[None, :], gmlp_ln_b[None, :]
    n1, n2 = norm1_g[None, :], norm2_g[None, :]

    rp = min(PROJ_ROWS, tp)
    cos_p, sin_p = _rope_tables(jnp.arange(tp))
    lp = min(tp, GMLP_CHUNK)
    bs_p = jnp.repeat(jnp.transpose(gmlp_bs[:, :lp]), GROUP_DIM_B, axis=1)
    (kp, vp, kip, q_bf, gm_p, kt_bf, v_bf, ki_bf, qit_bf, wit) = _proj_call(
        x_prompt, mod_p, n1, w_in_p, cos_p, sin_p, bd, qg, kg, lng, lnb, gmlp_ws[:, :lp, :lp], bs_p,
        nb=1, r=rp, prompt=True)
    attn_p = _dsa_prompt_call(q_bf, qit_bf, wit, kg, kt_bf, v_bf, ki_bf, topk=min(TOPK_MAX, tp // 4))
    yp = _out_call(x_prompt, attn_p, gm_p, mod_p, n2, w_out_b, w_ff1_b, w_ff2_b, nb=1, r=rp, name="out_prompt")

    nb = min(SAMPLE_GROUP, bs_)
    cos_s, sin_s = _rope_tables(past + jnp.arange(ts))
    ls = min(ts, GMLP_CHUNK)
    assert ls == ts
    eye = jnp.eye(nb, dtype=gmlp_ws.dtype)
    ws_s = jax.vmap(lambda w: jnp.kron(eye, w))(gmlp_ws[:, :ls, :ls])
    bs_s = jnp.tile(jnp.repeat(jnp.transpose(gmlp_bs[:, :ls]), GROUP_DIM_B, axis=1), (nb, 1))
    (ks, vs, kis, qs_bf, gm_s, qis_bf, wi_s, gvs) = _proj_call(
        x_sample, mod_s, n1, w_in_p, cos_s, sin_s, bd, qg, kg, lng, lnb, ws_s, bs_s,
        nb=nb, r=ts, prompt=False)
    cache_kt = jnp.transpose(cache_k, (0, 2, 3, 1)).reshape(bs_, WIDTH_A, past)
    cache_vt = jnp.transpose(cache_v, (0, 2, 3, 1)).reshape(bs_, WIDTH_A, past)
    cache_kit = jnp.transpose(cache_kidx, (0, 2, 1))
    attn_s = _dsa_sample_call(qs_bf, qis_bf, wi_s, ks, vs, kis, cache_kt, cache_vt, cache_kit,
                              topk=min(TOPK_MAX, (past + ts) // 4))
    ys = _out_call(x_sample, attn_s, gm_s, mod_s, n2, w_out_b, w_ff1_b, w_ff2_b, nb=nb, r=ts, name="out_sample")

    heads = lambda a: a.reshape(a.shape[0], a.shape[1], N_HEADS_A, HEAD_DIM)
    heads_t = lambda a: jnp.transpose(a.reshape(a.shape[0], N_HEADS_A, HEAD_DIM, a.shape[2]), (0, 3, 1, 2))
    return yp, ys, heads_t(kp), heads_t(vp), jnp.transpose(kip, (0, 2, 1)), heads(ks), heads(vs), kis, gvs


def kernel(x_prompt, x_sample, c_prompt, c_sample, cache_k, cache_v, cache_kidx, w_ada, b_ada, norm1_g, norm2_g,
           w_in, q_norm_g, k_norm_g, gmlp_ln_g, gmlp_ln_b, gmlp_ws, gmlp_bs, w_out, w_ff1, w_ff2):
    depth = w_ada.shape[0]
    yp, ys = x_prompt, x_sample
    outs = [[] for _ in range(7)]
    for l in range(depth):
        res = _layer(yp, ys, c_prompt, c_sample, cache_k[l], cache_v[l], cache_kidx[l], w_ada[l], b_ada[l],
                     norm1_g[l], norm2_g[l], w_in[l], q_norm_g[l], k_norm_g[l], gmlp_ln_g[l], gmlp_ln_b[l],
                     gmlp_ws[l], gmlp_bs[l], w_out[l], w_ff1[l], w_ff2[l])
        yp, ys = res[0], res[1]
        for acc, leaf in zip(outs, res[2:]):
            acc.append(leaf)
    return (yp, ys) + tuple(jnp.stack(o) for o in outs)
```

```python
import functools

import jax
import jax.numpy as jnp
from jax import lax
from jax.experimental import pallas as pl
from jax.experimental.pallas import tpu as pltpu

N_HEADS_A = 8
HEAD_DIM = 64
WIDTH_A = N_HEADS_A * HEAD_DIM
IDX_HEADS = 4
IDX_DIM = 64
N_GROUPS_B = 4
GROUP_DIM_B = 128
WIDTH_B = N_GROUPS_B * GROUP_DIM_B
GMLP_CHUNK = 128
CHUNK = 64
TOPK_MAX = 256
ROPE_THETA = 10000.0
EPS = 1e-6

LANES = 128
KI_PAD = LANES
WI_PAD = LANES
IN_WIDTH_PADDED = 3 * WIDTH_A + IDX_HEADS * IDX_DIM + KI_PAD + WI_PAD + 2 * WIDTH_B

PROJ_ROWS = 512
DSA_TQ = 256
DSA_TK = PROJ_ROWS
SAMPLE_GROUP = 8
COUNT_ACC_ROWS = 32
VMEM_LIMIT = 56 * 1024 * 1024

MASK_BIAS = -1e30
Q_SCALE = HEAD_DIM ** -0.5 * 1.4426950408889634
SHIFT_LIMIT = 30.0
KEY_NEG_INF = -2139095041
KEY_POS_INF = 2139095040
F32_MIN_NORMAL = 1.1754944e-38
F32_MAX = 3.4028235e38

_F32 = jnp.float32
_BF16 = jnp.bfloat16


def _dot(a, b):
    return jnp.dot(a, b, preferred_element_type=_F32)


def _dot_nt(a, b):
    return lax.dot_general(a, b, (((1,), (1,)), ((), ())), preferred_element_type=_F32)


def _split_bf16(a):
    hi = a.astype(_BF16)
    lo = (a - hi.astype(_F32)).astype(_BF16)
    return hi, lo


def _ada_kernel(c_ref, w_ref, b_ref, o_ref):
    c = c_ref[...]
    s_hi, s_lo = _split_bf16(c * jax.nn.sigmoid(c))
    w_hi, w_lo = _split_bf16(w_ref[...])
    o_ref[...] = _dot(s_hi, w_hi) + _dot(s_lo, w_hi) + _dot(s_hi, w_lo) + b_ref[...]


def _ada_call(c, w_ada, b_ada):
    rows, d = c.shape
    n = w_ada.shape[1]
    tn = 1024
    return pl.pallas_call(
        _ada_kernel,
        grid=(n // tn,),
        in_specs=[
            pl.BlockSpec((rows, d), lambda j: (0, 0)),
            pl.BlockSpec((d, tn), lambda j: (0, j)),
            pl.BlockSpec((1, tn), lambda j: (0, j)),
        ],
        out_specs=pl.BlockSpec((rows, tn), lambda j: (0, j)),
        out_shape=jax.ShapeDtypeStruct((rows, n), _F32),
        compiler_params=pltpu.CompilerParams(dimension_semantics=("arbitrary",), vmem_limit_bytes=VMEM_LIMIT),
        name="ada",
    )(c, w_ada, b_ada)


def _proj_kernel(x_ref, sh_ref, sc_ref, ng_ref, w_ref, cos_ref, sin_ref, bd_ref, qg_ref, kg_ref,
                 lng_ref, lnb_ref, ws_ref, bs_ref, *out_refs, prompt, cb):
    nb, r, d = x_ref.shape
    m = nb * r

    x = x_ref[...]
    ms = jnp.mean(x * x, axis=-1, keepdims=True)
    h = (x * lax.rsqrt(ms + EPS) * ng_ref[...]) * (1.0 + sc_ref[...]) + sh_ref[...]
    h = h.reshape(m, d).astype(_BF16)

    def seg(a, b):
        return _dot(h, w_ref[:, a:b])

    o = 0
    q = seg(o, o + WIDTH_A); o += WIDTH_A
    k = seg(o, o + WIDTH_A); o += WIDTH_A
    v = seg(o, o + WIDTH_A); o += WIDTH_A
    qi = seg(o, o + IDX_HEADS * IDX_DIM); o += IDX_HEADS * IDX_DIM
    kiwi = seg(o, o + KI_PAD + WI_PAD); o += KI_PAD + WI_PAD
    ki, wi = kiwi[:, :KI_PAD], kiwi[:, KI_PAD:]
    u = seg(o, o + WIDTH_B); o += WIDTH_B
    vg = seg(o, o + WIDTH_B)

    bd = bd_ref[...]

    def head_norm(t, g):
        hi, lo = _split_bf16(t * t)
        ss = _dot(hi, bd) + _dot(lo, bd)
        return t * lax.rsqrt(ss * (1.0 / HEAD_DIM) + EPS) * g

    cos1 = jnp.broadcast_to(cos_ref[...], (nb, r, LANES)).reshape(m, LANES)
    sin1 = jnp.broadcast_to(sin_ref[...], (nb, r, LANES)).reshape(m, LANES)

    def rope(t):
        w = t.shape[1]
        reps = w // LANES
        cosw = cos1 if reps == 1 else jnp.concatenate([cos1] * reps, axis=1)
        sinw = sin1 if reps == 1 else jnp.concatenate([sin1] * reps, axis=1)
        lane = lax.broadcasted_iota(jnp.int32, (1, w), 1)
        first_half = (lane % HEAD_DIM) < (HEAD_DIM // 2)
        rot = jnp.where(first_half, pltpu.roll(t, w - HEAD_DIM // 2, 1), pltpu.roll(t, HEAD_DIM // 2, 1))
        return t * cosw + rot * sinw

    qr = rope(head_norm(q, qg_ref[...]))
    kr = rope(head_norm(k, kg_ref[...]))
    qir = rope(qi)
    kir = rope(ki)
    wis = wi * ((IDX_DIM * IDX_HEADS) ** -0.5)

    ug = jax.nn.gelu(u)
    vgg = jax.nn.gelu(vg)
    mu = jnp.mean(vgg, axis=-1, keepdims=True)
    xc = vgg - mu
    var = jnp.mean(xc * xc, axis=-1, keepdims=True)
    vn = xc * lax.rsqrt(var + EPS) * lng_ref[...] + lnb_ref[...]
    vnb = vn.astype(_BF16)
    rowi = lax.broadcasted_iota(jnp.int32, (cb, cb), 0)
    coli = lax.broadcasted_iota(jnp.int32, (cb, cb), 1)
    ws_m = [jnp.where(rowi >= coli, ws_ref[g], 0.0).astype(_BF16) for g in range(N_GROUPS_B)]
    gm_rows = []
    for c in range(m // cb):
        pieces = []
        for g in range(N_GROUPS_B):
            lanes = slice(g * GROUP_DIM_B, (g + 1) * GROUP_DIM_B)
            mixed = _dot(ws_m[g], vnb[c * cb:(c + 1) * cb, lanes]) + bs_ref[:, lanes]
            pieces.append(ug[c * cb:(c + 1) * cb, lanes] * mixed)
        gm_rows.append(jnp.concatenate(pieces, axis=1))
    gm = gm_rows[0] if len(gm_rows) == 1 else jnp.concatenate(gm_rows, axis=0)

    def put(ref, val):
        ref[...] = val.astype(ref.dtype).reshape(ref.shape)

    if prompt:
        k_out, v_out, ki_out, q_bf, gm_out, kt_bf, v_bf, ki_bf, qit_bf, wit_out = out_refs
        krt = kr.T
        put(k_out, krt)
        put(v_out, v.T)
        put(ki_out, kir.T[:IDX_DIM, :])
        put(kt_bf, krt)
        put(ki_bf, kir[:, :IDX_DIM])
        put(qit_bf, qir.T)
        put(wit_out, wis.T[:8, :])
        lane = lax.broadcasted_iota(jnp.int32, (1, LANES), 1)
        pieces = []
        for hd in range(N_HEADS_A):
            src = v[:, (hd // 2) * LANES:(hd // 2 + 1) * LANES]
            if hd % 2:
                src = pltpu.roll(src, HEAD_DIM, 1)
            pieces.append(jnp.where(lane < HEAD_DIM, src, 1.0))
        put(v_bf, jnp.concatenate(pieces, axis=1))
    else:
        k_out, v_out, ki_out, q_bf, gm_out, qi_bf, wi_out, vn_out = out_refs
        put(qi_bf, qir)
        put(wi_out, wis)
        put(vn_out, vn)
        put(k_out, kr)
        put(v_out, v)
        put(ki_out, kir[:, :IDX_DIM])
    put(q_bf, qr * Q_SCALE)
    put(gm_out, gm)


def _proj_call(x3, mod3, norm_g, w_in_p, cos_t, sin_t, bd, qg, kg, lng, lnb, ws, bs_b, *, nb, r, prompt):
    nseq, t, d = x3.shape
    m = nb * r
    cb = ws.shape[1]
    grid = (nseq // nb, t // r)
    const2 = lambda i, j: (0, 0)
    const3 = lambda i, j: (0, 0, 0)
    row_spec = lambda c: pl.BlockSpec((nb, r, c), lambda i, j: (i, j, 0))
    in_specs = [
        row_spec(d),
        pl.BlockSpec((nb, 1, d), lambda i, j: (i, 0, 0)),
        pl.BlockSpec((nb, 1, d), lambda i, j: (i, 0, 1)),
        pl.BlockSpec((1, d), const2),
        pl.BlockSpec(w_in_p.shape, const2, pipeline_mode=pl.Buffered(1)),
        pl.BlockSpec((1, r, LANES), lambda i, j: (0, j, 0)),
        pl.BlockSpec((1, r, LANES), lambda i, j: (0, j, 0)),
        pl.BlockSpec(bd.shape, const2),
        pl.BlockSpec((1, WIDTH_A), const2),
        pl.BlockSpec((1, WIDTH_A), const2),
        pl.BlockSpec((1, WIDTH_B), const2),
        pl.BlockSpec((1, WIDTH_B), const2),
        pl.BlockSpec(ws.shape, const3),
        pl.BlockSpec(bs_b.shape, const2),
    ]
    if prompt:
        leaf = lambda c: jax.ShapeDtypeStruct((nseq, c, t), _F32)
        leaf_spec = lambda c: pl.BlockSpec((nb, c, r), lambda i, j: (i, 0, j))
    else:
        leaf = lambda c: jax.ShapeDtypeStruct((nseq, t, c), _F32)
        leaf_spec = row_spec
    out_shape = [
        leaf(WIDTH_A), leaf(WIDTH_A), leaf(IDX_DIM),
        jax.ShapeDtypeStruct((nseq, t, WIDTH_A), _BF16),
        jax.ShapeDtypeStruct((nseq, t, WIDTH_B), _BF16),
    ]
    out_specs = [leaf_spec(WIDTH_A), leaf_spec(WIDTH_A), leaf_spec(IDX_DIM), row_spec(WIDTH_A), row_spec(WIDTH_B)]
    if prompt:
        assert nb == 1
        nt = t // r
        out_shape += [
            jax.ShapeDtypeStruct((nseq, nt, WIDTH_A, r), _BF16),
            jax.ShapeDtypeStruct((nseq, t, N_HEADS_A * LANES), _BF16),
            jax.ShapeDtypeStruct((nseq, t, IDX_DIM), _BF16),
            jax.ShapeDtypeStruct((nseq, IDX_HEADS * IDX_DIM, t), _BF16),
            jax.ShapeDtypeStruct((nseq, 8, t), _F32),
        ]
        out_specs += [
            pl.BlockSpec((1, 1, WIDTH_A, r), lambda i, j: (i, j, 0, 0)),
            row_spec(N_HEADS_A * LANES),
            row_spec(IDX_DIM),
            pl.BlockSpec((1, IDX_HEADS * IDX_DIM, r), lambda i, j: (i, 0, j)),
            pl.BlockSpec((1, 8, r), lambda i, j: (i, 0, j)),
        ]
    else:
        out_shape += [
            jax.ShapeDtypeStruct((nseq, t, IDX_HEADS * IDX_DIM), _BF16),
            jax.ShapeDtypeStruct((nseq, t, LANES), _F32),
            jax.ShapeDtypeStruct((nseq, t, WIDTH_B), _F32),
        ]
        out_specs += [row_spec(IDX_HEADS * IDX_DIM), row_spec(LANES), row_spec(WIDTH_B)]
    return pl.pallas_call(
        functools.partial(_proj_kernel, prompt=prompt, cb=cb),
        grid=grid,
        in_specs=in_specs,
        out_specs=out_specs,
        out_shape=out_shape,
        compiler_params=pltpu.CompilerParams(
            dimension_semantics=("arbitrary", "arbitrary"), vmem_limit_bytes=VMEM_LIMIT),
        name="proj_prompt" if prompt else "proj_sample",
    )(x3, mod3, mod3, norm_g, w_in_p, cos_t, sin_t, bd, qg, kg, lng, lnb, ws, bs_b)


def _fori_by_pairs(n, body, init):
    if isinstance(n, int):
        return lax.fori_loop(0, n, body, init)
    carry = lax.fori_loop(0, n // 2, lambda p, c: body(2 * p + 1, body(2 * p, c)), init)
    return lax.cond(n % 2 == 1, lambda c: body(n - 1, c), lambda c: c, carry)


def _key_to_f32(key):
    bits = jnp.where(key >= 0, key, key ^ jnp.int32(0x7FFFFFFF))
    return lax.bitcast_convert_type(bits, _F32)


def _f32_to_key(x):
    bits = lax.bitcast_convert_type(x, jnp.int32)
    return jnp.where(bits >= 0, bits, bits ^ jnp.int32(0x7FFFFFFF))


def _count_keys(sct_ref, nkb, preds):
    _, tk, nq = sct_ref.shape
    g = COUNT_ACC_ROWS

    def body(j, accs):
        x = sct_ref[j]
        return tuple(acc + jnp.where(pred(x), 1.0, 0.0).reshape(tk // g, g, nq).sum(axis=0)
                     for acc, pred in zip(accs, preds))

    accs = _fori_by_pairs(nkb, body, tuple(jnp.zeros((g, nq), _F32) for _ in preds))
    return [jnp.sum(acc, axis=0, keepdims=True) for acc in accs]


def _scores_to_bias(sct_ref, bias_ref, score_max, nkb, topk):
    nblk, tk, nq = sct_ref.shape
    kf = jnp.float32(topk)
    total = jnp.asarray(nkb * tk).astype(_F32)

    def mid_of(lo, hi):
        return lo + lax.shift_right_logical(hi - lo, 1)

    def unsettled_rows(lo, cnt_lo, mid):
        return jnp.max(jnp.where((cnt_lo != kf) & (mid != lo), 1, 0))

    def cond(st):
        return (st[5] > 0) & (st[6] < 34)

    def body(st):
        return step(step(st))

    def step(st):
        lo, hi, cnt_lo, cnt_hi, mid, _, it = st
        thr_mid = _key_to_f32(mid)
        cnt = _count_keys(sct_ref, nkb, [lambda x: x >= thr_mid])[0]
        ge = cnt >= kf
        lo = jnp.where(ge, mid, lo)
        cnt_lo = jnp.where(ge, cnt, cnt_lo)
        hi = jnp.where(ge, hi, mid)
        cnt_hi = jnp.where(ge, cnt_hi, cnt)
        mid = mid_of(lo, hi)
        return lo, hi, cnt_lo, cnt_hi, mid, unsettled_rows(lo, cnt_lo, mid), it + 1

    key_p = _f32_to_key(score_max * 0.125)
    key_t = _f32_to_key(score_max * 2.0)
    probe_p, probe_t = _key_to_f32(key_p), _key_to_f32(key_t)
    c_ge0, c_gt0, c_p, c_t = _count_keys(
        sct_ref, nkb, [lambda x: x >= 0.0, lambda x: x > 0.0, lambda x: x >= probe_p, lambda x: x >= probe_t])
    is_pos = c_gt0 >= kf
    is_zero = c_ge0 >= kf
    p_low = (c_p >= kf) & (key_p > 1)
    p_high = (c_p < kf) & (key_p > 1)
    t_high = (c_t < kf) & (key_t > key_p) & (key_t > 1)
    lo = jnp.where(is_pos, jnp.where(p_low, key_p, 1), jnp.where(is_zero, 0, KEY_NEG_INF))
    cnt_lo = jnp.where(is_pos, jnp.where(p_low, c_p, c_gt0), jnp.where(is_zero, c_ge0, total))
    hi = jnp.where(is_pos, jnp.where(p_high, key_p, jnp.where(t_high, key_t, KEY_POS_INF)),
                   jnp.where(is_zero, 1, -1))
    cnt_hi = jnp.where(is_pos, jnp.where(p_high, c_p, jnp.where(t_high, c_t, 0.0)),
                       jnp.where(is_zero, c_gt0, c_ge0))
    mid = mid_of(lo, hi)
    init = (lo, hi, cnt_lo, cnt_hi, mid, unsettled_rows(lo, cnt_lo, mid), jnp.int32(0))
    lo, hi, cnt_lo, cnt_hi = lax.while_loop(cond, body, init)[:4]

    thr = _key_to_f32(lo)
    thr = jnp.where(jnp.abs(thr) < F32_MIN_NORMAL, 0.0, thr)
    thr = jnp.maximum(thr, -F32_MAX)
    tie = (cnt_lo > kf) & (lo > KEY_NEG_INF)
    need = jnp.where(lo == 1, -1.0, jnp.where(tie, kf - cnt_hi, 1e9))

    rowi = lax.broadcasted_iota(jnp.int32, (LANES, LANES), 0)
    coli = lax.broadcasted_iota(jnp.int32, (LANES, LANES), 1)
    tri = jnp.where(rowi >= coli, 1.0, 0.0).astype(_BF16)

    def fill(j, base):
        for c in range(tk // LANES):
            x = sct_ref[j, c * LANES:(c + 1) * LANES, :]
            eq = x == thr
            rank = base + _dot(tri, jnp.where(eq, 1.0, 0.0).astype(_BF16))
            tied = jnp.where(eq, jnp.where(rank <= need, 0.0, MASK_BIAS), MASK_BIAS)
            bias_ref[j, :, c * LANES:(c + 1) * LANES] = jnp.where(x > thr, 0.0, tied).T
            base = rank[LANES - 1:LANES, :]
        return base

    _fori_by_pairs(nkb, fill, jnp.zeros((1, nq), _F32))


def _dsa_prompt_kernel(q_ref, qit_ref, wit_ref, kg_ref, kt_ref, v_ref, ki_ref, o_ref, sct_ref, bias_ref, m_ref,
                       acc_ref, *, topk):
    i = pl.program_id(1)
    tq = q_ref.shape[1]
    tk = ki_ref.shape[2]
    nkb = ((i + 1) * tq + tk - 1) // tk

    wit = wit_ref[0]
    q_pos = i * tq + lax.broadcasted_iota(jnp.int32, (1, tq), 1)
    vis_lim = (q_pos // CHUNK + 1) * CHUNK
    key_in_block = lax.broadcasted_iota(jnp.int32, (tk, tq), 0)

    g = COUNT_ACC_ROWS

    def score_body(j, run_max):
        ki = ki_ref[0, j]
        s_all = [_dot(ki, qit_ref[0, h * IDX_DIM:(h + 1) * IDX_DIM, :]) for h in range(IDX_HEADS)]
        acc = jnp.zeros((tk, tq), _F32)
        for h in range(IDX_HEADS):
            acc = acc + wit[h:h + 1, :] * jnp.maximum(s_all[h], 0.0)
        sc = jnp.where((j * tk + key_in_block) < vis_lim, acc, -jnp.inf)
        sct_ref[j] = sc
        return jnp.maximum(run_max, sc.reshape(tk // g, g, tq).max(axis=0))

    run_max = _fori_by_pairs(nkb, score_body, jnp.full((g, tq), -jnp.inf, _F32))
    _scores_to_bias(sct_ref, bias_ref, jnp.max(run_max, axis=0, keepdims=True), nkb, topk)

    reps = tk // LANES
    q_heads = [q_ref[0, :, h * HEAD_DIM:(h + 1) * HEAD_DIM] for h in range(N_HEADS_A)]

    def logits(j, h):
        return _dot(q_heads[h], kt_ref[0, j, h * HEAD_DIM:(h + 1) * HEAD_DIM, :]) + bias_ref[j]

    k_norm = jnp.max(jnp.abs(kg_ref[...])) * (HEAD_DIM ** 0.5)
    shift_max = jnp.float32(0.0)
    for h in range(N_HEADS_A):
        qf = q_heads[h].astype(_F32)
        bound = jnp.sqrt(jnp.sum(qf * qf, axis=1, keepdims=True)) * k_norm
        m_ref[h] = jnp.broadcast_to(bound, (tq, LANES))
        shift_max = jnp.maximum(shift_max, jnp.max(bound))

    @pl.when(shift_max > SHIFT_LIMIT)
    def _():
        m_ref[...] = jnp.full(m_ref.shape, -jnp.inf, _F32)

        def max_body(j, carry):
            for h in range(N_HEADS_A):
                s = logits(j, h)
                part = s[:, :LANES]
                for c in range(1, reps):
                    part = jnp.maximum(part, s[:, c * LANES:(c + 1) * LANES])
                m_ref[h] = jnp.maximum(m_ref[h], part)
            return carry

        lax.fori_loop(0, nkb, max_body, 0)
        for h in range(N_HEADS_A):
            m_ref[h] = jnp.broadcast_to(jnp.max(m_ref[h], axis=1, keepdims=True), (tq, LANES))

    acc_ref[...] = jnp.zeros(acc_ref.shape, _F32)

    def att_body(j, carry):
        s_all = [logits(j, h) for h in range(N_HEADS_A)]
        for h in range(N_HEADS_A):
            p = jnp.exp2(s_all[h] - jnp.concatenate([m_ref[h]] * reps, axis=1))
            acc_ref[h] += _dot(p.astype(_BF16), v_ref[0, j, :, h * LANES:(h + 1) * LANES])
        return carry

    _fori_by_pairs(nkb, att_body, 0)
    lane = lax.broadcasted_iota(jnp.int32, (1, LANES), 1)
    for hp in range(N_HEADS_A // 2):
        a0, a1 = acc_ref[2 * hp], acc_ref[2 * hp + 1]
        out0 = a0 / pltpu.roll(a0, HEAD_DIM, 1)
        out1 = pltpu.roll(a1, HEAD_DIM, 1) / a1
        o_ref[0, :, hp * LANES:(hp + 1) * LANES] = jnp.where(lane < HEAD_DIM, out0, out1).astype(o_ref.dtype)


def _dsa_prompt_call(q_bf, qit_bf, wit, kg, kt_bf, v_bf, ki_bf, *, topk):
    b, t, _ = q_bf.shape
    nt, tk = kt_bf.shape[1], kt_bf.shape[3]
    tq = min(DSA_TQ, t)
    v4 = v_bf.reshape(b, nt, tk, N_HEADS_A * LANES)
    ki4 = ki_bf.reshape(b, nt, tk, IDX_DIM)
    resident = lambda shape: pl.BlockSpec((1,) + shape, lambda bi, i: (bi, 0, 0, 0), pipeline_mode=pl.Buffered(1))
    return pl.pallas_call(
        functools.partial(_dsa_prompt_kernel, topk=topk),
        grid=(b, t // tq),
        in_specs=[
            pl.BlockSpec((1, tq, WIDTH_A), lambda bi, i: (bi, i, 0)),
            pl.BlockSpec((1, IDX_HEADS * IDX_DIM, tq), lambda bi, i: (bi, 0, i)),
            pl.BlockSpec((1, 8, tq), lambda bi, i: (bi, 0, i)),
            pl.BlockSpec((1, WIDTH_A), lambda bi, i: (0, 0)),
            resident((nt, WIDTH_A, tk)),
            resident((nt, tk, N_HEADS_A * LANES)),
            resident((nt, tk, IDX_DIM)),
        ],
        out_specs=pl.BlockSpec((1, tq, WIDTH_A), lambda bi, i: (bi, i, 0)),
        out_shape=jax.ShapeDtypeStruct((b, t, WIDTH_A), _BF16),
        scratch_shapes=[
            pltpu.VMEM((nt, tk, tq), _F32),
            pltpu.VMEM((nt, tq, tk), _F32),
            pltpu.VMEM((N_HEADS_A, tq, LANES), _F32),
            pltpu.VMEM((N_HEADS_A, tq, LANES), _F32),
        ],
        compiler_params=pltpu.CompilerParams(
            dimension_semantics=("arbitrary", "arbitrary"), vmem_limit_bytes=VMEM_LIMIT),
        name="dsa_prompt",
    )(q_bf, qit_bf, wit, kg, kt_bf, v4, ki4)


def _new_cols(x, width):
    return jnp.concatenate([x, jnp.zeros((width - x.shape[0], x.shape[1]), _F32)], axis=0).T


def _idx_sample_kernel(qi_ref, wi_ref, kin_ref, ckit_ref, o_ref, sct_ref, bias_ref, kit_ref, *, topk):
    g, t = qi_ref.shape[0], qi_ref.shape[1]
    past = ckit_ref.shape[2]
    s_pad, nq = sct_ref.shape[1], sct_ref.shape[2]
    scores = []
    for b in range(g):
        kit_ref[b, :, :past] = ckit_ref[b].astype(_BF16)
        kin = jnp.concatenate([kin_ref[b], jnp.zeros((t, LANES - IDX_DIM), _F32)], axis=1)
        kit_ref[b, :, past:] = _new_cols(kin, s_pad - past)[:IDX_DIM].astype(_BF16)
        qi = qi_ref[b]
        wi = wi_ref[b]
        qi_stack = jnp.concatenate([qi[:, h * IDX_DIM:(h + 1) * IDX_DIM] for h in range(IDX_HEADS)], axis=0)
        s_idx = _dot(qi_stack, kit_ref[b])
        score = jnp.zeros((t, s_pad), _F32)
        for h in range(IDX_HEADS):
            score = score + wi[:, h:h + 1] * jnp.maximum(s_idx[h * t:(h + 1) * t, :], 0.0)
        scores.append(score)
    key_pos = lax.broadcasted_iota(jnp.int32, (s_pad, nq), 0)
    sc = jnp.where(key_pos < past + t, jnp.concatenate(scores, axis=0).T, -jnp.inf)
    sct_ref[0] = sc
    _scores_to_bias(sct_ref, bias_ref, jnp.max(sc, axis=0, keepdims=True), 1, topk)
    for b in range(g):
        o_ref[b] = bias_ref[0, b * t:(b + 1) * t, :]


def _idx_sample_call(qi_bf, wi, ki_new, cache_kit, *, topk):
    b, t, _ = qi_bf.shape
    past = cache_kit.shape[2]
    s_pad = past + LANES
    assert LANES % t == 0 and b % (LANES // t) == 0
    g = LANES // t
    spec = lambda r, c: pl.BlockSpec((g, r, c), lambda i: (i, 0, 0))
    return pl.pallas_call(
        functools.partial(_idx_sample_kernel, topk=topk),
        grid=(b // g,),
        in_specs=[spec(t, IDX_HEADS * IDX_DIM), spec(t, LANES), spec(t, IDX_DIM), spec(IDX_DIM, past)],
        out_specs=spec(t, s_pad),
        out_shape=jax.ShapeDtypeStruct((b, t, s_pad), _F32),
        scratch_shapes=[
            pltpu.VMEM((1, s_pad, LANES), _F32),
            pltpu.VMEM((1, LANES, s_pad), _F32),
            pltpu.VMEM((g, IDX_DIM, s_pad), _BF16),
        ],
        compiler_params=pltpu.CompilerParams(dimension_semantics=("arbitrary",), vmem_limit_bytes=VMEM_LIMIT),
        name="idx_sample",
    )(qi_bf, wi, ki_new, cache_kit)


def _dsa_sample_kernel(q_ref, bias_ref, kn_ref, vn_ref, ckt_ref, cvt_ref, o_ref, kt_ref, vt_ref):
    t = q_ref.shape[1]
    past = ckt_ref.shape[2]
    s_pad = kt_ref.shape[1]

    kt_ref[:, :past] = ckt_ref[0].astype(_BF16)
    vt_ref[:, :past] = cvt_ref[0].astype(_BF16)
    kt_ref[:, past:] = _new_cols(kn_ref[0], s_pad - past).astype(_BF16)
    vt_ref[:, past:] = _new_cols(vn_ref[0], s_pad - past).astype(_BF16)

    nrow = N_HEADS_A * t
    row_head = lax.broadcasted_iota(jnp.int32, (nrow, WIDTH_A), 0) // t
    col_head = lax.broadcasted_iota(jnp.int32, (nrow, WIDTH_A), 1) // HEAD_DIM
    own = row_head == col_head
    q_rep = jnp.concatenate([q_ref[0]] * N_HEADS_A, axis=0)
    q_bd = jnp.where(own, q_rep, jnp.zeros_like(q_rep))
    s = _dot(q_bd, kt_ref[...]) + jnp.concatenate([bias_ref[0]] * N_HEADS_A, axis=0)
    m = jnp.max(s, axis=1, keepdims=True)
    p = jnp.exp2(s - m)
    l = jnp.sum(p, axis=1, keepdims=True)
    pv = _dot_nt(p.astype(_BF16), vt_ref[...]) / l
    pv = jnp.where(own, pv, 0.0)
    out = pv[:t]
    for h in range(1, N_HEADS_A):
        out = out + pv[h * t:(h + 1) * t]
    o_ref[0] = out.astype(o_ref.dtype)


def _dsa_sample_call(q_bf, bias, k_new, v_new, cache_kt, cache_vt):
    b, t, _ = q_bf.shape
    past = cache_kt.shape[2]
    s_pad = bias.shape[2]
    row = lambda c: pl.BlockSpec((1, t, c), lambda bi: (bi, 0, 0))
    cache = lambda c: pl.BlockSpec((1, c, past), lambda bi: (bi, 0, 0))
    return pl.pallas_call(
        _dsa_sample_kernel,
        grid=(b,),
        in_specs=[row(WIDTH_A), row(s_pad), row(WIDTH_A), row(WIDTH_A), cache(WIDTH_A), cache(WIDTH_A)],
        out_specs=row(WIDTH_A),
        out_shape=jax.ShapeDtypeStruct((b, t, WIDTH_A), _BF16),
        scratch_shapes=[
            pltpu.VMEM((WIDTH_A, s_pad), _BF16),
            pltpu.VMEM((WIDTH_A, s_pad), _BF16),
        ],
        compiler_params=pltpu.CompilerParams(dimension_semantics=("arbitrary",), vmem_limit_bytes=VMEM_LIMIT),
        name="dsa_sample",
    )(q_bf, bias, k_new, v_new, cache_kt, cache_vt)


def _out_kernel(x_ref, at_ref, gm_ref, g1_ref, sh2_ref, sc2_ref, g2_ref, ng_ref, wo_ref, w1_ref, w2_ref, y_ref):
    nb, r, d = x_ref.shape
    m = nb * r
    at = at_ref[...].reshape(m, WIDTH_A)
    gm = gm_ref[...].reshape(m, WIDTH_B)
    y = _dot(at, wo_ref[:WIDTH_A, :]) + _dot(gm, wo_ref[WIDTH_A:, :])
    x1 = x_ref[...] + g1_ref[...] * y.reshape(nb, r, d)
    ms = jnp.mean(x1 * x1, axis=-1, keepdims=True)
    h2 = (x1 * lax.rsqrt(ms + EPS) * ng_ref[...]) * (1.0 + sc2_ref[...]) + sh2_ref[...]
    h2 = h2.reshape(m, d).astype(_BF16)
    dff = w1_ref.shape[1]
    fc = 1024
    ff = jnp.zeros((m, d), _F32)
    for c in range(dff // fc):
        a = jnp.maximum(_dot(h2, w1_ref[:, c * fc:(c + 1) * fc]), 0.0)
        ff = ff + _dot((a * a).astype(_BF16), w2_ref[c * fc:(c + 1) * fc, :])
    y_ref[...] = x1 + g2_ref[...] * ff.reshape(nb, r, d)


def _out_call(x3, attn, gm, mod3, norm_g, w_out, w_ff1, w_ff2, *, nb, r, name):
    nseq, t, d = x3.shape
    const2 = lambda i, j: (0, 0)
    row_spec = lambda c: pl.BlockSpec((nb, r, c), lambda i, j: (i, j, 0))
    mod_spec = lambda col: pl.BlockSpec((nb, 1, d), lambda i, j: (i, 0, col))
    return pl.pallas_call(
        _out_kernel,
        grid=(nseq // nb, t // r),
        in_specs=[row_spec(d), row_spec(WIDTH_A), row_spec(WIDTH_B),
                  mod_spec(2), mod_spec(3), mod_spec(4), mod_spec(5),
                  pl.BlockSpec((1, d), const2),
                  pl.BlockSpec(w_out.shape, const2, pipeline_mode=pl.Buffered(1)),
                  pl.BlockSpec(w_ff1.shape, const2, pipeline_mode=pl.Buffered(1)),
                  pl.BlockSpec(w_ff2.shape, const2, pipeline_mode=pl.Buffered(1))],
        out_specs=row_spec(d),
        out_shape=jax.ShapeDtypeStruct((nseq, t, d), _F32),
        compiler_params=pltpu.CompilerParams(
            dimension_semantics=("arbitrary", "arbitrary"), vmem_limit_bytes=VMEM_LIMIT),
        name=name,
    )(x3, attn, gm, mod3, mod3, mod3, mod3, norm_g, w_out, w_ff1, w_ff2)


def _rope_tables(pos):
    half = HEAD_DIM // 2
    inv_freq = jnp.power(jnp.float32(ROPE_THETA), -jnp.arange(half, dtype=_F32) / half)
    ang = pos.astype(_F32)[:, None] * inv_freq[None, :]
    cos, sin = jnp.cos(ang), jnp.sin(ang)
    reps = LANES // HEAD_DIM
    cos_t = jnp.tile(jnp.concatenate([cos, cos], axis=1), (1, reps))
    sin_t = jnp.tile(jnp.concatenate([-sin, sin], axis=1), (1, reps))
    return cos_t[None], sin_t[None]


def _pad_w_in(w_in):
    d = w_in.shape[0]
    a, i = WIDTH_A, IDX_HEADS * IDX_DIM
    o_ki = 3 * a + i
    o_wi = o_ki + IDX_DIM
    o_u = o_wi + IDX_HEADS
    z = lambda n: jnp.zeros((d, n), w_in.dtype)
    cols = [w_in[:, :o_ki], w_in[:, o_ki:o_wi], z(KI_PAD - IDX_DIM), w_in[:, o_wi:o_u], z(WI_PAD - IDX_HEADS),
            w_in[:, o_u:]]
    return jnp.concatenate(cols, axis=1).astype(_BF16)


def _layer(x_prompt, x_sample, c_prompt, c_sample, cache_k, cache_v, cache_kidx, w_ada, b_ada, norm1_g, norm2_g,
           w_in, q_norm_g, k_norm_g, gmlp_ln_g, gmlp_ln_b, gmlp_ws, gmlp_bs, w_out, w_ff1, w_ff2):
    bp, tp, d = x_prompt.shape
    bs_, ts, _ = x_sample.shape
    past = cache_k.shape[1]

    c_all = jnp.concatenate([c_prompt, c_sample], axis=0)
    rows = -(-c_all.shape[0] // 16) * 16
    c_all = jnp.pad(c_all, ((0, rows - c_all.shape[0]), (0, 0)))
    mod = _ada_call(c_all, w_ada, b_ada[None, :])
    mod_p = mod[:bp, None, :]
    mod_s = mod[bp:bp + bs_, None, :]

    w_in_p = _pad_w_in(w_in)
    w_out_b, w_ff1_b, w_ff2_b = w_out.astype(_BF16), w_ff1.astype(_BF16), w_ff2.astype(_BF16)
    head_of = jnp.arange(WIDTH_A) // HEAD_DIM
    bd = (head_of[:, None] == head_of[None, :]).astype(_BF16)
    qg = jnp.tile(q_norm_g, N_HEADS_A)[None, :]
    kg = jnp.tile(k_norm_g, N_HEADS_A)[None, :]
    lng, lnb = gmlp_ln_g[None, :], gmlp_ln_b[None, :]
    n1, n2 = norm1_g[None, :], norm2_g[None, :]

    rp = min(PROJ_ROWS, tp)
    cos_p, sin_p = _rope_tables(jnp.arange(tp))
    lp = min(tp, GMLP_CHUNK)
    bs_p = jnp.repeat(jnp.transpose(gmlp_bs[:, :lp]), GROUP_DIM_B, axis=1)
    (kp, vp, kip, q_bf, gm_p, kt_bf, v_bf, ki_bf, qit_bf, wit) = _proj_call(
        x_prompt, mod_p, n1, w_in_p, cos_p, sin_p, bd, qg, kg, lng, lnb, gmlp_ws[:, :lp, :lp], bs_p,
        nb=1, r=rp, prompt=True)
    attn_p = _dsa_prompt_call(q_bf, qit_bf, wit, kg, kt_bf, v_bf, ki_bf, topk=min(TOPK_MAX, tp // 4))
    yp = _out_call(x_prompt, attn_p, gm_p, mod_p, n2, w_out_b, w_ff1_b, w_ff2_b, nb=1, r=rp, name="out_prompt")

    nb = min(SAMPLE_GROUP, bs_)
    cos_s, sin_s = _rope_tables(past + jnp.arange(ts))
    ls = min(ts, GMLP_CHUNK)
    assert ls == ts
    eye = jnp.eye(nb, dtype=gmlp_ws.dtype)
    ws_s = jax.vmap(lambda w: jnp.kron(eye, w))(gmlp_ws[:, :ls, :ls])
    bs_s = jnp.tile(jnp.repeat(jnp.transpose(gmlp_bs[:, :ls]), GROUP_DIM_B, axis=1), (nb, 1))
    (ks, vs, kis, qs_bf, gm_s, qis_bf, wi_s, gvs) = _proj_call(
        x_sample, mod_s, n1, w_in_p, cos_s, sin_s, bd, qg, kg, lng, lnb, ws_s, bs_s,
        nb=nb, r=ts, prompt=False)
    cache_kt = jnp.transpose(cache_k, (0, 2, 3, 1)).reshape(bs_, WIDTH_A, past)
    cache_vt = jnp.transpose(cache_v, (0, 2, 3, 1)).reshape(bs_, WIDTH_A, past)
    cache_kit = jnp.transpose(cache_kidx, (0, 2, 1))
    bias_s = _idx_sample_call(qis_bf, wi_s, kis, cache_kit, topk=min(TOPK_MAX, (past + ts) // 4))
    attn_s = _dsa_sample_call(qs_bf, bias_s, ks, vs, cache_kt, cache_vt)
    ys = _out_call(x_sample, attn_s, gm_s, mod_s, n2, w_out_b, w_ff1_b, w_ff2_b, nb=nb, r=ts, name="out_sample")

    heads = lambda a: a.reshape(a.shape[0], a.shape[1], N_HEADS_A, HEAD_DIM)
    heads_t = lambda a: jnp.transpose(a.reshape(a.shape[0], N_HEADS_A, HEAD_DIM, a.shape[2]), (0, 3, 1, 2))
    return yp, ys, heads_t(kp), heads_t(vp), jnp.transpose(kip, (0, 2, 1)), heads(ks), heads(vs), kis, gvs


def kernel(x_prompt, x_sample, c_prompt, c_sample, cache_k, cache_v, cache_kidx, w_ada, b_ada, norm1_g, norm2_g,
           w_in, q_norm_g, k_norm_g, gmlp_ln_g, gmlp_ln_b, gmlp_ws, gmlp_bs, w_out, w_ff1, w_ff2):
    depth = w_ada.shape[0]
    yp, ys = x_prompt, x_sample
    outs = [[] for _ in range(7)]
    for l in range(depth):
        res = _layer(yp, ys, c_prompt, c_sample, cache_k[l], cache_v[l], cache_kidx[l], w_ada[l], b_ada[l],
                     norm1_g[l], norm2_g[l], w_in[l], q_norm_g[l], k_norm_g[l], gmlp_ln_g[l], gmlp_ln_b[l],
                     gmlp_ws[l], gmlp_bs[l], w_out[l], w_ff1[l], w_ff2[l])
        yp, ys = res[0], res[1]
        for acc, leaf in zip(outs, res[2:]):
            acc.append(leaf)
    return (yp, ys) + tuple(jnp.stack(o) for o in outs)
```

```python
import functools

import jax
import jax.numpy as jnp
from jax import lax
from jax.experimental import pallas as pl
from jax.experimental.pallas import tpu as pltpu

N_HEADS_A = 8
HEAD_DIM = 64
WIDTH_A = N_HEADS_A * HEAD_DIM
IDX_HEADS = 4
IDX_DIM = 64
N_GROUPS_B = 4
GROUP_DIM_B = 128
WIDTH_B = N_GROUPS_B * GROUP_DIM_B
GMLP_CHUNK = 128
CHUNK = 64
TOPK_MAX = 256
ROPE_THETA = 10000.0
EPS = 1e-6

LANES = 128
KI_PAD = LANES
WI_PAD = LANES
IN_WIDTH_PADDED = 3 * WIDTH_A + IDX_HEADS * IDX_DIM + KI_PAD + WI_PAD + 2 * WIDTH_B

PROJ_ROWS = 512
DSA_TQ = 256
DSA_TK = PROJ_ROWS
SAMPLE_GROUP = 8
COUNT_ACC_ROWS = 32
VMEM_LIMIT = 56 * 1024 * 1024

MASK_BIAS = -1e30
Q_SCALE = HEAD_DIM ** -0.5 * 1.4426950408889634
SHIFT_LIMIT = 30.0
KEY_NEG_INF = -2139095041
KEY_POS_INF = 2139095040
F32_MIN_NORMAL = 1.1754944e-38
F32_MAX = 3.4028235e38

_F32 = jnp.float32
_BF16 = jnp.bfloat16


def _dot(a, b):
    return jnp.dot(a, b, preferred_element_type=_F32)


def _dot_nt(a, b):
    return lax.dot_general(a, b, (((1,), (1,)), ((), ())), preferred_element_type=_F32)


def _split_bf16(a):
    hi = a.astype(_BF16)
    lo = (a - hi.astype(_F32)).astype(_BF16)
    return hi, lo


def _ada_kernel(c_ref, w_ref, b_ref, o_ref):
    c = c_ref[...]
    s_hi, s_lo = _split_bf16(c * jax.nn.sigmoid(c))
    w_hi, w_lo = _split_bf16(w_ref[...])
    o_ref[...] = _dot(s_hi, w_hi) + _dot(s_lo, w_hi) + _dot(s_hi, w_lo) + b_ref[...]


def _ada_call(c, w_ada, b_ada):
    rows, d = c.shape
    n = w_ada.shape[1]
    tn = 1024
    return pl.pallas_call(
        _ada_kernel,
        grid=(n // tn,),
        in_specs=[
            pl.BlockSpec((rows, d), lambda j: (0, 0)),
            pl.BlockSpec((d, tn), lambda j: (0, j)),
            pl.BlockSpec((1, tn), lambda j: (0, j)),
        ],
        out_specs=pl.BlockSpec((rows, tn), lambda j: (0, j)),
        out_shape=jax.ShapeDtypeStruct((rows, n), _F32),
        compiler_params=pltpu.CompilerParams(dimension_semantics=("arbitrary",), vmem_limit_bytes=VMEM_LIMIT),
        name="ada",
    )(c, w_ada, b_ada)


def _proj_kernel(x_ref, sh_ref, sc_ref, ng_ref, w_ref, cos_ref, sin_ref, bd_ref, qg_ref, kg_ref,
                 lng_ref, lnb_ref, ws_ref, bs_ref, *out_refs, prompt, cb):
    nb, r, d = x_ref.shape
    m = nb * r

    x = x_ref[...]
    ms = jnp.mean(x * x, axis=-1, keepdims=True)
    h = (x * lax.rsqrt(ms + EPS) * ng_ref[...]) * (1.0 + sc_ref[...]) + sh_ref[...]
    h = h.reshape(m, d).astype(_BF16)

    def seg(a, b):
        return _dot(h, w_ref[:, a:b])

    o = 0
    q = seg(o, o + WIDTH_A); o += WIDTH_A
    k = seg(o, o + WIDTH_A); o += WIDTH_A
    v = seg(o, o + WIDTH_A); o += WIDTH_A
    qi = seg(o, o + IDX_HEADS * IDX_DIM); o += IDX_HEADS * IDX_DIM
    kiwi = seg(o, o + KI_PAD + WI_PAD); o += KI_PAD + WI_PAD
    ki, wi = kiwi[:, :KI_PAD], kiwi[:, KI_PAD:]
    u = seg(o, o + WIDTH_B); o += WIDTH_B
    vg = seg(o, o + WIDTH_B)

    bd = bd_ref[...]

    def head_norm(t, g):
        hi, lo = _split_bf16(t * t)
        ss = _dot(hi, bd) + _dot(lo, bd)
        return t * lax.rsqrt(ss * (1.0 / HEAD_DIM) + EPS) * g

    cos1 = jnp.broadcast_to(cos_ref[...], (nb, r, LANES)).reshape(m, LANES)
    sin1 = jnp.broadcast_to(sin_ref[...], (nb, r, LANES)).reshape(m, LANES)

    def rope(t):
        w = t.shape[1]
        reps = w // LANES
        cosw = cos1 if reps == 1 else jnp.concatenate([cos1] * reps, axis=1)
        sinw = sin1 if reps == 1 else jnp.concatenate([sin1] * reps, axis=1)
        lane = lax.broadcasted_iota(jnp.int32, (1, w), 1)
        first_half = (lane % HEAD_DIM) < (HEAD_DIM // 2)
        rot = jnp.where(first_half, pltpu.roll(t, w - HEAD_DIM // 2, 1), pltpu.roll(t, HEAD_DIM // 2, 1))
        return t * cosw + rot * sinw

    qr = rope(head_norm(q, qg_ref[...]))
    kr = rope(head_norm(k, kg_ref[...]))
    qir = rope(qi)
    kir = rope(ki)
    wis = wi * ((IDX_DIM * IDX_HEADS) ** -0.5)

    ug = jax.nn.gelu(u)
    vgg = jax.nn.gelu(vg)
    mu = jnp.mean(vgg, axis=-1, keepdims=True)
    xc = vgg - mu
    var = jnp.mean(xc * xc, axis=-1, keepdims=True)
    vn = xc * lax.rsqrt(var + EPS) * lng_ref[...] + lnb_ref[...]
    vnb = vn.astype(_BF16)
    rowi = lax.broadcasted_iota(jnp.int32, (cb, cb), 0)
    coli = lax.broadcasted_iota(jnp.int32, (cb, cb), 1)
    ws_m = [jnp.where(rowi >= coli, ws_ref[g], 0.0).astype(_BF16) for g in range(N_GROUPS_B)]
    gm_rows = []
    for c in range(m // cb):
        pieces = []
        for g in range(N_GROUPS_B):
            lanes = slice(g * GROUP_DIM_B, (g + 1) * GROUP_DIM_B)
            mixed = _dot(ws_m[g], vnb[c * cb:(c + 1) * cb, lanes]) + bs_ref[:, lanes]
            pieces.append(ug[c * cb:(c + 1) * cb, lanes] * mixed)
        gm_rows.append(jnp.concatenate(pieces, axis=1))
    gm = gm_rows[0] if len(gm_rows) == 1 else jnp.concatenate(gm_rows, axis=0)

    def put(ref, val):
        ref[...] = val.astype(ref.dtype).reshape(ref.shape)

    if prompt:
        k_out, v_out, ki_out, q_bf, gm_out, kt_bf, v_bf, ki_bf, qit_bf, wit_out = out_refs
        krt = kr.T
        put(k_out, krt)
        put(v_out, v.T)
        put(ki_out, kir.T[:IDX_DIM, :])
        put(kt_bf, krt)
        put(ki_bf, kir[:, :IDX_DIM])
        put(qit_bf, qir.T)
        put(wit_out, wis.T[:8, :])
        lane = lax.broadcasted_iota(jnp.int32, (1, LANES), 1)
        pieces = []
        for hd in range(N_HEADS_A):
            src = v[:, (hd // 2) * LANES:(hd // 2 + 1) * LANES]
            if hd % 2:
                src = pltpu.roll(src, HEAD_DIM, 1)
            pieces.append(jnp.where(lane < HEAD_DIM, src, 1.0))
        put(v_bf, jnp.concatenate(pieces, axis=1))
    else:
        k_out, v_out, ki_out, q_bf, gm_out, qi_bf, wi_out, vn_out = out_refs
        put(qi_bf, qir)
        put(wi_out, wis)
        put(vn_out, vn)
        put(k_out, kr)
        put(v_out, v)
        put(ki_out, kir[:, :IDX_DIM])
    put(q_bf, qr * Q_SCALE)
    put(gm_out, gm)


def _proj_call(x3, mod3, norm_g, w_in_p, cos_t, sin_t, bd, qg, kg, lng, lnb, ws, bs_b, *, nb, r, prompt):
    nseq, t, d = x3.shape
    m = nb * r
    cb = ws.shape[1]
    grid = (nseq // nb, t // r)
    const2 = lambda i, j: (0, 0)
    const3 = lambda i, j: (0, 0, 0)
    row_spec = lambda c: pl.BlockSpec((nb, r, c), lambda i, j: (i, j, 0))
    in_specs = [
        row_spec(d),
        pl.BlockSpec((nb, 1, d), lambda i, j: (i, 0, 0)),
        pl.BlockSpec((nb, 1, d), lambda i, j: (i, 0, 1)),
        pl.BlockSpec((1, d), const2),
        pl.BlockSpec(w_in_p.shape, const2, pipeline_mode=pl.Buffered(1)),
        pl.BlockSpec((1, r, LANES), lambda i, j: (0, j, 0)),
        pl.BlockSpec((1, r, LANES), lambda i, j: (0, j, 0)),
        pl.BlockSpec(bd.shape, const2),
        pl.BlockSpec((1, WIDTH_A), const2),
        pl.BlockSpec((1, WIDTH_A), const2),
        pl.BlockSpec((1, WIDTH_B), const2),
        pl.BlockSpec((1, WIDTH_B), const2),
        pl.BlockSpec(ws.shape, const3),
        pl.BlockSpec(bs_b.shape, const2),
    ]
    if prompt:
        leaf = lambda c: jax.ShapeDtypeStruct((nseq, c, t), _F32)
        leaf_spec = lambda c: pl.BlockSpec((nb, c, r), lambda i, j: (i, 0, j))
    else:
        leaf = lambda c: jax.ShapeDtypeStruct((nseq, t, c), _F32)
        leaf_spec = row_spec
    out_shape = [
        leaf(WIDTH_A), leaf(WIDTH_A), leaf(IDX_DIM),
        jax.ShapeDtypeStruct((nseq, t, WIDTH_A), _BF16),
        jax.ShapeDtypeStruct((nseq, t, WIDTH_B), _BF16),
    ]
    out_specs = [leaf_spec(WIDTH_A), leaf_spec(WIDTH_A), leaf_spec(IDX_DIM), row_spec(WIDTH_A), row_spec(WIDTH_B)]
    if prompt:
        assert nb == 1
        nt = t // r
        out_shape += [
            jax.ShapeDtypeStruct((nseq, nt, WIDTH_A, r), _BF16),
            jax.ShapeDtypeStruct((nseq, t, N_HEADS_A * LANES), _BF16),
            jax.ShapeDtypeStruct((nseq, t, IDX_DIM), _BF16),
            jax.ShapeDtypeStruct((nseq, IDX_HEADS * IDX_DIM, t), _BF16),
            jax.ShapeDtypeStruct((nseq, 8, t), _F32),
        ]
        out_specs += [
            pl.BlockSpec((1, 1, WIDTH_A, r), lambda i, j: (i, j, 0, 0)),
            row_spec(N_HEADS_A * LANES),
            row_spec(IDX_DIM),
            pl.BlockSpec((1, IDX_HEADS * IDX_DIM, r), lambda i, j: (i, 0, j)),
            pl.BlockSpec((1, 8, r), lambda i, j: (i, 0, j)),
        ]
    else:
        out_shape += [
            jax.ShapeDtypeStruct((nseq, t, IDX_HEADS * IDX_DIM), _BF16),
            jax.ShapeDtypeStruct((nseq, t, LANES), _F32),
            jax.ShapeDtypeStruct((nseq, t, WIDTH_B), _F32),
        ]
        out_specs += [row_spec(IDX_HEADS * IDX_DIM), row_spec(LANES), row_spec(WIDTH_B)]
    return pl.pallas_call(
        functools.partial(_proj_kernel, prompt=prompt, cb=cb),
        grid=grid,
        in_specs=in_specs,
        out_specs=out_specs,
        out_shape=out_shape,
        compiler_params=pltpu.CompilerParams(
            dimension_semantics=("arbitrary", "arbitrary"), vmem_limit_bytes=VMEM_LIMIT),
        name="proj_prompt" if prompt else "proj_sample",
    )(x3, mod3, mod3, norm_g, w_in_p, cos_t, sin_t, bd, qg, kg, lng, lnb, ws, bs_b)


def _fori_by_pairs(n, body, init):
    if isinstance(n, int):
        return lax.fori_loop(0, n, body, init)
    carry = lax.fori_loop(0, n // 2, lambda p, c: body(2 * p + 1, body(2 * p, c)), init)
    return lax.cond(n % 2 == 1, lambda c: body(n - 1, c), lambda c: c, carry)


def _key_to_f32(key):
    bits = jnp.where(key >= 0, key, key ^ jnp.int32(0x7FFFFFFF))
    return lax.bitcast_convert_type(bits, _F32)


def _f32_to_key(x):
    bits = lax.bitcast_convert_type(x, jnp.int32)
    return jnp.where(bits >= 0, bits, bits ^ jnp.int32(0x7FFFFFFF))


def _count_keys(sct_ref, nkb, preds):
    _, tk, nq = sct_ref.shape
    g = COUNT_ACC_ROWS

    def body(j, accs):
        x = sct_ref[j]
        return tuple(acc + jnp.where(pred(x), 1.0, 0.0).reshape(tk // g, g, nq).sum(axis=0)
                     for acc, pred in zip(accs, preds))

    accs = _fori_by_pairs(nkb, body, tuple(jnp.zeros((g, nq), _F32) for _ in preds))
    return [jnp.sum(acc, axis=0, keepdims=True) for acc in accs]


def _scores_to_bias(sct_ref, bias_ref, score_max, nkb, topk):
    nblk, tk, nq = sct_ref.shape
    kf = jnp.float32(topk)
    total = jnp.asarray(nkb * tk).astype(_F32)

    def mid_of(lo, hi):
        return lo + lax.shift_right_logical(hi - lo, 1)

    def unsettled_rows(lo, cnt_lo, mid):
        return jnp.max(jnp.where((cnt_lo != kf) & (mid != lo), 1, 0))

    def cond(st):
        return (st[5] > 0) & (st[6] < 34)

    def body(st):
        return step(step(st))

    def step(st):
        lo, hi, cnt_lo, cnt_hi, mid, _, it = st
        thr_mid = _key_to_f32(mid)
        cnt = _count_keys(sct_ref, nkb, [lambda x: x >= thr_mid])[0]
        ge = cnt >= kf
        lo = jnp.where(ge, mid, lo)
        cnt_lo = jnp.where(ge, cnt, cnt_lo)
        hi = jnp.where(ge, hi, mid)
        cnt_hi = jnp.where(ge, cnt_hi, cnt)
        mid = mid_of(lo, hi)
        return lo, hi, cnt_lo, cnt_hi, mid, unsettled_rows(lo, cnt_lo, mid), it + 1

    key_p = _f32_to_key(score_max * 0.125)
    key_t = _f32_to_key(score_max)
    probe_p, probe_t = _key_to_f32(key_p), _key_to_f32(key_t)
    c_ge0, c_gt0, c_p, c_t = _count_keys(
        sct_ref, nkb, [lambda x: x >= 0.0, lambda x: x > 0.0, lambda x: x >= probe_p, lambda x: x >= probe_t])
    is_pos = c_gt0 >= kf
    is_zero = c_ge0 >= kf
    p_low = (c_p >= kf) & (key_p > 1)
    p_high = (c_p < kf) & (key_p > 1)
    t_high = (c_t < kf) & (key_t > key_p) & (key_t > 1)
    lo = jnp.where(is_pos, jnp.where(p_low, key_p, 1), jnp.where(is_zero, 0, KEY_NEG_INF))
    cnt_lo = jnp.where(is_pos, jnp.where(p_low, c_p, c_gt0), jnp.where(is_zero, c_ge0, total))
    hi = jnp.where(is_pos, jnp.where(p_high, key_p, jnp.where(t_high, key_t, KEY_POS_INF)),
                   jnp.where(is_zero, 1, -1))
    cnt_hi = jnp.where(is_pos, jnp.where(p_high, c_p, jnp.where(t_high, c_t, 0.0)),
                       jnp.where(is_zero, c_gt0, c_ge0))
    mid = mid_of(lo, hi)
    init = (lo, hi, cnt_lo, cnt_hi, mid, unsettled_rows(lo, cnt_lo, mid), jnp.int32(0))
    lo, hi, cnt_lo, cnt_hi = lax.while_loop(cond, body, init)[:4]

    thr = _key_to_f32(lo)
    thr = jnp.where(jnp.abs(thr) < F32_MIN_NORMAL, 0.0, thr)
    thr = jnp.maximum(thr, -F32_MAX)
    tie = (cnt_lo > kf) & (lo > KEY_NEG_INF)
    need = jnp.where(lo == 1, -1.0, jnp.where(tie, kf - cnt_hi, 1e9))

    rowi = lax.broadcasted_iota(jnp.int32, (LANES, LANES), 0)
    coli = lax.broadcasted_iota(jnp.int32, (LANES, LANES), 1)
    tri = jnp.where(rowi >= coli, 1.0, 0.0).astype(_BF16)

    def fill(j, base):
        for c in range(tk // LANES):
            x = sct_ref[j, c * LANES:(c + 1) * LANES, :]
            eq = x == thr
            rank = base + _dot(tri, jnp.where(eq, 1.0, 0.0).astype(_BF16))
            tied = jnp.where(eq, jnp.where(rank <= need, 0.0, MASK_BIAS), MASK_BIAS)
            bias_ref[j, :, c * LANES:(c + 1) * LANES] = jnp.where(x > thr, 0.0, tied).T
            base = rank[LANES - 1:LANES, :]
        return base

    _fori_by_pairs(nkb, fill, jnp.zeros((1, nq), _F32))


def _dsa_prompt_kernel(q_ref, qit_ref, wit_ref, kg_ref, kt_ref, v_ref, ki_ref, o_ref, sct_ref, bias_ref, m_ref,
                       acc_ref, *, topk):
    i = pl.program_id(1)
    tq = q_ref.shape[1]
    tk = ki_ref.shape[2]
    nkb = ((i + 1) * tq + tk - 1) // tk

    wit = wit_ref[0]
    q_pos = i * tq + lax.broadcasted_iota(jnp.int32, (1, tq), 1)
    vis_lim = (q_pos // CHUNK + 1) * CHUNK
    key_in_block = lax.broadcasted_iota(jnp.int32, (tk, tq), 0)

    g = COUNT_ACC_ROWS

    def score_body(j, run_max):
        ki = ki_ref[0, j]
        s_all = [_dot(ki, qit_ref[0, h * IDX_DIM:(h + 1) * IDX_DIM, :]) for h in range(IDX_HEADS)]
        acc = jnp.zeros((tk, tq), _F32)
        for h in range(IDX_HEADS):
            acc = acc + wit[h:h + 1, :] * jnp.maximum(s_all[h], 0.0)
        sc = jnp.where((j * tk + key_in_block) < vis_lim, acc, -jnp.inf)
        sct_ref[j] = sc
        return jnp.maximum(run_max, sc.reshape(tk // g, g, tq).max(axis=0))

    run_max = _fori_by_pairs(nkb, score_body, jnp.full((g, tq), -jnp.inf, _F32))
    _scores_to_bias(sct_ref, bias_ref, jnp.max(run_max, axis=0, keepdims=True), nkb, topk)

    reps = tk // LANES
    q_heads = [q_ref[0, :, h * HEAD_DIM:(h + 1) * HEAD_DIM] for h in range(N_HEADS_A)]

    def logits(j, h):
        return _dot(q_heads[h], kt_ref[0, j, h * HEAD_DIM:(h + 1) * HEAD_DIM, :]) + bias_ref[j]

    k_norm = jnp.max(jnp.abs(kg_ref[...])) * (HEAD_DIM ** 0.5)
    shift_max = jnp.float32(0.0)
    for h in range(N_HEADS_A):
        qf = q_heads[h].astype(_F32)
        bound = jnp.sqrt(jnp.sum(qf * qf, axis=1, keepdims=True)) * k_norm
        m_ref[h] = jnp.broadcast_to(bound, (tq, LANES))
        shift_max = jnp.maximum(shift_max, jnp.max(bound))

    @pl.when(shift_max > SHIFT_LIMIT)
    def _():
        m_ref[...] = jnp.full(m_ref.shape, -jnp.inf, _F32)

        def max_body(j, carry):
            for h in range(N_HEADS_A):
                s = logits(j, h)
                part = s[:, :LANES]
                for c in range(1, reps):
                    part = jnp.maximum(part, s[:, c * LANES:(c + 1) * LANES])
                m_ref[h] = jnp.maximum(m_ref[h], part)
            return carry

        lax.fori_loop(0, nkb, max_body, 0)
        for h in range(N_HEADS_A):
            m_ref[h] = jnp.broadcast_to(jnp.max(m_ref[h], axis=1, keepdims=True), (tq, LANES))

    acc_ref[...] = jnp.zeros(acc_ref.shape, _F32)

    def att_body(j, carry):
        s_all = [logits(j, h) for h in range(N_HEADS_A)]
        for h in range(N_HEADS_A):
            p = jnp.exp2(s_all[h] - jnp.concatenate([m_ref[h]] * reps, axis=1))
            acc_ref[h] += _dot(p.astype(_BF16), v_ref[0, j, :, h * LANES:(h + 1) * LANES])
        return carry

    _fori_by_pairs(nkb, att_body, 0)
    lane = lax.broadcasted_iota(jnp.int32, (1, LANES), 1)
    for hp in range(N_HEADS_A // 2):
        a0, a1 = acc_ref[2 * hp], acc_ref[2 * hp + 1]
        out0 = a0 / pltpu.roll(a0, HEAD_DIM, 1)
        out1 = pltpu.roll(a1, HEAD_DIM, 1) / a1
        o_ref[0, :, hp * LANES:(hp + 1) * LANES] = jnp.where(lane < HEAD_DIM, out0, out1).astype(o_ref.dtype)


def _dsa_prompt_call(q_bf, qit_bf, wit, kg, kt_bf, v_bf, ki_bf, *, topk):
    b, t, _ = q_bf.shape
    nt, tk = kt_bf.shape[1], kt_bf.shape[3]
    tq = min(DSA_TQ, t)
    v4 = v_bf.reshape(b, nt, tk, N_HEADS_A * LANES)
    ki4 = ki_bf.reshape(b, nt, tk, IDX_DIM)
    resident = lambda shape: pl.BlockSpec((1,) + shape, lambda bi, i: (bi, 0, 0, 0), pipeline_mode=pl.Buffered(1))
    return pl.pallas_call(
        functools.partial(_dsa_prompt_kernel, topk=topk),
        grid=(b, t // tq),
        in_specs=[
            pl.BlockSpec((1, tq, WIDTH_A), lambda bi, i: (bi, i, 0)),
            pl.BlockSpec((1, IDX_HEADS * IDX_DIM, tq), lambda bi, i: (bi, 0, i)),
            pl.BlockSpec((1, 8, tq), lambda bi, i: (bi, 0, i)),
            pl.BlockSpec((1, WIDTH_A), lambda bi, i: (0, 0)),
            resident((nt, WIDTH_A, tk)),
            resident((nt, tk, N_HEADS_A * LANES)),
            resident((nt, tk, IDX_DIM)),
        ],
        out_specs=pl.BlockSpec((1, tq, WIDTH_A), lambda bi, i: (bi, i, 0)),
        out_shape=jax.ShapeDtypeStruct((b, t, WIDTH_A), _BF16),
        scratch_shapes=[
            pltpu.VMEM((nt, tk, tq), _F32),
            pltpu.VMEM((nt, tq, tk), _F32),
            pltpu.VMEM((N_HEADS_A, tq, LANES), _F32),
            pltpu.VMEM((N_HEADS_A, tq, LANES), _F32),
        ],
        compiler_params=pltpu.CompilerParams(
            dimension_semantics=("arbitrary", "arbitrary"), vmem_limit_bytes=VMEM_LIMIT),
        name="dsa_prompt",
    )(q_bf, qit_bf, wit, kg, kt_bf, v4, ki4)


def _new_cols(x, width):
    return jnp.concatenate([x, jnp.zeros((width - x.shape[0], x.shape[1]), _F32)], axis=0).T


def _idx_sample_kernel(qi_ref, wi_ref, kin_ref, ckit_ref, o_ref, sct_ref, bias_ref, kit_ref, *, topk):
    g, t = qi_ref.shape[0], qi_ref.shape[1]
    past = ckit_ref.shape[2]
    s_pad, nq = sct_ref.shape[1], sct_ref.shape[2]
    scores = []
    for b in range(g):
        kit_ref[b, :, :past] = ckit_ref[b].astype(_BF16)
        kin = jnp.concatenate([kin_ref[b], jnp.zeros((t, LANES - IDX_DIM), _F32)], axis=1)
        kit_ref[b, :, past:] = _new_cols(kin, s_pad - past)[:IDX_DIM].astype(_BF16)
        qi = qi_ref[b]
        wi = wi_ref[b]
        qi_stack = jnp.concatenate([qi[:, h * IDX_DIM:(h + 1) * IDX_DIM] for h in range(IDX_HEADS)], axis=0)
        s_idx = _dot(qi_stack, kit_ref[b])
        score = jnp.zeros((t, s_pad), _F32)
        for h in range(IDX_HEADS):
            score = score + wi[:, h:h + 1] * jnp.maximum(s_idx[h * t:(h + 1) * t, :], 0.0)
        scores.append(score)
    key_pos = lax.broadcasted_iota(jnp.int32, (s_pad, nq), 0)
    sc = jnp.where(key_pos < past + t, jnp.concatenate(scores, axis=0).T, -jnp.inf)
    sct_ref[0] = sc
    _scores_to_bias(sct_ref, bias_ref, jnp.max(sc, axis=0, keepdims=True), 1, topk)
    for b in range(g):
        o_ref[b] = bias_ref[0, b * t:(b + 1) * t, :]


def _idx_sample_call(qi_bf, wi, ki_new, cache_kit, *, topk):
    b, t, _ = qi_bf.shape
    past = cache_kit.shape[2]
    s_pad = past + LANES
    assert LANES % t == 0 and b % (LANES // t) == 0
    g = LANES // t
    spec = lambda r, c: pl.BlockSpec((g, r, c), lambda i: (i, 0, 0))
    return pl.pallas_call(
        functools.partial(_idx_sample_kernel, topk=topk),
        grid=(b // g,),
        in_specs=[spec(t, IDX_HEADS * IDX_DIM), spec(t, LANES), spec(t, IDX_DIM), spec(IDX_DIM, past)],
        out_specs=spec(t, s_pad),
        out_shape=jax.ShapeDtypeStruct((b, t, s_pad), _F32),
        scratch_shapes=[
            pltpu.VMEM((1, s_pad, LANES), _F32),
            pltpu.VMEM((1, LANES, s_pad), _F32),
            pltpu.VMEM((g, IDX_DIM, s_pad), _BF16),
        ],
        compiler_params=pltpu.CompilerParams(dimension_semantics=("arbitrary",), vmem_limit_bytes=VMEM_LIMIT),
        name="idx_sample",
    )(qi_bf, wi, ki_new, cache_kit)


def _dsa_sample_kernel(q_ref, bias_ref, kn_ref, vn_ref, ckt_ref, cvt_ref, o_ref, kt_ref, vt_ref):
    t = q_ref.shape[1]
    past = ckt_ref.shape[2]
    s_pad = kt_ref.shape[1]

    kt_ref[:, :past] = ckt_ref[0].astype(_BF16)
    vt_ref[:, :past] = cvt_ref[0].astype(_BF16)
    kt_ref[:, past:] = _new_cols(kn_ref[0], s_pad - past).astype(_BF16)
    vt_ref[:, past:] = _new_cols(vn_ref[0], s_pad - past).astype(_BF16)

    nrow = N_HEADS_A * t
    row_head = lax.broadcasted_iota(jnp.int32, (nrow, WIDTH_A), 0) // t
    col_head = lax.broadcasted_iota(jnp.int32, (nrow, WIDTH_A), 1) // HEAD_DIM
    own = row_head == col_head
    q_rep = jnp.concatenate([q_ref[0]] * N_HEADS_A, axis=0)
    q_bd = jnp.where(own, q_rep, jnp.zeros_like(q_rep))
    s = _dot(q_bd, kt_ref[...]) + jnp.concatenate([bias_ref[0]] * N_HEADS_A, axis=0)
    m = jnp.max(s, axis=1, keepdims=True)
    p = jnp.exp2(s - m)
    l = jnp.sum(p, axis=1, keepdims=True)
    pv = _dot_nt(p.astype(_BF16), vt_ref[...]) / l
    pv = jnp.where(own, pv, 0.0)
    out = pv[:t]
    for h in range(1, N_HEADS_A):
        out = out + pv[h * t:(h + 1) * t]
    o_ref[0] = out.astype(o_ref.dtype)


def _dsa_sample_call(q_bf, bias, k_new, v_new, cache_kt, cache_vt):
    b, t, _ = q_bf.shape
    past = cache_kt.shape[2]
    s_pad = bias.shape[2]
    row = lambda c: pl.BlockSpec((1, t, c), lambda bi: (bi, 0, 0))
    cache = lambda c: pl.BlockSpec((1, c, past), lambda bi: (bi, 0, 0))
    return pl.pallas_call(
        _dsa_sample_kernel,
        grid=(b,),
        in_specs=[row(WIDTH_A), row(s_pad), row(WIDTH_A), row(WIDTH_A), cache(WIDTH_A), cache(WIDTH_A)],
        out_specs=row(WIDTH_A),
        out_shape=jax.ShapeDtypeStruct((b, t, WIDTH_A), _BF16),
        scratch_shapes=[
            pltpu.VMEM((WIDTH_A, s_pad), _BF16),
            pltpu.VMEM((WIDTH_A, s_pad), _BF16),
        ],
        compiler_params=pltpu.CompilerParams(dimension_semantics=("arbitrary",), vmem_limit_bytes=VMEM_LIMIT),
        name="dsa_sample",
    )(q_bf, bias, k_new, v_new, cache_kt, cache_vt)


def _out_kernel(x_ref, at_ref, gm_ref, g1_ref, sh2_ref, sc2_ref, g2_ref, ng_ref, wo_ref, w1_ref, w2_ref, y_ref):
    nb, r, d = x_ref.shape
    m = nb * r
    at = at_ref[...].reshape(m, WIDTH_A)
    gm = gm_ref[...].reshape(m, WIDTH_B)
    y = _dot(at, wo_ref[:WIDTH_A, :]) + _dot(gm, wo_ref[WIDTH_A:, :])
    x1 = x_ref[...] + g1_ref[...] * y.reshape(nb, r, d)
    ms = jnp.mean(x1 * x1, axis=-1, keepdims=True)
    h2 = (x1 * lax.rsqrt(ms + EPS) * ng_ref[...]) * (1.0 + sc2_ref[...]) + sh2_ref[...]
    h2 = h2.reshape(m, d).astype(_BF16)
    dff = w1_ref.shape[1]
    fc = 1024
    ff = jnp.zeros((m, d), _F32)
    for c in range(dff // fc):
        a = jnp.maximum(_dot(h2, w1_ref[:, c * fc:(c + 1) * fc]), 0.0)
        ff = ff + _dot((a * a).astype(_BF16), w2_ref[c * fc:(c + 1) * fc, :])
    y_ref[...] = x1 + g2_ref[...] * ff.reshape(nb, r, d)


def _out_call(x3, attn, gm, mod3, norm_g, w_out, w_ff1, w_ff2, *, nb, r, name):
    nseq, t, d = x3.shape
    const2 = lambda i, j: (0, 0)
    row_spec = lambda c: pl.BlockSpec((nb, r, c), lambda i, j: (i, j, 0))
    mod_spec = lambda col: pl.BlockSpec((nb, 1, d), lambda i, j: (i, 0, col))
    return pl.pallas_call(
        _out_kernel,
        grid=(nseq // nb, t // r),
        in_specs=[row_spec(d), row_spec(WIDTH_A), row_spec(WIDTH_B),
                  mod_spec(2), mod_spec(3), mod_spec(4), mod_spec(5),
                  pl.BlockSpec((1, d), const2),
                  pl.BlockSpec(w_out.shape, const2, pipeline_mode=pl.Buffered(1)),
                  pl.BlockSpec(w_ff1.shape, const2, pipeline_mode=pl.Buffered(1)),
                  pl.BlockSpec(w_ff2.shape, const2, pipeline_mode=pl.Buffered(1))],
        out_specs=row_spec(d),
        out_shape=jax.ShapeDtypeStruct((nseq, t, d), _F32),
        compiler_params=pltpu.CompilerParams(
            dimension_semantics=("arbitrary", "arbitrary"), vmem_limit_bytes=VMEM_LIMIT),
        name=name,
    )(x3, attn, gm, mod3, mod3, mod3, mod3, norm_g, w_out, w_ff1, w_ff2)


def _rope_tables(pos):
    half = HEAD_DIM // 2
    inv_freq = jnp.power(jnp.float32(ROPE_THETA), -jnp.arange(half, dtype=_F32) / half)
    ang = pos.astype(_F32)[:, None] * inv_freq[None, :]
    cos, sin = jnp.cos(ang), jnp.sin(ang)
    reps = LANES // HEAD_DIM
    cos_t = jnp.tile(jnp.concatenate([cos, cos], axis=1), (1, reps))
    sin_t = jnp.tile(jnp.concatenate([-sin, sin], axis=1), (1, reps))
    return cos_t[None], sin_t[None]


def _pad_w_in(w_in):
    d = w_in.shape[0]
    a, i = WIDTH_A, IDX_HEADS * IDX_DIM
    o_ki = 3 * a + i
    o_wi = o_ki + IDX_DIM
    o_u = o_wi + IDX_HEADS
    z = lambda n: jnp.zeros((d, n), w_in.dtype)
    cols = [w_in[:, :o_ki], w_in[:, o_ki:o_wi], z(KI_PAD - IDX_DIM), w_in[:, o_wi:o_u], z(WI_PAD - IDX_HEADS),
            w_in[:, o_u:]]
    return jnp.concatenate(cols, axis=1).astype(_BF16)


def _layer(x_prompt, x_sample, c_prompt, c_sample, cache_k, cache_v, cache_kidx, w_ada, b_ada, norm1_g, norm2_g,
           w_in, q_norm_g, k_norm_g, gmlp_ln_g, gmlp_ln_b, gmlp_ws, gmlp_bs, w_out, w_ff1, w_ff2):
    bp, tp, d = x_prompt.shape
    bs_, ts, _ = x_sample.shape
    past = cache_k.shape[1]

    c_all = jnp.concatenate([c_prompt, c_sample], axis=0)
    rows = -(-c_all.shape[0] // 16) * 16
    c_all = jnp.pad(c_all, ((0, rows - c_all.shape[0]), (0, 0)))
    mod = _ada_call(c_all, w_ada, b_ada[None, :])
    mod_p = mod[:bp, None, :]
    mod_s = mod[bp:bp + bs_, None, :]

    w_in_p = _pad_w_in(w_in)
    w_out_b, w_ff1_b, w_ff2_b = w_out.astype(_BF16), w_ff1.astype(_BF16), w_ff2.astype(_BF16)
    head_of = jnp.arange(WIDTH_A) // HEAD_DIM
    bd = (head_of[:, None] == head_of[None, :]).astype(_BF16)
    qg = jnp.tile(q_norm_g, N_HEADS_A)[None, :]
    kg = jnp.tile(k_norm_g, N_HEADS_A)[None, :]
    lng, lnb = gmlp_ln_g[None, :], gmlp_ln_b[None, :]
    n1, n2 = norm1_g[None, :], norm2_g[None, :]

    rp = min(PROJ_ROWS, tp)
    cos_p, sin_p = _rope_tables(jnp.arange(tp))
    lp = min(tp, GMLP_CHUNK)
    bs_p = jnp.repeat(jnp.transpose(gmlp_bs[:, :lp]), GROUP_DIM_B, axis=1)
    (kp, vp, kip, q_bf, gm_p, kt_bf, v_bf, ki_bf, qit_bf, wit) = _proj_call(
        x_prompt, mod_p, n1, w_in_p, cos_p, sin_p, bd, qg, kg, lng, lnb, gmlp_ws[:, :lp, :lp], bs_p,
        nb=1, r=rp, prompt=True)
    attn_p = _dsa_prompt_call(q_bf, qit_bf, wit, kg, kt_bf, v_bf, ki_bf, topk=min(TOPK_MAX, tp // 4))
    yp = _out_call(x_prompt, attn_p, gm_p, mod_p, n2, w_out_b, w_ff1_b, w_ff2_b, nb=1, r=rp, name="out_prompt")

    nb = min(SAMPLE_GROUP, bs_)
    cos_s, sin_s = _rope_tables(past + jnp.arange(ts))
    ls = min(ts, GMLP_CHUNK)
    assert ls == ts
    eye = jnp.eye(nb, dtype=gmlp_ws.dtype)
    ws_s = jax.vmap(lambda w: jnp.kron(eye, w))(gmlp_ws[:, :ls, :ls])
    bs_s = jnp.tile(jnp.repeat(jnp.transpose(gmlp_bs[:, :ls]), GROUP_DIM_B, axis=1), (nb, 1))
    (ks, vs, kis, qs_bf, gm_s, qis_bf, wi_s, gvs) = _proj_call(
        x_sample, mod_s, n1, w_in_p, cos_s, sin_s, bd, qg, kg, lng, lnb, ws_s, bs_s,
        nb=nb, r=ts, prompt=False)
    cache_kt = jnp.transpose(cache_k, (0, 2, 3, 1)).reshape(bs_, WIDTH_A, past)
    cache_vt = jnp.transpose(cache_v, (0, 2, 3, 1)).reshape(bs_, WIDTH_A, past)
    cache_kit = jnp.transpose(cache_kidx, (0, 2, 1))
    bias_s = _idx_sample_call(qis_bf, wi_s, kis, cache_kit, topk=min(TOPK_MAX, (past + ts) // 4))
    attn_s = _dsa_sample_call(qs_bf, bias_s, ks, vs, cache_kt, cache_vt)
    ys = _out_call(x_sample, attn_s, gm_s, mod_s, n2, w_out_b, w_ff1_b, w_ff2_b, nb=nb, r=ts, name="out_sample")

    heads = lambda a: a.reshape(a.shape[0], a.shape[1], N_HEADS_A, HEAD_DIM)
    heads_t = lambda a: jnp.transpose(a.reshape(a.shape[0], N_HEADS_A, HEAD_DIM, a.shape[2]), (0, 3, 1, 2))
    return yp, ys, heads_t(kp), heads_t(vp), jnp.transpose(kip, (0, 2, 1)), heads(ks), heads(vs), kis, gvs


def kernel(x_prompt, x_sample, c_prompt, c_sample, cache_k, cache_v, cache_kidx, w_ada, b_ada, norm1_g, norm2_g,
           w_in, q_norm_g, k_norm_g, gmlp_ln_g, gmlp_ln_b, gmlp_ws, gmlp_bs, w_out, w_ff1, w_ff2):
    depth = w_ada.shape[0]
    yp, ys = x_prompt, x_sample
    outs = [[] for _ in range(7)]
    for l in range(depth):
        res = _layer(yp, ys, c_prompt, c_sample, cache_k[l], cache_v[l], cache_kidx[l], w_ada[l], b_ada[l],
                     norm1_g[l], norm2_g[l], w_in[l], q_norm_g[l], k_norm_g[l], gmlp_ln_g[l], gmlp_ln_b[l],
                     gmlp_ws[l], gmlp_bs[l], w_out[l], w_ff1[l], w_ff2[l])
        yp, ys = res[0], res[1]
        for acc, leaf in zip(outs, res[2:]):
            acc.append(leaf)
    return (yp, ys) + tuple(jnp.stack(o) for o in outs)
```

```python
import functools

import jax
import jax.numpy as jnp
from jax import lax
from jax.experimental import pallas as pl
from jax.experimental.pallas import tpu as pltpu

N_HEADS_A = 8
HEAD_DIM = 64
WIDTH_A = N_HEADS_A * HEAD_DIM
IDX_HEADS = 4
IDX_DIM = 64
N_GROUPS_B = 4
GROUP_DIM_B = 128
WIDTH_B = N_GROUPS_B * GROUP_DIM_B
GMLP_CHUNK = 128
CHUNK = 64
TOPK_MAX = 256
ROPE_THETA = 10000.0
EPS = 1e-6

LANES = 128
KI_PAD = LANES
WI_PAD = LANES
IN_WIDTH_PADDED = 3 * WIDTH_A + IDX_HEADS * IDX_DIM + KI_PAD + WI_PAD + 2 * WIDTH_B

PROJ_ROWS = 512
DSA_TQ = 256
DSA_TK = PROJ_ROWS
SAMPLE_GROUP = 8
COUNT_ACC_ROWS = 16
VMEM_LIMIT = 56 * 1024 * 1024

MASK_BIAS = -1e30
Q_SCALE = HEAD_DIM ** -0.5 * 1.4426950408889634
SHIFT_LIMIT = 30.0
KEY_NEG_INF = -2139095041
KEY_POS_INF = 2139095040
F32_MIN_NORMAL = 1.1754944e-38
F32_MAX = 3.4028235e38

_F32 = jnp.float32
_BF16 = jnp.bfloat16


def _dot(a, b):
    return jnp.dot(a, b, preferred_element_type=_F32)


def _dot_nt(a, b):
    return lax.dot_general(a, b, (((1,), (1,)), ((), ())), preferred_element_type=_F32)


def _split_bf16(a):
    hi = a.astype(_BF16)
    lo = (a - hi.astype(_F32)).astype(_BF16)
    return hi, lo


def _ada_kernel(c_ref, w_ref, b_ref, o_ref):
    c = c_ref[...]
    s_hi, s_lo = _split_bf16(c * jax.nn.sigmoid(c))
    w_hi, w_lo = _split_bf16(w_ref[...])
    o_ref[...] = _dot(s_hi, w_hi) + _dot(s_lo, w_hi) + _dot(s_hi, w_lo) + b_ref[...]


def _ada_call(c, w_ada, b_ada):
    rows, d = c.shape
    n = w_ada.shape[1]
    tn = 1024
    return pl.pallas_call(
        _ada_kernel,
        grid=(n // tn,),
        in_specs=[
            pl.BlockSpec((rows, d), lambda j: (0, 0)),
            pl.BlockSpec((d, tn), lambda j: (0, j)),
            pl.BlockSpec((1, tn), lambda j: (0, j)),
        ],
        out_specs=pl.BlockSpec((rows, tn), lambda j: (0, j)),
        out_shape=jax.ShapeDtypeStruct((rows, n), _F32),
        compiler_params=pltpu.CompilerParams(dimension_semantics=("arbitrary",), vmem_limit_bytes=VMEM_LIMIT),
        name="ada",
    )(c, w_ada, b_ada)


def _proj_kernel(x_ref, sh_ref, sc_ref, ng_ref, w_ref, cos_ref, sin_ref, bd_ref, qg_ref, kg_ref,
                 lng_ref, lnb_ref, ws_ref, bs_ref, *out_refs, prompt, cb):
    nb, r, d = x_ref.shape
    m = nb * r

    x = x_ref[...]
    ms = jnp.mean(x * x, axis=-1, keepdims=True)
    h = (x * lax.rsqrt(ms + EPS) * ng_ref[...]) * (1.0 + sc_ref[...]) + sh_ref[...]
    h = h.reshape(m, d).astype(_BF16)

    def seg(a, b):
        return _dot(h, w_ref[:, a:b])

    o = 0
    q = seg(o, o + WIDTH_A); o += WIDTH_A
    k = seg(o, o + WIDTH_A); o += WIDTH_A
    v = seg(o, o + WIDTH_A); o += WIDTH_A
    qi = seg(o, o + IDX_HEADS * IDX_DIM); o += IDX_HEADS * IDX_DIM
    kiwi = seg(o, o + KI_PAD + WI_PAD); o += KI_PAD + WI_PAD
    ki, wi = kiwi[:, :KI_PAD], kiwi[:, KI_PAD:]
    u = seg(o, o + WIDTH_B); o += WIDTH_B
    vg = seg(o, o + WIDTH_B)

    bd = bd_ref[...]

    def head_norm(t, g):
        hi, lo = _split_bf16(t * t)
        ss = _dot(hi, bd) + _dot(lo, bd)
        return t * lax.rsqrt(ss * (1.0 / HEAD_DIM) + EPS) * g

    cos1 = jnp.broadcast_to(cos_ref[...], (nb, r, LANES)).reshape(m, LANES)
    sin1 = jnp.broadcast_to(sin_ref[...], (nb, r, LANES)).reshape(m, LANES)

    def rope(t):
        w = t.shape[1]
        reps = w // LANES
        cosw = cos1 if reps == 1 else jnp.concatenate([cos1] * reps, axis=1)
        sinw = sin1 if reps == 1 else jnp.concatenate([sin1] * reps, axis=1)
        lane = lax.broadcasted_iota(jnp.int32, (1, w), 1)
        first_half = (lane % HEAD_DIM) < (HEAD_DIM // 2)
        rot = jnp.where(first_half, pltpu.roll(t, w - HEAD_DIM // 2, 1), pltpu.roll(t, HEAD_DIM // 2, 1))
        return t * cosw + rot * sinw

    qr = rope(head_norm(q, qg_ref[...]))
    kr = rope(head_norm(k, kg_ref[...]))
    qir = rope(qi)
    kir = rope(ki)
    wis = wi * ((IDX_DIM * IDX_HEADS) ** -0.5)

    ug = jax.nn.gelu(u)
    vgg = jax.nn.gelu(vg)
    mu = jnp.mean(vgg, axis=-1, keepdims=True)
    xc = vgg - mu
    var = jnp.mean(xc * xc, axis=-1, keepdims=True)
    vn = xc * lax.rsqrt(var + EPS) * lng_ref[...] + lnb_ref[...]
    vnb = vn.astype(_BF16)
    rowi = lax.broadcasted_iota(jnp.int32, (cb, cb), 0)
    coli = lax.broadcasted_iota(jnp.int32, (cb, cb), 1)
    ws_m = [jnp.where(rowi >= coli, ws_ref[g], 0.0).astype(_BF16) for g in range(N_GROUPS_B)]
    gm_rows = []
    for c in range(m // cb):
        pieces = []
        for g in range(N_GROUPS_B):
            lanes = slice(g * GROUP_DIM_B, (g + 1) * GROUP_DIM_B)
            mixed = _dot(ws_m[g], vnb[c * cb:(c + 1) * cb, lanes]) + bs_ref[:, lanes]
            pieces.append(ug[c * cb:(c + 1) * cb, lanes] * mixed)
        gm_rows.append(jnp.concatenate(pieces, axis=1))
    gm = gm_rows[0] if len(gm_rows) == 1 else jnp.concatenate(gm_rows, axis=0)

    def put(ref, val):
        ref[...] = val.astype(ref.dtype).reshape(ref.shape)

    if prompt:
        k_out, v_out, ki_out, q_bf, gm_out, kt_bf, v_bf, ki_bf, qit_bf, wit_out = out_refs
        krt = kr.T
        put(k_out, krt)
        put(v_out, v.T)
        put(ki_out, kir.T[:IDX_DIM, :])
        put(kt_bf, krt)
        put(ki_bf, kir[:, :IDX_DIM])
        put(qit_bf, qir.T)
        put(wit_out, wis.T[:8, :])
        lane = lax.broadcasted_iota(jnp.int32, (1, LANES), 1)
        pieces = []
        for hd in range(N_HEADS_A):
            src = v[:, (hd // 2) * LANES:(hd // 2 + 1) * LANES]
            if hd % 2:
                src = pltpu.roll(src, HEAD_DIM, 1)
            pieces.append(jnp.where(lane < HEAD_DIM, src, 1.0))
        put(v_bf, jnp.concatenate(pieces, axis=1))
    else:
        k_out, v_out, ki_out, q_bf, gm_out, qi_bf, wi_out, vn_out = out_refs
        put(qi_bf, qir)
        put(wi_out, wis)
        put(vn_out, vn)
        put(k_out, kr)
        put(v_out, v)
        put(ki_out, kir[:, :IDX_DIM])
    put(q_bf, qr * Q_SCALE)
    put(gm_out, gm)


def _proj_call(x3, mod3, norm_g, w_in_p, cos_t, sin_t, bd, qg, kg, lng, lnb, ws, bs_b, *, nb, r, prompt):
    nseq, t, d = x3.shape
    m = nb * r
    cb = ws.shape[1]
    grid = (nseq // nb, t // r)
    const2 = lambda i, j: (0, 0)
    const3 = lambda i, j: (0, 0, 0)
    row_spec = lambda c: pl.BlockSpec((nb, r, c), lambda i, j: (i, j, 0))
    in_specs = [
        row_spec(d),
        pl.BlockSpec((nb, 1, d), lambda i, j: (i, 0, 0)),
        pl.BlockSpec((nb, 1, d), lambda i, j: (i, 0, 1)),
        pl.BlockSpec((1, d), const2),
        pl.BlockSpec(w_in_p.shape, const2, pipeline_mode=pl.Buffered(1)),
        pl.BlockSpec((1, r, LANES), lambda i, j: (0, j, 0)),
        pl.BlockSpec((1, r, LANES), lambda i, j: (0, j, 0)),
        pl.BlockSpec(bd.shape, const2),
        pl.BlockSpec((1, WIDTH_A), const2),
        pl.BlockSpec((1, WIDTH_A), const2),
        pl.BlockSpec((1, WIDTH_B), const2),
        pl.BlockSpec((1, WIDTH_B), const2),
        pl.BlockSpec(ws.shape, const3),
        pl.BlockSpec(bs_b.shape, const2),
    ]
    if prompt:
        leaf = lambda c: jax.ShapeDtypeStruct((nseq, c, t), _F32)
        leaf_spec = lambda c: pl.BlockSpec((nb, c, r), lambda i, j: (i, 0, j))
    else:
        leaf = lambda c: jax.ShapeDtypeStruct((nseq, t, c), _F32)
        leaf_spec = row_spec
    out_shape = [
        leaf(WIDTH_A), leaf(WIDTH_A), leaf(IDX_DIM),
        jax.ShapeDtypeStruct((nseq, t, WIDTH_A), _BF16),
        jax.ShapeDtypeStruct((nseq, t, WIDTH_B), _BF16),
    ]
    out_specs = [leaf_spec(WIDTH_A), leaf_spec(WIDTH_A), leaf_spec(IDX_DIM), row_spec(WIDTH_A), row_spec(WIDTH_B)]
    if prompt:
        assert nb == 1
        nt = t // r
        out_shape += [
            jax.ShapeDtypeStruct((nseq, nt, WIDTH_A, r), _BF16),
            jax.ShapeDtypeStruct((nseq, t, N_HEADS_A * LANES), _BF16),
            jax.ShapeDtypeStruct((nseq, t, IDX_DIM), _BF16),
            jax.ShapeDtypeStruct((nseq, IDX_HEADS * IDX_DIM, t), _BF16),
            jax.ShapeDtypeStruct((nseq, 8, t), _F32),
        ]
        out_specs += [
            pl.BlockSpec((1, 1, WIDTH_A, r), lambda i, j: (i, j, 0, 0)),
            row_spec(N_HEADS_A * LANES),
            row_spec(IDX_DIM),
            pl.BlockSpec((1, IDX_HEADS * IDX_DIM, r), lambda i, j: (i, 0, j)),
            pl.BlockSpec((1, 8, r), lambda i, j: (i, 0, j)),
        ]
    else:
        out_shape += [
            jax.ShapeDtypeStruct((nseq, t, IDX_HEADS * IDX_DIM), _BF16),
            jax.ShapeDtypeStruct((nseq, t, LANES), _F32),
            jax.ShapeDtypeStruct((nseq, t, WIDTH_B), _F32),
        ]
        out_specs += [row_spec(IDX_HEADS * IDX_DIM), row_spec(LANES), row_spec(WIDTH_B)]
    return pl.pallas_call(
        functools.partial(_proj_kernel, prompt=prompt, cb=cb),
        grid=grid,
        in_specs=in_specs,
        out_specs=out_specs,
        out_shape=out_shape,
        compiler_params=pltpu.CompilerParams(
            dimension_semantics=("arbitrary", "arbitrary"), vmem_limit_bytes=VMEM_LIMIT),
        name="proj_prompt" if prompt else "proj_sample",
    )(x3, mod3, mod3, norm_g, w_in_p, cos_t, sin_t, bd, qg, kg, lng, lnb, ws, bs_b)


def _fori_by_pairs(n, body, init):
    if isinstance(n, int):
        return lax.fori_loop(0, n, body, init)
    carry = lax.fori_loop(0, n // 2, lambda p, c: body(2 * p + 1, body(2 * p, c)), init)
    return lax.cond(n % 2 == 1, lambda c: body(n - 1, c), lambda c: c, carry)


def _key_to_f32(key):
    bits = jnp.where(key >= 0, key, key ^ jnp.int32(0x7FFFFFFF))
    return lax.bitcast_convert_type(bits, _F32)


def _f32_to_key(x):
    bits = lax.bitcast_convert_type(x, jnp.int32)
    return jnp.where(bits >= 0, bits, bits ^ jnp.int32(0x7FFFFFFF))


def _count_keys(sct_ref, nkb, preds):
    _, tk, nq = sct_ref.shape
    g = COUNT_ACC_ROWS

    def body(j, accs):
        x = sct_ref[j]
        return tuple(acc + jnp.where(pred(x), 1.0, 0.0).reshape(tk // g, g, nq).sum(axis=0)
                     for acc, pred in zip(accs, preds))

    accs = _fori_by_pairs(nkb, body, tuple(jnp.zeros((g, nq), _F32) for _ in preds))
    return [jnp.sum(acc, axis=0, keepdims=True) for acc in accs]


def _scores_to_bias(sct_ref, bias_ref, score_max, nkb, topk):
    nblk, tk, nq = sct_ref.shape
    kf = jnp.float32(topk)
    total = jnp.asarray(nkb * tk).astype(_F32)

    def mid_of(lo, hi):
        return lo + lax.shift_right_logical(hi - lo, 1)

    def unsettled_rows(lo, cnt_lo, mid):
        return jnp.max(jnp.where((cnt_lo != kf) & (mid != lo), 1, 0))

    def cond(st):
        return (st[5] > 0) & (st[6] < 34)

    def body(st):
        return step(step(st))

    def step(st):
        lo, hi, cnt_lo, cnt_hi, mid, _, it = st
        thr_mid = _key_to_f32(mid)
        cnt = _count_keys(sct_ref, nkb, [lambda x: x >= thr_mid])[0]
        ge = cnt >= kf
        lo = jnp.where(ge, mid, lo)
        cnt_lo = jnp.where(ge, cnt, cnt_lo)
        hi = jnp.where(ge, hi, mid)
        cnt_hi = jnp.where(ge, cnt_hi, cnt)
        mid = mid_of(lo, hi)
        return lo, hi, cnt_lo, cnt_hi, mid, unsettled_rows(lo, cnt_lo, mid), it + 1

    key_p = _f32_to_key(score_max * 0.125)
    key_t = _f32_to_key(score_max)
    probe_p, probe_t = _key_to_f32(key_p), _key_to_f32(key_t)
    c_ge0, c_gt0, c_p, c_t = _count_keys(
        sct_ref, nkb, [lambda x: x >= 0.0, lambda x: x > 0.0, lambda x: x >= probe_p, lambda x: x >= probe_t])
    is_pos = c_gt0 >= kf
    is_zero = c_ge0 >= kf
    p_low = (c_p >= kf) & (key_p > 1)
    p_high = (c_p < kf) & (key_p > 1)
    t_high = (c_t < kf) & (key_t > key_p) & (key_t > 1)
    lo = jnp.where(is_pos, jnp.where(p_low, key_p, 1), jnp.where(is_zero, 0, KEY_NEG_INF))
    cnt_lo = jnp.where(is_pos, jnp.where(p_low, c_p, c_gt0), jnp.where(is_zero, c_ge0, total))
    hi = jnp.where(is_pos, jnp.where(p_high, key_p, jnp.where(t_high, key_t, KEY_POS_INF)),
                   jnp.where(is_zero, 1, -1))
    cnt_hi = jnp.where(is_pos, jnp.where(p_high, c_p, jnp.where(t_high, c_t, 0.0)),
                       jnp.where(is_zero, c_gt0, c_ge0))
    mid = mid_of(lo, hi)
    init = (lo, hi, cnt_lo, cnt_hi, mid, unsettled_rows(lo, cnt_lo, mid), jnp.int32(0))
    lo, hi, cnt_lo, cnt_hi = lax.while_loop(cond, body, init)[:4]

    thr = _key_to_f32(lo)
    thr = jnp.where(jnp.abs(thr) < F32_MIN_NORMAL, 0.0, thr)
    thr = jnp.maximum(thr, -F32_MAX)
    tie = (cnt_lo > kf) & (lo > KEY_NEG_INF)
    need = jnp.where(lo == 1, -1.0, jnp.where(tie, kf - cnt_hi, 1e9))

    rowi = lax.broadcasted_iota(jnp.int32, (LANES, LANES), 0)
    coli = lax.broadcasted_iota(jnp.int32, (LANES, LANES), 1)
    tri = jnp.where(rowi >= coli, 1.0, 0.0).astype(_BF16)

    def fill(j, base):
        for c in range(tk // LANES):
            x = sct_ref[j, c * LANES:(c + 1) * LANES, :]
            eq = x == thr
            rank = base + _dot(tri, jnp.where(eq, 1.0, 0.0).astype(_BF16))
            tied = jnp.where(eq, jnp.where(rank <= need, 0.0, MASK_BIAS), MASK_BIAS)
            bias_ref[j, :, c * LANES:(c + 1) * LANES] = jnp.where(x > thr, 0.0, tied).T
            base = rank[LANES - 1:LANES, :]
        return base

    _fori_by_pairs(nkb, fill, jnp.zeros((1, nq), _F32))


def _dsa_prompt_kernel(q_ref, qit_ref, wit_ref, kg_ref, kt_ref, v_ref, ki_ref, o_ref, sct_ref, bias_ref, m_ref,
                       acc_ref, *, topk):
    i = pl.program_id(1)
    tq = q_ref.shape[1]
    tk = ki_ref.shape[2]
    nkb = ((i + 1) * tq + tk - 1) // tk

    wit = wit_ref[0]
    q_pos = i * tq + lax.broadcasted_iota(jnp.int32, (1, tq), 1)
    vis_lim = (q_pos // CHUNK + 1) * CHUNK
    key_in_block = lax.broadcasted_iota(jnp.int32, (tk, tq), 0)

    g = COUNT_ACC_ROWS

    def score_body(j, run_max):
        ki = ki_ref[0, j]
        s_all = [_dot(ki, qit_ref[0, h * IDX_DIM:(h + 1) * IDX_DIM, :]) for h in range(IDX_HEADS)]
        acc = jnp.zeros((tk, tq), _F32)
        for h in range(IDX_HEADS):
            acc = acc + wit[h:h + 1, :] * jnp.maximum(s_all[h], 0.0)
        sc = jnp.where((j * tk + key_in_block) < vis_lim, acc, -jnp.inf)
        sct_ref[j] = sc
        return jnp.maximum(run_max, sc.reshape(tk // g, g, tq).max(axis=0))

    run_max = _fori_by_pairs(nkb, score_body, jnp.full((g, tq), -jnp.inf, _F32))
    _scores_to_bias(sct_ref, bias_ref, jnp.max(run_max, axis=0, keepdims=True), nkb, topk)

    reps = tk // LANES
    q_heads = [q_ref[0, :, h * HEAD_DIM:(h + 1) * HEAD_DIM] for h in range(N_HEADS_A)]

    def logits(j, h):
        return _dot(q_heads[h], kt_ref[0, j, h * HEAD_DIM:(h + 1) * HEAD_DIM, :]) + bias_ref[j]

    k_norm = jnp.max(jnp.abs(kg_ref[...])) * (HEAD_DIM ** 0.5)
    shift_max = jnp.float32(0.0)
    for h in range(N_HEADS_A):
        qf = q_heads[h].astype(_F32)
        bound = jnp.sqrt(jnp.sum(qf * qf, axis=1, keepdims=True)) * k_norm
        m_ref[h] = jnp.broadcast_to(bound, (tq, LANES))
        shift_max = jnp.maximum(shift_max, jnp.max(bound))

    @pl.when(shift_max > SHIFT_LIMIT)
    def _():
        m_ref[...] = jnp.full(m_ref.shape, -jnp.inf, _F32)

        def max_body(j, carry):
            for h in range(N_HEADS_A):
                s = logits(j, h)
                part = s[:, :LANES]
                for c in range(1, reps):
                    part = jnp.maximum(part, s[:, c * LANES:(c + 1) * LANES])
                m_ref[h] = jnp.maximum(m_ref[h], part)
            return carry

        lax.fori_loop(0, nkb, max_body, 0)
        for h in range(N_HEADS_A):
            m_ref[h] = jnp.broadcast_to(jnp.max(m_ref[h], axis=1, keepdims=True), (tq, LANES))

    acc_ref[...] = jnp.zeros(acc_ref.shape, _F32)

    def att_body(j, carry):
        s_all = [logits(j, h) for h in range(N_HEADS_A)]
        for h in range(N_HEADS_A):
            p = jnp.exp2(s_all[h] - jnp.concatenate([m_ref[h]] * reps, axis=1))
            acc_ref[h] += _dot(p.astype(_BF16), v_ref[0, j, :, h * LANES:(h + 1) * LANES])
        return carry

    _fori_by_pairs(nkb, att_body, 0)
    lane = lax.broadcasted_iota(jnp.int32, (1, LANES), 1)
    for hp in range(N_HEADS_A // 2):
        a0, a1 = acc_ref[2 * hp], acc_ref[2 * hp + 1]
        out0 = a0 / pltpu.roll(a0, HEAD_DIM, 1)
        out1 = pltpu.roll(a1, HEAD_DIM, 1) / a1
        o_ref[0, :, hp * LANES:(hp + 1) * LANES] = jnp.where(lane < HEAD_DIM, out0, out1).astype(o_ref.dtype)


def _dsa_prompt_call(q_bf, qit_bf, wit, kg, kt_bf, v_bf, ki_bf, *, topk):
    b, t, _ = q_bf.shape
    nt, tk = kt_bf.shape[1], kt_bf.shape[3]
    tq = min(DSA_TQ, t)
    v4 = v_bf.reshape(b, nt, tk, N_HEADS_A * LANES)
    ki4 = ki_bf.reshape(b, nt, tk, IDX_DIM)
    resident = lambda shape: pl.BlockSpec((1,) + shape, lambda bi, i: (bi, 0, 0, 0), pipeline_mode=pl.Buffered(1))
    return pl.pallas_call(
        functools.partial(_dsa_prompt_kernel, topk=topk),
        grid=(b, t // tq),
        in_specs=[
            pl.BlockSpec((1, tq, WIDTH_A), lambda bi, i: (bi, i, 0)),
            pl.BlockSpec((1, IDX_HEADS * IDX_DIM, tq), lambda bi, i: (bi, 0, i)),
            pl.BlockSpec((1, 8, tq), lambda bi, i: (bi, 0, i)),
            pl.BlockSpec((1, WIDTH_A), lambda bi, i: (0, 0)),
            resident((nt, WIDTH_A, tk)),
            resident((nt, tk, N_HEADS_A * LANES)),
            resident((nt, tk, IDX_DIM)),
        ],
        out_specs=pl.BlockSpec((1, tq, WIDTH_A), lambda bi, i: (bi, i, 0)),
        out_shape=jax.ShapeDtypeStruct((b, t, WIDTH_A), _BF16),
        scratch_shapes=[
            pltpu.VMEM((nt, tk, tq), _F32),
            pltpu.VMEM((nt, tq, tk), _F32),
            pltpu.VMEM((N_HEADS_A, tq, LANES), _F32),
            pltpu.VMEM((N_HEADS_A, tq, LANES), _F32),
        ],
        compiler_params=pltpu.CompilerParams(
            dimension_semantics=("arbitrary", "arbitrary"), vmem_limit_bytes=VMEM_LIMIT),
        name="dsa_prompt",
    )(q_bf, qit_bf, wit, kg, kt_bf, v4, ki4)


def _new_cols(x, width):
    return jnp.concatenate([x, jnp.zeros((width - x.shape[0], x.shape[1]), _F32)], axis=0).T


def _idx_sample_kernel(qi_ref, wi_ref, kin_ref, ckit_ref, o_ref, sct_ref, bias_ref, kit_ref, *, topk):
    g, t = qi_ref.shape[0], qi_ref.shape[1]
    past = ckit_ref.shape[2]
    s_pad, nq = sct_ref.shape[1], sct_ref.shape[2]
    scores = []
    for b in range(g):
        kit_ref[b, :, :past] = ckit_ref[b].astype(_BF16)
        kin = jnp.concatenate([kin_ref[b], jnp.zeros((t, LANES - IDX_DIM), _F32)], axis=1)
        kit_ref[b, :, past:] = _new_cols(kin, s_pad - past)[:IDX_DIM].astype(_BF16)
        qi = qi_ref[b]
        wi = wi_ref[b]
        qi_stack = jnp.concatenate([qi[:, h * IDX_DIM:(h + 1) * IDX_DIM] for h in range(IDX_HEADS)], axis=0)
        s_idx = _dot(qi_stack, kit_ref[b])
        score = jnp.zeros((t, s_pad), _F32)
        for h in range(IDX_HEADS):
            score = score + wi[:, h:h + 1] * jnp.maximum(s_idx[h * t:(h + 1) * t, :], 0.0)
        scores.append(score)
    key_pos = lax.broadcasted_iota(jnp.int32, (s_pad, nq), 0)
    sc = jnp.where(key_pos < past + t, jnp.concatenate(scores, axis=0).T, -jnp.inf)
    sct_ref[0] = sc
    _scores_to_bias(sct_ref, bias_ref, jnp.max(sc, axis=0, keepdims=True), 1, topk)
    for b in range(g):
        o_ref[b] = bias_ref[0, b * t:(b + 1) * t, :]


def _idx_sample_call(qi_bf, wi, ki_new, cache_kit, *, topk):
    b, t, _ = qi_bf.shape
    past = cache_kit.shape[2]
    s_pad = past + LANES
    assert LANES % t == 0 and b % (LANES // t) == 0
    g = LANES // t
    spec = lambda r, c: pl.BlockSpec((g, r, c), lambda i: (i, 0, 0))
    return pl.pallas_call(
        functools.partial(_idx_sample_kernel, topk=topk),
        grid=(b // g,),
        in_specs=[spec(t, IDX_HEADS * IDX_DIM), spec(t, LANES), spec(t, IDX_DIM), spec(IDX_DIM, past)],
        out_specs=spec(t, s_pad),
        out_shape=jax.ShapeDtypeStruct((b, t, s_pad), _F32),
        scratch_shapes=[
            pltpu.VMEM((1, s_pad, LANES), _F32),
            pltpu.VMEM((1, LANES, s_pad), _F32),
            pltpu.VMEM((g, IDX_DIM, s_pad), _BF16),
        ],
        compiler_params=pltpu.CompilerParams(dimension_semantics=("arbitrary",), vmem_limit_bytes=VMEM_LIMIT),
        name="idx_sample",
    )(qi_bf, wi, ki_new, cache_kit)


def _dsa_sample_kernel(q_ref, bias_ref, kn_ref, vn_ref, ckt_ref, cvt_ref, o_ref, kt_ref, vt_ref):
    t = q_ref.shape[1]
    past = ckt_ref.shape[2]
    s_pad = kt_ref.shape[1]

    kt_ref[:, :past] = ckt_ref[0].astype(_BF16)
    vt_ref[:, :past] = cvt_ref[0].astype(_BF16)
    kt_ref[:, past:] = _new_cols(kn_ref[0], s_pad - past).astype(_BF16)
    vt_ref[:, past:] = _new_cols(vn_ref[0], s_pad - past).astype(_BF16)

    nrow = N_HEADS_A * t
    row_head = lax.broadcasted_iota(jnp.int32, (nrow, WIDTH_A), 0) // t
    col_head = lax.broadcasted_iota(jnp.int32, (nrow, WIDTH_A), 1) // HEAD_DIM
    own = row_head == col_head
    q_rep = jnp.concatenate([q_ref[0]] * N_HEADS_A, axis=0)
    q_bd = jnp.where(own, q_rep, jnp.zeros_like(q_rep))
    s = _dot(q_bd, kt_ref[...]) + jnp.concatenate([bias_ref[0]] * N_HEADS_A, axis=0)
    m = jnp.max(s, axis=1, keepdims=True)
    p = jnp.exp2(s - m)
    l = jnp.sum(p, axis=1, keepdims=True)
    pv = _dot_nt(p.astype(_BF16), vt_ref[...]) / l
    pv = jnp.where(own, pv, 0.0)
    out = pv[:t]
    for h in range(1, N_HEADS_A):
        out = out + pv[h * t:(h + 1) * t]
    o_ref[0] = out.astype(o_ref.dtype)


def _dsa_sample_call(q_bf, bias, k_new, v_new, cache_kt, cache_vt):
    b, t, _ = q_bf.shape
    past = cache_kt.shape[2]
    s_pad = bias.shape[2]
    row = lambda c: pl.BlockSpec((1, t, c), lambda bi: (bi, 0, 0))
    cache = lambda c: pl.BlockSpec((1, c, past), lambda bi: (bi, 0, 0))
    return pl.pallas_call(
        _dsa_sample_kernel,
        grid=(b,),
        in_specs=[row(WIDTH_A), row(s_pad), row(WIDTH_A), row(WIDTH_A), cache(WIDTH_A), cache(WIDTH_A)],
        out_specs=row(WIDTH_A),
        out_shape=jax.ShapeDtypeStruct((b, t, WIDTH_A), _BF16),
        scratch_shapes=[
            pltpu.VMEM((WIDTH_A, s_pad), _BF16),
            pltpu.VMEM((WIDTH_A, s_pad), _BF16),
        ],
        compiler_params=pltpu.CompilerParams(dimension_semantics=("arbitrary",), vmem_limit_bytes=VMEM_LIMIT),
        name="dsa_sample",
    )(q_bf, bias, k_new, v_new, cache_kt, cache_vt)


def _out_kernel(x_ref, at_ref, gm_ref, g1_ref, sh2_ref, sc2_ref, g2_ref, ng_ref, wo_ref, w1_ref, w2_ref, y_ref):
    nb, r, d = x_ref.shape
    m = nb * r
    at = at_ref[...].reshape(m, WIDTH_A)
    gm = gm_ref[...].reshape(m, WIDTH_B)
    y = _dot(at, wo_ref[:WIDTH_A, :]) + _dot(gm, wo_ref[WIDTH_A:, :])
    x1 = x_ref[...] + g1_ref[...] * y.reshape(nb, r, d)
    ms = jnp.mean(x1 * x1, axis=-1, keepdims=True)
    h2 = (x1 * lax.rsqrt(ms + EPS) * ng_ref[...]) * (1.0 + sc2_ref[...]) + sh2_ref[...]
    h2 = h2.reshape(m, d).astype(_BF16)
    dff = w1_ref.shape[1]
    fc = 1024
    ff = jnp.zeros((m, d), _F32)
    for c in range(dff // fc):
        a = jnp.maximum(_dot(h2, w1_ref[:, c * fc:(c + 1) * fc]), 0.0)
        ff = ff + _dot((a * a).astype(_BF16), w2_ref[c * fc:(c + 1) * fc, :])
    y_ref[...] = x1 + g2_ref[...] * ff.reshape(nb, r, d)


def _out_call(x3, attn, gm, mod3, norm_g, w_out, w_ff1, w_ff2, *, nb, r, name):
    nseq, t, d = x3.shape
    const2 = lambda i, j: (0, 0)
    row_spec = lambda c: pl.BlockSpec((nb, r, c), lambda i, j: (i, j, 0))
    mod_spec = lambda col: pl.BlockSpec((nb, 1, d), lambda i, j: (i, 0, col))
    return pl.pallas_call(
        _out_kernel,
        grid=(nseq // nb, t // r),
        in_specs=[row_spec(d), row_spec(WIDTH_A), row_spec(WIDTH_B),
                  mod_spec(2), mod_spec(3), mod_spec(4), mod_spec(5),
                  pl.BlockSpec((1, d), const2),
                  pl.BlockSpec(w_out.shape, const2, pipeline_mode=pl.Buffered(1)),
                  pl.BlockSpec(w_ff1.shape, const2, pipeline_mode=pl.Buffered(1)),
                  pl.BlockSpec(w_ff2.shape, const2, pipeline_mode=pl.Buffered(1))],
        out_specs=row_spec(d),
        out_shape=jax.ShapeDtypeStruct((nseq, t, d), _F32),
        compiler_params=pltpu.CompilerParams(
            dimension_semantics=("arbitrary", "arbitrary"), vmem_limit_bytes=VMEM_LIMIT),
        name=name,
    )(x3, attn, gm, mod3, mod3, mod3, mod3, norm_g, w_out, w_ff1, w_ff2)


def _rope_tables(pos):
    half = HEAD_DIM // 2
    inv_freq = jnp.power(jnp.float32(ROPE_THETA), -jnp.arange(half, dtype=_F32) / half)
    ang = pos.astype(_F32)[:, None] * inv_freq[None, :]
    cos, sin = jnp.cos(ang), jnp.sin(ang)
    reps = LANES // HEAD_DIM
    cos_t = jnp.tile(jnp.concatenate([cos, cos], axis=1), (1, reps))
    sin_t = jnp.tile(jnp.concatenate([-sin, sin], axis=1), (1, reps))
    return cos_t[None], sin_t[None]


def _pad_w_in(w_in):
    d = w_in.shape[0]
    a, i = WIDTH_A, IDX_HEADS * IDX_DIM
    o_ki = 3 * a + i
    o_wi = o_ki + IDX_DIM
    o_u = o_wi + IDX_HEADS
    z = lambda n: jnp.zeros((d, n), w_in.dtype)
    cols = [w_in[:, :o_ki], w_in[:, o_ki:o_wi], z(KI_PAD - IDX_DIM), w_in[:, o_wi:o_u], z(WI_PAD - IDX_HEADS),
            w_in[:, o_u:]]
    return jnp.concatenate(cols, axis=1).astype(_BF16)


def _layer(x_prompt, x_sample, c_prompt, c_sample, cache_k, cache_v, cache_kidx, w_ada, b_ada, norm1_g, norm2_g,
           w_in, q_norm_g, k_norm_g, gmlp_ln_g, gmlp_ln_b, gmlp_ws, gmlp_bs, w_out, w_ff1, w_ff2):
    bp, tp, d = x_prompt.shape
    bs_, ts, _ = x_sample.shape
    past = cache_k.shape[1]

    c_all = jnp.concatenate([c_prompt, c_sample], axis=0)
    rows = -(-c_all.shape[0] // 16) * 16
    c_all = jnp.pad(c_all, ((0, rows - c_all.shape[0]), (0, 0)))
    mod = _ada_call(c_all, w_ada, b_ada[None, :])
    mod_p = mod[:bp, None, :]
    mod_s = mod[bp:bp + bs_, None, :]

    w_in_p = _pad_w_in(w_in)
    w_out_b, w_ff1_b, w_ff2_b = w_out.astype(_BF16), w_ff1.astype(_BF16), w_ff2.astype(_BF16)
    head_of = jnp.arange(WIDTH_A) // HEAD_DIM
    bd = (head_of[:, None] == head_of[None, :]).astype(_BF16)
    qg = jnp.tile(q_norm_g, N_HEADS_A)[None, :]
    kg = jnp.tile(k_norm_g, N_HEADS_A)[None, :]
    lng, lnb = gmlp_ln_g[None, :], gmlp_ln_b[None, :]
    n1, n2 = norm1_g[None, :], norm2_g[None, :]

    rp = min(PROJ_ROWS, tp)
    cos_p, sin_p = _rope_tables(jnp.arange(tp))
    lp = min(tp, GMLP_CHUNK)
    bs_p = jnp.repeat(jnp.transpose(gmlp_bs[:, :lp]), GROUP_DIM_B, axis=1)
    (kp, vp, kip, q_bf, gm_p, kt_bf, v_bf, ki_bf, qit_bf, wit) = _proj_call(
        x_prompt, mod_p, n1, w_in_p, cos_p, sin_p, bd, qg, kg, lng, lnb, gmlp_ws[:, :lp, :lp], bs_p,
        nb=1, r=rp, prompt=True)
    attn_p = _dsa_prompt_call(q_bf, qit_bf, wit, kg, kt_bf, v_bf, ki_bf, topk=min(TOPK_MAX, tp // 4))
    yp = _out_call(x_prompt, attn_p, gm_p, mod_p, n2, w_out_b, w_ff1_b, w_ff2_b, nb=1, r=rp, name="out_prompt")

    nb = min(SAMPLE_GROUP, bs_)
    cos_s, sin_s = _rope_tables(past + jnp.arange(ts))
    ls = min(ts, GMLP_CHUNK)
    assert ls == ts
    eye = jnp.eye(nb, dtype=gmlp_ws.dtype)
    ws_s = jax.vmap(lambda w: jnp.kron(eye, w))(gmlp_ws[:, :ls, :ls])
    bs_s = jnp.tile(jnp.repeat(jnp.transpose(gmlp_bs[:, :ls]), GROUP_DIM_B, axis=1), (nb, 1))
    (ks, vs, kis, qs_bf, gm_s, qis_bf, wi_s, gvs) = _proj_call(
        x_sample, mod_s, n1, w_in_p, cos_s, sin_s, bd, qg, kg, lng, lnb, ws_s, bs_s,
        nb=nb, r=ts, prompt=False)
    cache_kt = jnp.transpose(cache_k, (0, 2, 3, 1)).reshape(bs_, WIDTH_A, past)
    cache_vt = jnp.transpose(cache_v, (0, 2, 3, 1)).reshape(bs_, WIDTH_A, past)
    cache_kit = jnp.transpose(cache_kidx, (0, 2, 1))
    bias_s = _idx_sample_call(qis_bf, wi_s, kis, cache_kit, topk=min(TOPK_MAX, (past + ts) // 4))
    attn_s = _dsa_sample_call(qs_bf, bias_s, ks, vs, cache_kt, cache_vt)
    ys = _out_call(x_sample, attn_s, gm_s, mod_s, n2, w_out_b, w_ff1_b, w_ff2_b, nb=nb, r=ts, name="out_sample")

    heads = lambda a: a.reshape(a.shape[0], a.shape[1], N_HEADS_A, HEAD_DIM)
    heads_t = lambda a: jnp.transpose(a.reshape(a.shape[0], N_HEADS_A, HEAD_DIM, a.shape[2]), (0, 3, 1, 2))
    return yp, ys, heads_t(kp), heads_t(vp), jnp.transpose(kip, (0, 2, 1)), heads(ks), heads(vs), kis, gvs


def kernel(x_prompt, x_sample, c_prompt, c_sample, cache_k, cache_v, cache_kidx, w_ada, b_ada, norm1_g, norm2_g,
           w_in, q_norm_g, k_norm_g, gmlp_ln_g, gmlp_ln_b, gmlp_ws, gmlp_bs, w_out, w_ff1, w_ff2):
    depth = w_ada.shape[0]
    yp, ys = x_prompt, x_sample
    outs = [[] for _ in range(7)]
    for l in range(depth):
        res = _layer(yp, ys, c_prompt, c_sample, cache_k[l], cache_v[l], cache_kidx[l], w_ada[l], b_ada[l],
                     norm1_g[l], norm2_g[l], w_in[l], q_norm_g[l], k_norm_g[l], gmlp_ln_g[l], gmlp_ln_b[l],
                     gmlp_ws[l], gmlp_bs[l], w_out[l], w_ff1[l], w_ff2[l])
        yp, ys = res[0], res[1]
        for acc, leaf in zip(outs, res[2:]):
            acc.append(leaf)
    return (yp, ys) + tuple(jnp.stack(o) for o in outs)
```

```python
import functools

import jax
import jax.numpy as jnp
from jax import lax
from jax.experimental import pallas as pl
from jax.experimental.pallas import tpu as pltpu

N_HEADS_A = 8
HEAD_DIM = 64
WIDTH_A = N_HEADS_A * HEAD_DIM
IDX_HEADS = 4
IDX_DIM = 64
N_GROUPS_B = 4
GROUP_DIM_B = 128
WIDTH_B = N_GROUPS_B * GROUP_DIM_B
GMLP_CHUNK = 128
CHUNK = 64
TOPK_MAX = 256
ROPE_THETA = 10000.0
EPS = 1e-6

LANES = 128
KI_PAD = LANES
WI_PAD = LANES
IN_WIDTH_PADDED = 3 * WIDTH_A + IDX_HEADS * IDX_DIM + KI_PAD + WI_PAD + 2 * WIDTH_B

PROJ_ROWS = 512
DSA_TQ = 256
DSA_TK = PROJ_ROWS
SAMPLE_GROUP = 8
COUNT_ACC_ROWS = 16
UNTESTED_STEPS = 12
VMEM_LIMIT = 56 * 1024 * 1024

MASK_BIAS = -1e30
Q_SCALE = HEAD_DIM ** -0.5 * 1.4426950408889634
SHIFT_LIMIT = 30.0
KEY_NEG_INF = -2139095041
KEY_POS_INF = 2139095040
F32_MIN_NORMAL = 1.1754944e-38
F32_MAX = 3.4028235e38

_F32 = jnp.float32
_BF16 = jnp.bfloat16


def _dot(a, b):
    return jnp.dot(a, b, preferred_element_type=_F32)


def _dot_nt(a, b):
    return lax.dot_general(a, b, (((1,), (1,)), ((), ())), preferred_element_type=_F32)


def _split_bf16(a):
    hi = a.astype(_BF16)
    lo = (a - hi.astype(_F32)).astype(_BF16)
    return hi, lo


def _ada_kernel(c_ref, w_ref, b_ref, o_ref):
    c = c_ref[...]
    s_hi, s_lo = _split_bf16(c * jax.nn.sigmoid(c))
    w_hi, w_lo = _split_bf16(w_ref[...])
    o_ref[...] = _dot(s_hi, w_hi) + _dot(s_lo, w_hi) + _dot(s_hi, w_lo) + b_ref[...]


def _ada_call(c, w_ada, b_ada):
    rows, d = c.shape
    n = w_ada.shape[1]
    tn = 1024
    return pl.pallas_call(
        _ada_kernel,
        grid=(n // tn,),
        in_specs=[
            pl.BlockSpec((rows, d), lambda j: (0, 0)),
            pl.BlockSpec((d, tn), lambda j: (0, j)),
            pl.BlockSpec((1, tn), lambda j: (0, j)),
        ],
        out_specs=pl.BlockSpec((rows, tn), lambda j: (0, j)),
        out_shape=jax.ShapeDtypeStruct((rows, n), _F32),
        compiler_params=pltpu.CompilerParams(dimension_semantics=("arbitrary",), vmem_limit_bytes=VMEM_LIMIT),
        name="ada",
    )(c, w_ada, b_ada)


def _proj_kernel(x_ref, sh_ref, sc_ref, ng_ref, w_ref, cos_ref, sin_ref, bd_ref, qg_ref, kg_ref,
                 lng_ref, lnb_ref, ws_ref, bs_ref, *out_refs, prompt, cb):
    nb, r, d = x_ref.shape
    m = nb * r

    x = x_ref[...]
    ms = jnp.mean(x * x, axis=-1, keepdims=True)
    h = (x * lax.rsqrt(ms + EPS) * ng_ref[...]) * (1.0 + sc_ref[...]) + sh_ref[...]
    h = h.reshape(m, d).astype(_BF16)

    def seg(a, b):
        return _dot(h, w_ref[:, a:b])

    o = 0
    q = seg(o, o + WIDTH_A); o += WIDTH_A
    k = seg(o, o + WIDTH_A); o += WIDTH_A
    v = seg(o, o + WIDTH_A); o += WIDTH_A
    qi = seg(o, o + IDX_HEADS * IDX_DIM); o += IDX_HEADS * IDX_DIM
    kiwi = seg(o, o + KI_PAD + WI_PAD); o += KI_PAD + WI_PAD
    ki, wi = kiwi[:, :KI_PAD], kiwi[:, KI_PAD:]
    u = seg(o, o + WIDTH_B); o += WIDTH_B
    vg = seg(o, o + WIDTH_B)

    bd = bd_ref[...]

    def head_norm(t, g):
        hi, lo = _split_bf16(t * t)
        ss = _dot(hi, bd) + _dot(lo, bd)
        return t * lax.rsqrt(ss * (1.0 / HEAD_DIM) + EPS) * g

    cos1 = jnp.broadcast_to(cos_ref[...], (nb, r, LANES)).reshape(m, LANES)
    sin1 = jnp.broadcast_to(sin_ref[...], (nb, r, LANES)).reshape(m, LANES)

    def rope(t):
        w = t.shape[1]
        reps = w // LANES
        cosw = cos1 if reps == 1 else jnp.concatenate([cos1] * reps, axis=1)
        sinw = sin1 if reps == 1 else jnp.concatenate([sin1] * reps, axis=1)
        lane = lax.broadcasted_iota(jnp.int32, (1, w), 1)
        first_half = (lane % HEAD_DIM) < (HEAD_DIM // 2)
        rot = jnp.where(first_half, pltpu.roll(t, w - HEAD_DIM // 2, 1), pltpu.roll(t, HEAD_DIM // 2, 1))
        return t * cosw + rot * sinw

    qr = rope(head_norm(q, qg_ref[...]))
    kr = rope(head_norm(k, kg_ref[...]))
    qir = rope(qi)
    kir = rope(ki)
    wis = wi * ((IDX_DIM * IDX_HEADS) ** -0.5)

    ug = jax.nn.gelu(u)
    vgg = jax.nn.gelu(vg)
    mu = jnp.mean(vgg, axis=-1, keepdims=True)
    xc = vgg - mu
    var = jnp.mean(xc * xc, axis=-1, keepdims=True)
    vn = xc * lax.rsqrt(var + EPS) * lng_ref[...] + lnb_ref[...]
    vnb = vn.astype(_BF16)
    rowi = lax.broadcasted_iota(jnp.int32, (cb, cb), 0)
    coli = lax.broadcasted_iota(jnp.int32, (cb, cb), 1)
    ws_m = [jnp.where(rowi >= coli, ws_ref[g], 0.0).astype(_BF16) for g in range(N_GROUPS_B)]
    gm_rows = []
    for c in range(m // cb):
        pieces = []
        for g in range(N_GROUPS_B):
            lanes = slice(g * GROUP_DIM_B, (g + 1) * GROUP_DIM_B)
            mixed = _dot(ws_m[g], vnb[c * cb:(c + 1) * cb, lanes]) + bs_ref[:, lanes]
            pieces.append(ug[c * cb:(c + 1) * cb, lanes] * mixed)
        gm_rows.append(jnp.concatenate(pieces, axis=1))
    gm = gm_rows[0] if len(gm_rows) == 1 else jnp.concatenate(gm_rows, axis=0)

    def put(ref, val):
        ref[...] = val.astype(ref.dtype).reshape(ref.shape)

    if prompt:
        k_out, v_out, ki_out, q_bf, gm_out, kt_bf, v_bf, ki_bf, qit_bf, wit_out = out_refs
        krt = kr.T
        put(k_out, krt)
        put(v_out, v.T)
        put(ki_out, kir.T[:IDX_DIM, :])
        put(kt_bf, krt)
        put(ki_bf, kir[:, :IDX_DIM])
        put(qit_bf, qir.T)
        put(wit_out, wis.T[:8, :])
        lane = lax.broadcasted_iota(jnp.int32, (1, LANES), 1)
        pieces = []
        for hd in range(N_HEADS_A):
            src = v[:, (hd // 2) * LANES:(hd // 2 + 1) * LANES]
            if hd % 2:
                src = pltpu.roll(src, HEAD_DIM, 1)
            pieces.append(jnp.where(lane < HEAD_DIM, src, 1.0))
        put(v_bf, jnp.concatenate(pieces, axis=1))
    else:
        k_out, v_out, ki_out, q_bf, gm_out, qi_bf, wi_out, vn_out = out_refs
        put(qi_bf, qir)
        put(wi_out, wis)
        put(vn_out, vn)
        put(k_out, kr)
        put(v_out, v)
        put(ki_out, kir[:, :IDX_DIM])
    put(q_bf, qr * Q_SCALE)
    put(gm_out, gm)


def _proj_call(x3, mod3, norm_g, w_in_p, cos_t, sin_t, bd, qg, kg, lng, lnb, ws, bs_b, *, nb, r, prompt):
    nseq, t, d = x3.shape
    m = nb * r
    cb = ws.shape[1]
    grid = (nseq // nb, t // r)
    const2 = lambda i, j: (0, 0)
    const3 = lambda i, j: (0, 0, 0)
    row_spec = lambda c: pl.BlockSpec((nb, r, c), lambda i, j: (i, j, 0))
    in_specs = [
        row_spec(d),
        pl.BlockSpec((nb, 1, d), lambda i, j: (i, 0, 0)),
        pl.BlockSpec((nb, 1, d), lambda i, j: (i, 0, 1)),
        pl.BlockSpec((1, d), const2),
        pl.BlockSpec(w_in_p.shape, const2, pipeline_mode=pl.Buffered(1)),
        pl.BlockSpec((1, r, LANES), lambda i, j: (0, j, 0)),
        pl.BlockSpec((1, r, LANES), lambda i, j: (0, j, 0)),
        pl.BlockSpec(bd.shape, const2),
        pl.BlockSpec((1, WIDTH_A), const2),
        pl.BlockSpec((1, WIDTH_A), const2),
        pl.BlockSpec((1, WIDTH_B), const2),
        pl.BlockSpec((1, WIDTH_B), const2),
        pl.BlockSpec(ws.shape, const3),
        pl.BlockSpec(bs_b.shape, const2),
    ]
    if prompt:
        leaf = lambda c: jax.ShapeDtypeStruct((nseq, c, t), _F32)
        leaf_spec = lambda c: pl.BlockSpec((nb, c, r), lambda i, j: (i, 0, j))
    else:
        leaf = lambda c: jax.ShapeDtypeStruct((nseq, t, c), _F32)
        leaf_spec = row_spec
    out_shape = [
        leaf(WIDTH_A), leaf(WIDTH_A), leaf(IDX_DIM),
        jax.ShapeDtypeStruct((nseq, t, WIDTH_A), _BF16),
        jax.ShapeDtypeStruct((nseq, t, WIDTH_B), _BF16),
    ]
    out_specs = [leaf_spec(WIDTH_A), leaf_spec(WIDTH_A), leaf_spec(IDX_DIM), row_spec(WIDTH_A), row_spec(WIDTH_B)]
    if prompt:
        assert nb == 1
        nt = t // r
        out_shape += [
            jax.ShapeDtypeStruct((nseq, nt, WIDTH_A, r), _BF16),
            jax.ShapeDtypeStruct((nseq, t, N_HEADS_A * LANES), _BF16),
            jax.ShapeDtypeStruct((nseq, t, IDX_DIM), _BF16),
            jax.ShapeDtypeStruct((nseq, IDX_HEADS * IDX_DIM, t), _BF16),
            jax.ShapeDtypeStruct((nseq, 8, t), _F32),
        ]
        out_specs += [
            pl.BlockSpec((1, 1, WIDTH_A, r), lambda i, j: (i, j, 0, 0)),
            row_spec(N_HEADS_A * LANES),
            row_spec(IDX_DIM),
            pl.BlockSpec((1, IDX_HEADS * IDX_DIM, r), lambda i, j: (i, 0, j)),
            pl.BlockSpec((1, 8, r), lambda i, j: (i, 0, j)),
        ]
    else:
        out_shape += [
            jax.ShapeDtypeStruct((nseq, t, IDX_HEADS * IDX_DIM), _BF16),
            jax.ShapeDtypeStruct((nseq, t, LANES), _F32),
            jax.ShapeDtypeStruct((nseq, t, WIDTH_B), _F32),
        ]
        out_specs += [row_spec(IDX_HEADS * IDX_DIM), row_spec(LANES), row_spec(WIDTH_B)]
    return pl.pallas_call(
        functools.partial(_proj_kernel, prompt=prompt, cb=cb),
        grid=grid,
        in_specs=in_specs,
        out_specs=out_specs,
        out_shape=out_shape,
        compiler_params=pltpu.CompilerParams(
            dimension_semantics=("arbitrary", "arbitrary"), vmem_limit_bytes=VMEM_LIMIT),
        name="proj_prompt" if prompt else "proj_sample",
    )(x3, mod3, mod3, norm_g, w_in_p, cos_t, sin_t, bd, qg, kg, lng, lnb, ws, bs_b)


def _fori_by_pairs(n, body, init):
    if isinstance(n, int):
        return lax.fori_loop(0, n, body, init)
    carry = lax.fori_loop(0, n // 2, lambda p, c: body(2 * p + 1, body(2 * p, c)), init)
    return lax.cond(n % 2 == 1, lambda c: body(n - 1, c), lambda c: c, carry)


def _key_to_f32(key):
    bits = jnp.where(key >= 0, key, key ^ jnp.int32(0x7FFFFFFF))
    return lax.bitcast_convert_type(bits, _F32)


def _f32_to_key(x):
    bits = lax.bitcast_convert_type(x, jnp.int32)
    return jnp.where(bits >= 0, bits, bits ^ jnp.int32(0x7FFFFFFF))


def _count_keys(sct_ref, nkb, preds):
    _, tk, nq = sct_ref.shape
    g = COUNT_ACC_ROWS

    def body(j, accs):
        x = sct_ref[j]
        return tuple(acc + jnp.where(pred(x), 1.0, 0.0).reshape(tk // g, g, nq).sum(axis=0)
                     for acc, pred in zip(accs, preds))

    accs = _fori_by_pairs(nkb, body, tuple(jnp.zeros((g, nq), _F32) for _ in preds))
    return [jnp.sum(acc, axis=0, keepdims=True) for acc in accs]


def _scores_to_bias(sct_ref, bias_ref, score_max, nkb, topk):
    nblk, tk, nq = sct_ref.shape
    kf = jnp.float32(topk)
    total = jnp.asarray(nkb * tk).astype(_F32)

    def mid_of(lo, hi):
        return lo + lax.shift_right_logical(hi - lo, 1)

    def unsettled_rows(lo, cnt_lo, mid):
        return jnp.max(jnp.where((cnt_lo != kf) & (mid != lo), 1, 0))

    def cond(st):
        return (st[5] > 0) & (st[6] < 34)

    def body(st):
        return step(step(st))

    def step(st):
        lo, hi, cnt_lo, cnt_hi, mid, _, it = st
        thr_mid = _key_to_f32(mid)
        cnt = _count_keys(sct_ref, nkb, [lambda x: x >= thr_mid])[0]
        ge = cnt >= kf
        lo = jnp.where(ge, mid, lo)
        cnt_lo = jnp.where(ge, cnt, cnt_lo)
        hi = jnp.where(ge, hi, mid)
        cnt_hi = jnp.where(ge, cnt_hi, cnt)
        mid = mid_of(lo, hi)
        return lo, hi, cnt_lo, cnt_hi, mid, unsettled_rows(lo, cnt_lo, mid), it + 1

    key_p = _f32_to_key(score_max * 0.125)
    key_t = _f32_to_key(score_max)
    probe_p, probe_t = _key_to_f32(key_p), _key_to_f32(key_t)
    c_ge0, c_gt0, c_p, c_t = _count_keys(
        sct_ref, nkb, [lambda x: x >= 0.0, lambda x: x > 0.0, lambda x: x >= probe_p, lambda x: x >= probe_t])
    is_pos = c_gt0 >= kf
    is_zero = c_ge0 >= kf
    p_low = (c_p >= kf) & (key_p > 1)
    p_high = (c_p < kf) & (key_p > 1)
    t_high = (c_t < kf) & (key_t > key_p) & (key_t > 1)
    lo = jnp.where(is_pos, jnp.where(p_low, key_p, 1), jnp.where(is_zero, 0, KEY_NEG_INF))
    cnt_lo = jnp.where(is_pos, jnp.where(p_low, c_p, c_gt0), jnp.where(is_zero, c_ge0, total))
    hi = jnp.where(is_pos, jnp.where(p_high, key_p, jnp.where(t_high, key_t, KEY_POS_INF)),
                   jnp.where(is_zero, 1, -1))
    cnt_hi = jnp.where(is_pos, jnp.where(p_high, c_p, jnp.where(t_high, c_t, 0.0)),
                       jnp.where(is_zero, c_gt0, c_ge0))
    mid = mid_of(lo, hi)
    init = (lo, hi, cnt_lo, cnt_hi, mid, unsettled_rows(lo, cnt_lo, mid), jnp.int32(0))
    init = lax.fori_loop(0, UNTESTED_STEPS // 2, lambda _, st: body(st), init)
    lo, hi, cnt_lo, cnt_hi = lax.while_loop(cond, body, init)[:4]

    thr = _key_to_f32(lo)
    thr = jnp.where(jnp.abs(thr) < F32_MIN_NORMAL, 0.0, thr)
    thr = jnp.maximum(thr, -F32_MAX)
    tie = (cnt_lo > kf) & (lo > KEY_NEG_INF)
    need = jnp.where(lo == 1, -1.0, jnp.where(tie, kf - cnt_hi, 1e9))

    rowi = lax.broadcasted_iota(jnp.int32, (LANES, LANES), 0)
    coli = lax.broadcasted_iota(jnp.int32, (LANES, LANES), 1)
    tri = jnp.where(rowi >= coli, 1.0, 0.0).astype(_BF16)

    def fill(j, base):
        for c in range(tk // LANES):
            x = sct_ref[j, c * LANES:(c + 1) * LANES, :]
            eq = x == thr
            rank = base + _dot(tri, jnp.where(eq, 1.0, 0.0).astype(_BF16))
            tied = jnp.where(eq, jnp.where(rank <= need, 0.0, MASK_BIAS), MASK_BIAS)
            bias_ref[j, :, c * LANES:(c + 1) * LANES] = jnp.where(x > thr, 0.0, tied).T
            base = rank[LANES - 1:LANES, :]
        return base

    _fori_by_pairs(nkb, fill, jnp.zeros((1, nq), _F32))


def _dsa_prompt_kernel(q_ref, qit_ref, wit_ref, kg_ref, kt_ref, v_ref, ki_ref, o_ref, sct_ref, bias_ref, m_ref,
                       acc_ref, *, topk):
    i = pl.program_id(1)
    tq = q_ref.shape[1]
    tk = ki_ref.shape[2]
    nkb = ((i + 1) * tq + tk - 1) // tk

    wit = wit_ref[0]
    q_pos = i * tq + lax.broadcasted_iota(jnp.int32, (1, tq), 1)
    vis_lim = (q_pos // CHUNK + 1) * CHUNK
    key_in_block = lax.broadcasted_iota(jnp.int32, (tk, tq), 0)

    g = COUNT_ACC_ROWS

    def score_body(j, run_max):
        ki = ki_ref[0, j]
        s_all = [_dot(ki, qit_ref[0, h * IDX_DIM:(h + 1) * IDX_DIM, :]) for h in range(IDX_HEADS)]
        acc = jnp.zeros((tk, tq), _F32)
        for h in range(IDX_HEADS):
            acc = acc + wit[h:h + 1, :] * jnp.maximum(s_all[h], 0.0)
        sc = jnp.where((j * tk + key_in_block) < vis_lim, acc, -jnp.inf)
        sct_ref[j] = sc
        return jnp.maximum(run_max, sc.reshape(tk // g, g, tq).max(axis=0))

    run_max = _fori_by_pairs(nkb, score_body, jnp.full((g, tq), -jnp.inf, _F32))
    _scores_to_bias(sct_ref, bias_ref, jnp.max(run_max, axis=0, keepdims=True), nkb, topk)

    reps = tk // LANES
    q_heads = [q_ref[0, :, h * HEAD_DIM:(h + 1) * HEAD_DIM] for h in range(N_HEADS_A)]

    def logits(j, h):
        return _dot(q_heads[h], kt_ref[0, j, h * HEAD_DIM:(h + 1) * HEAD_DIM, :]) + bias_ref[j]

    k_norm = jnp.max(jnp.abs(kg_ref[...])) * (HEAD_DIM ** 0.5)
    shift_max = jnp.float32(0.0)
    for h in range(N_HEADS_A):
        qf = q_heads[h].astype(_F32)
        bound = jnp.sqrt(jnp.sum(qf * qf, axis=1, keepdims=True)) * k_norm
        m_ref[h] = jnp.broadcast_to(bound, (tq, LANES))
        shift_max = jnp.maximum(shift_max, jnp.max(bound))

    @pl.when(shift_max > SHIFT_LIMIT)
    def _():
        m_ref[...] = jnp.full(m_ref.shape, -jnp.inf, _F32)

        def max_body(j, carry):
            for h in range(N_HEADS_A):
                s = logits(j, h)
                part = s[:, :LANES]
                for c in range(1, reps):
                    part = jnp.maximum(part, s[:, c * LANES:(c + 1) * LANES])
                m_ref[h] = jnp.maximum(m_ref[h], part)
            return carry

        lax.fori_loop(0, nkb, max_body, 0)
        for h in range(N_HEADS_A):
            m_ref[h] = jnp.broadcast_to(jnp.max(m_ref[h], axis=1, keepdims=True), (tq, LANES))

    acc_ref[...] = jnp.zeros(acc_ref.shape, _F32)

    def att_body(j, carry):
        s_all = [logits(j, h) for h in range(N_HEADS_A)]
        for h in range(N_HEADS_A):
            p = jnp.exp2(s_all[h] - jnp.concatenate([m_ref[h]] * reps, axis=1))
            acc_ref[h] += _dot(p.astype(_BF16), v_ref[0, j, :, h * LANES:(h + 1) * LANES])
        return carry

    _fori_by_pairs(nkb, att_body, 0)
    lane = lax.broadcasted_iota(jnp.int32, (1, LANES), 1)
    for hp in range(N_HEADS_A // 2):
        a0, a1 = acc_ref[2 * hp], acc_ref[2 * hp + 1]
        out0 = a0 / pltpu.roll(a0, HEAD_DIM, 1)
        out1 = pltpu.roll(a1, HEAD_DIM, 1) / a1
        o_ref[0, :, hp * LANES:(hp + 1) * LANES] = jnp.where(lane < HEAD_DIM, out0, out1).astype(o_ref.dtype)


def _dsa_prompt_call(q_bf, qit_bf, wit, kg, kt_bf, v_bf, ki_bf, *, topk):
    b, t, _ = q_bf.shape
    nt, tk = kt_bf.shape[1], kt_bf.shape[3]
    tq = min(DSA_TQ, t)
    v4 = v_bf.reshape(b, nt, tk, N_HEADS_A * LANES)
    ki4 = ki_bf.reshape(b, nt, tk, IDX_DIM)
    resident = lambda shape: pl.BlockSpec((1,) + shape, lambda bi, i: (bi, 0, 0, 0), pipeline_mode=pl.Buffered(1))
    return pl.pallas_call(
        functools.partial(_dsa_prompt_kernel, topk=topk),
        grid=(b, t // tq),
        in_specs=[
            pl.BlockSpec((1, tq, WIDTH_A), lambda bi, i: (bi, i, 0)),
            pl.BlockSpec((1, IDX_HEADS * IDX_DIM, tq), lambda bi, i: (bi, 0, i)),
            pl.BlockSpec((1, 8, tq), lambda bi, i: (bi, 0, i)),
            pl.BlockSpec((1, WIDTH_A), lambda bi, i: (0, 0)),
            resident((nt, WIDTH_A, tk)),
            resident((nt, tk, N_HEADS_A * LANES)),
            resident((nt, tk, IDX_DIM)),
        ],
        out_specs=pl.BlockSpec((1, tq, WIDTH_A), lambda bi, i: (bi, i, 0)),
        out_shape=jax.ShapeDtypeStruct((b, t, WIDTH_A), _BF16),
        scratch_shapes=[
            pltpu.VMEM((nt, tk, tq), _F32),
            pltpu.VMEM((nt, tq, tk), _F32),
            pltpu.VMEM((N_HEADS_A, tq, LANES), _F32),
            pltpu.VMEM((N_HEADS_A, tq, LANES), _F32),
        ],
        compiler_params=pltpu.CompilerParams(
            dimension_semantics=("arbitrary", "arbitrary"), vmem_limit_bytes=VMEM_LIMIT),
        name="dsa_prompt",
    )(q_bf, qit_bf, wit, kg, kt_bf, v4, ki4)


def _new_cols(x, width):
    return jnp.concatenate([x, jnp.zeros((width - x.shape[0], x.shape[1]), _F32)], axis=0).T


def _idx_sample_kernel(qi_ref, wi_ref, kin_ref, ckit_ref, o_ref, sct_ref, bias_ref, kit_ref, *, topk):
    g, t = qi_ref.shape[0], qi_ref.shape[1]
    past = ckit_ref.shape[2]
    s_pad, nq = sct_ref.shape[1], sct_ref.shape[2]
    scores = []
    for b in range(g):
        kit_ref[b, :, :past] = ckit_ref[b].astype(_BF16)
        kin = jnp.concatenate([kin_ref[b], jnp.zeros((t, LANES - IDX_DIM), _F32)], axis=1)
        kit_ref[b, :, past:] = _new_cols(kin, s_pad - past)[:IDX_DIM].astype(_BF16)
        qi = qi_ref[b]
        wi = wi_ref[b]
        qi_stack = jnp.concatenate([qi[:, h * IDX_DIM:(h + 1) * IDX_DIM] for h in range(IDX_HEADS)], axis=0)
        s_idx = _dot(qi_stack, kit_ref[b])
        score = jnp.zeros((t, s_pad), _F32)
        for h in range(IDX_HEADS):
            score = score + wi[:, h:h + 1] * jnp.maximum(s_idx[h * t:(h + 1) * t, :], 0.0)
        scores.append(score)
    key_pos = lax.broadcasted_iota(jnp.int32, (s_pad, nq), 0)
    sc = jnp.where(key_pos < past + t, jnp.concatenate(scores, axis=0).T, -jnp.inf)
    sct_ref[0] = sc
    _scores_to_bias(sct_ref, bias_ref, jnp.max(sc, axis=0, keepdims=True), 1, topk)
    for b in range(g):
        o_ref[b] = bias_ref[0, b * t:(b + 1) * t, :]


def _idx_sample_call(qi_bf, wi, ki_new, cache_kit, *, topk):
    b, t, _ = qi_bf.shape
    past = cache_kit.shape[2]
    s_pad = past + LANES
    assert LANES % t == 0 and b % (LANES // t) == 0
    g = LANES // t
    spec = lambda r, c: pl.BlockSpec((g, r, c), lambda i: (i, 0, 0))
    return pl.pallas_call(
        functools.partial(_idx_sample_kernel, topk=topk),
        grid=(b // g,),
        in_specs=[spec(t, IDX_HEADS * IDX_DIM), spec(t, LANES), spec(t, IDX_DIM), spec(IDX_DIM, past)],
        out_specs=spec(t, s_pad),
        out_shape=jax.ShapeDtypeStruct((b, t, s_pad), _F32),
        scratch_shapes=[
            pltpu.VMEM((1, s_pad, LANES), _F32),
            pltpu.VMEM((1, LANES, s_pad), _F32),
            pltpu.VMEM((g, IDX_DIM, s_pad), _BF16),
        ],
        compiler_params=pltpu.CompilerParams(dimension_semantics=("arbitrary",), vmem_limit_bytes=VMEM_LIMIT),
        name="idx_sample",
    )(qi_bf, wi, ki_new, cache_kit)


def _dsa_sample_kernel(q_ref, bias_ref, kn_ref, vn_ref, ckt_ref, cvt_ref, o_ref, kt_ref, vt_ref):
    t = q_ref.shape[1]
    past = ckt_ref.shape[2]
    s_pad = kt_ref.shape[1]

    kt_ref[:, :past] = ckt_ref[0].astype(_BF16)
    vt_ref[:, :past] = cvt_ref[0].astype(_BF16)
    kt_ref[:, past:] = _new_cols(kn_ref[0], s_pad - past).astype(_BF16)
    vt_ref[:, past:] = _new_cols(vn_ref[0], s_pad - past).astype(_BF16)

    nrow = N_HEADS_A * t
    row_head = lax.broadcasted_iota(jnp.int32, (nrow, WIDTH_A), 0) // t
    col_head = lax.broadcasted_iota(jnp.int32, (nrow, WIDTH_A), 1) // HEAD_DIM
    own = row_head == col_head
    q_rep = jnp.concatenate([q_ref[0]] * N_HEADS_A, axis=0)
    q_bd = jnp.where(own, q_rep, jnp.zeros_like(q_rep))
    s = _dot(q_bd, kt_ref[...]) + jnp.concatenate([bias_ref[0]] * N_HEADS_A, axis=0)
    m = jnp.max(s, axis=1, keepdims=True)
    p = jnp.exp2(s - m)
    l = jnp.sum(p, axis=1, keepdims=True)
    pv = _dot_nt(p.astype(_BF16), vt_ref[...]) / l
    pv = jnp.where(own, pv, 0.0)
    out = pv[:t]
    for h in range(1, N_HEADS_A):
        out = out + pv[h * t:(h + 1) * t]
    o_ref[0] = out.astype(o_ref.dtype)


def _dsa_sample_call(q_bf, bias, k_new, v_new, cache_kt, cache_vt):
    b, t, _ = q_bf.shape
    past = cache_kt.shape[2]
    s_pad = bias.shape[2]
    row = lambda c: pl.BlockSpec((1, t, c), lambda bi: (bi, 0, 0))
    cache = lambda c: pl.BlockSpec((1, c, past), lambda bi: (bi, 0, 0))
    return pl.pallas_call(
        _dsa_sample_kernel,
        grid=(b,),
        in_specs=[row(WIDTH_A), row(s_pad), row(WIDTH_A), row(WIDTH_A), cache(WIDTH_A), cache(WIDTH_A)],
        out_specs=row(WIDTH_A),
        out_shape=jax.ShapeDtypeStruct((b, t, WIDTH_A), _BF16),
        scratch_shapes=[
            pltpu.VMEM((WIDTH_A, s_pad), _BF16),
            pltpu.VMEM((WIDTH_A, s_pad), _BF16),
        ],
        compiler_params=pltpu.CompilerParams(dimension_semantics=("arbitrary",), vmem_limit_bytes=VMEM_LIMIT),
        name="dsa_sample",
    )(q_bf, bias, k_new, v_new, cache_kt, cache_vt)


def _out_kernel(x_ref, at_ref, gm_ref, g1_ref, sh2_ref, sc2_ref, g2_ref, ng_ref, wo_ref, w1_ref, w2_ref, y_ref):
    nb, r, d = x_ref.shape
    m = nb * r
    at = at_ref[...].reshape(m, WIDTH_A)
    gm = gm_ref[...].reshape(m, WIDTH_B)
    y = _dot(at, wo_ref[:WIDTH_A, :]) + _dot(gm, wo_ref[WIDTH_A:, :])
    x1 = x_ref[...] + g1_ref[...] * y.reshape(nb, r, d)
    ms = jnp.mean(x1 * x1, axis=-1, keepdims=True)
    h2 = (x1 * lax.rsqrt(ms + EPS) * ng_ref[...]) * (1.0 + sc2_ref[...]) + sh2_ref[...]
    h2 = h2.reshape(m, d).astype(_BF16)
    dff = w1_ref.shape[1]
    fc = 1024
    ff = jnp.zeros((m, d), _F32)
    for c in range(dff // fc):
        a = jnp.maximum(_dot(h2, w1_ref[:, c * fc:(c + 1) * fc]), 0.0)
        ff = ff + _dot((a * a).astype(_BF16), w2_ref[c * fc:(c + 1) * fc, :])
    y_ref[...] = x1 + g2_ref[...] * ff.reshape(nb, r, d)


def _out_call(x3, attn, gm, mod3, norm_g, w_out, w_ff1, w_ff2, *, nb, r, name):
    nseq, t, d = x3.shape
    const2 = lambda i, j: (0, 0)
    row_spec = lambda c: pl.BlockSpec((nb, r, c), lambda i, j: (i, j, 0))
    mod_spec = lambda col: pl.BlockSpec((nb, 1, d), lambda i, j: (i, 0, col))
    return pl.pallas_call(
        _out_kernel,
        grid=(nseq // nb, t // r),
        in_specs=[row_spec(d), row_spec(WIDTH_A), row_spec(WIDTH_B),
                  mod_spec(2), mod_spec(3), mod_spec(4), mod_spec(5),
                  pl.BlockSpec((1, d), const2),
                  pl.BlockSpec(w_out.shape, const2, pipeline_mode=pl.Buffered(1)),
                  pl.BlockSpec(w_ff1.shape, const2, pipeline_mode=pl.Buffered(1)),
                  pl.BlockSpec(w_ff2.shape, const2, pipeline_mode=pl.Buffered(1))],
        out_specs=row_spec(d),
        out_shape=jax.ShapeDtypeStruct((nseq, t, d), _F32),
        compiler_params=pltpu.CompilerParams(
            dimension_semantics=("arbitrary", "arbitrary"), vmem_limit_bytes=VMEM_LIMIT),
        name=name,
    )(x3, attn, gm, mod3, mod3, mod3, mod3, norm_g, w_out, w_ff1, w_ff2)


def _rope_tables(pos):
    half = HEAD_DIM // 2
    inv_freq = jnp.power(jnp.float32(ROPE_THETA), -jnp.arange(half, dtype=_F32) / half)
    ang = pos.astype(_F32)[:, None] * inv_freq[None, :]
    cos, sin = jnp.cos(ang), jnp.sin(ang)
    reps = LANES // HEAD_DIM
    cos_t = jnp.tile(jnp.concatenate([cos, cos], axis=1), (1, reps))
    sin_t = jnp.tile(jnp.concatenate([-sin, sin], axis=1), (1, reps))
    return cos_t[None], sin_t[None]


def _pad_w_in(w_in):
    d = w_in.shape[0]
    a, i = WIDTH_A, IDX_HEADS * IDX_DIM
    o_ki = 3 * a + i
    o_wi = o_ki + IDX_DIM
    o_u = o_wi + IDX_HEADS
    z = lambda n: jnp.zeros((d, n), w_in.dtype)
    cols = [w_in[:, :o_ki], w_in[:, o_ki:o_wi], z(KI_PAD - IDX_DIM), w_in[:, o_wi:o_u], z(WI_PAD - IDX_HEADS),
            w_in[:, o_u:]]
    return jnp.concatenate(cols, axis=1).astype(_BF16)


def _layer(x_prompt, x_sample, c_prompt, c_sample, cache_k, cache_v, cache_kidx, w_ada, b_ada, norm1_g, norm2_g,
           w_in, q_norm_g, k_norm_g, gmlp_ln_g, gmlp_ln_b, gmlp_ws, gmlp_bs, w_out, w_ff1, w_ff2):
    bp, tp, d = x_prompt.shape
    bs_, ts, _ = x_sample.shape
    past = cache_k.shape[1]

    c_all = jnp.concatenate([c_prompt, c_sample], axis=0)
    rows = -(-c_all.shape[0] // 16) * 16
    c_all = jnp.pad(c_all, ((0, rows - c_all.shape[0]), (0, 0)))
    mod = _ada_call(c_all, w_ada, b_ada[None, :])
    mod_p = mod[:bp, None, :]
    mod_s = mod[bp:bp + bs_, None, :]

    w_in_p = _pad_w_in(w_in)
    w_out_b, w_ff1_b, w_ff2_b = w_out.astype(_BF16), w_ff1.astype(_BF16), w_ff2.astype(_BF16)
    head_of = jnp.arange(WIDTH_A) // HEAD_DIM
    bd = (head_of[:, None] == head_of[None, :]).astype(_BF16)
    qg = jnp.tile(q_norm_g, N_HEADS_A)[None, :]
    kg = jnp.tile(k_norm_g, N_HEADS_A)[None, :]
    lng, lnb = gmlp_ln_g[None, :], gmlp_ln_b[None, :]
    n1, n2 = norm1_g[None, :], norm2_g[None, :]

    rp = min(PROJ_ROWS, tp)
    cos_p, sin_p = _rope_tables(jnp.arange(tp))
    lp = min(tp, GMLP_CHUNK)
    bs_p = jnp.repeat(jnp.transpose(gmlp_bs[:, :lp]), GROUP_DIM_B, axis=1)
    (kp, vp, kip, q_bf, gm_p, kt_bf, v_bf, ki_bf, qit_bf, wit) = _proj_call(
        x_prompt, mod_p, n1, w_in_p, cos_p, sin_p, bd, qg, kg, lng, lnb, gmlp_ws[:, :lp, :lp], bs_p,
        nb=1, r=rp, prompt=True)
    attn_p = _dsa_prompt_call(q_bf, qit_bf, wit, kg, kt_bf, v_bf, ki_bf, topk=min(TOPK_MAX, tp // 4))
    yp = _out_call(x_prompt, attn_p, gm_p, mod_p, n2, w_out_b, w_ff1_b, w_ff2_b, nb=1, r=rp, name="out_prompt")

    nb = min(SAMPLE_GROUP, bs_)
    cos_s, sin_s = _rope_tables(past + jnp.arange(ts))
    ls = min(ts, GMLP_CHUNK)
    assert ls == ts
    eye = jnp.eye(nb, dtype=gmlp_ws.dtype)
    ws_s = jax.vmap(lambda w: jnp.kron(eye, w))(gmlp_ws[:, :ls, :ls])
    bs_s = jnp.tile(jnp.repeat(jnp.transpose(gmlp_bs[:, :ls]), GROUP_DIM_B, axis=1), (nb, 1))
    (ks, vs, kis, qs_bf, gm_s, qis_bf, wi_s, gvs) = _proj_call(
        x_sample, mod_s, n1, w_in_p, cos_s, sin_s, bd, qg, kg, lng, lnb, ws_s, bs_s,
        nb=nb, r=ts, prompt=False)
    cache_kt = jnp.transpose(cache_k, (0, 2, 3, 1)).reshape(bs_, WIDTH_A, past)
    cache_vt = jnp.transpose(cache_v, (0, 2, 3, 1)).reshape(bs_, WIDTH_A, past)
    cache_kit = jnp.transpose(cache_kidx, (0, 2, 1))
    bias_s = _idx_sample_call(qis_bf, wi_s, kis, cache_kit, topk=min(TOPK_MAX, (past + ts) // 4))
    attn_s = _dsa_sample_call(qs_bf, bias_s, ks, vs, cache_kt, cache_vt)
    ys = _out_call(x_sample, attn_s, gm_s, mod_s, n2, w_out_b, w_ff1_b, w_ff2_b, nb=nb, r=ts, name="out_sample")

    heads = lambda a: a.reshape(a.shape[0], a.shape[1], N_HEADS_A, HEAD_DIM)
    heads_t = lambda a: jnp.transpose(a.reshape(a.shape[0], N_HEADS_A, HEAD_DIM, a.shape[2]), (0, 3, 1, 2))
    return yp, ys, heads_t(kp), heads_t(vp), jnp.transpose(kip, (0, 2, 1)), heads(ks), heads(vs), kis, gvs


def kernel(x_prompt, x_sample, c_prompt, c_sample, cache_k, cache_v, cache_kidx, w_ada, b_ada, norm1_g, norm2_g,
           w_in, q_norm_g, k_norm_g, gmlp_ln_g, gmlp_ln_b, gmlp_ws, gmlp_bs, w_out, w_ff1, w_ff2):
    depth = w_ada.shape[0]
    yp, ys = x_prompt, x_sample
    outs = [[] for _ in range(7)]
    for l in range(depth):
        res = _layer(yp, ys, c_prompt, c_sample, cache_k[l], cache_v[l], cache_kidx[l], w_ada[l], b_ada[l],
                     norm1_g[l], norm2_g[l], w_in[l], q_norm_g[l], k_norm_g[l], gmlp_ln_g[l], gmlp_ln_b[l],
                     gmlp_ws[l], gmlp_bs[l], w_out[l], w_ff1[l], w_ff2[l])
        yp, ys = res[0], res[1]
        for acc, leaf in zip(outs, res[2:]):
            acc.append(leaf)
    return (yp, ys) + tuple(jnp.stack(o) for o in outs)
```
